```python
import jax, jax.numpy as jnp
from jax import lax
import numpy as np

D_MODEL = 1024
BATCH = 8
SEQ = 2048
DEPTH = 1
DEC_BATCH = 128
DEC_SEQ = 8
PAST_LEN = 16384
PAGE_SIZE = 128

N_MEM = 256
GLA_HEADS = 4
GLA_DK = D_MODEL // 2 // GLA_HEADS
GLA_DV = D_MODEL // GLA_HEADS
GLA_QK = GLA_HEADS * GLA_DK
GLA_V = GLA_HEADS * GLA_DV
GLA_GATE_RANK = 16
GLA_TAU = 16.0
RET_HEADS = 4
RET_DK = D_MODEL // 2 // RET_HEADS
RET_DV = D_MODEL // RET_HEADS
RET_QK = RET_HEADS * RET_DK
RET_V = RET_HEADS * RET_DV
XA_HEADS = 4
XA_DH = D_MODEL // XA_HEADS
D_FF = 4 * D_MODEL
CHUNK = 64
ROPE_BASE = 10000.0
EPS = 1e-6
IN_WIDTHS = (GLA_QK, GLA_QK, GLA_V, GLA_V, GLA_GATE_RANK, RET_QK, RET_QK, RET_V, RET_V, D_MODEL, D_MODEL)
N_IN = 2 * GLA_QK + 2 * GLA_V + GLA_GATE_RANK + 2 * RET_QK + 2 * RET_V + 2 * D_MODEL

kernel_name = "gla_retnet_parallel_memxattn_decoder_step"


def rmsnorm(x, g):
    xf = x.astype(jnp.float32)
    y = xf * lax.rsqrt(jnp.mean(xf * xf, axis=-1, keepdims=True) + EPS)
    return (y * g.astype(jnp.float32)).astype(x.dtype)


def groupnorm(x, g):
    xf = x.astype(jnp.float32)
    mu = jnp.mean(xf, axis=-1, keepdims=True)
    xc = xf - mu
    y = xc * lax.rsqrt(jnp.mean(xc * xc, axis=-1, keepdims=True) + EPS)
    return (y * g.astype(jnp.float32)).astype(x.dtype)


def rotary(x, pos):
    half = x.shape[-1] // 2
    inv = ROPE_BASE ** (-jnp.arange(half, dtype=jnp.float32) / half)
    ang = pos.astype(jnp.float32)[:, None] * inv[None, :]
    cos = jnp.cos(ang)[None, :, None, :]
    sin = jnp.sin(ang)[None, :, None, :]
    xf = x.astype(jnp.float32)
    x1, x2 = xf[..., :half], xf[..., half:]
    return jnp.concatenate([x1 * cos - x2 * sin, x1 * sin + x2 * cos], axis=-1).astype(x.dtype)


def chunked_decay_linear_attn(q, k, v, log_a, s0):
    B, H, L, dk = q.shape
    c = min(CHUNK, L)
    n = -(-L // c)
    pad = n * c - L

    def prep(t):
        t = t.astype(jnp.float32)
        t = jnp.pad(t, ((0, 0), (0, 0), (0, pad), (0, 0)))
        t = t.reshape(B, H, n, c, t.shape[-1])
        return jnp.moveaxis(t, 2, 0)

    qs, ks, vs, as_ = prep(q), prep(k), prep(v), prep(log_a)
    mask = jnp.tril(jnp.ones((c, c), dtype=bool))

    def step(S, inp):
        qc, kc, vc, ac = inp
        b = jnp.cumsum(ac, axis=2)
        qe = qc * jnp.exp(b)
        ke = kc * jnp.exp(-b)
        scores = jnp.where(mask, jnp.einsum('bhtd,bhsd->bhts', qe, ke), 0.0)
        o = jnp.einsum('bhts,bhsv->bhtv', scores, vc) + jnp.einsum('bhtd,bhdv->bhtv', qe, S)
        b_last = b[:, :, -1:, :]
        kd = kc * jnp.exp(b_last - b)
        S_new = jnp.exp(b_last)[:, :, 0, :, None] * S + jnp.einsum('bhsd,bhsv->bhdv', kd, vc)
        return S_new, o

    S_fin, os_ = lax.scan(step, s0.astype(jnp.float32), (qs, ks, vs, as_))
    o = jnp.moveaxis(os_, 0, 2).reshape(B, H, n * c, -1)[:, :, :L]
    return o, S_fin.astype(s0.dtype)


def memory_kv(mem, g_mem, w_xk, w_xv):
    B = mem.shape[0]
    m = rmsnorm(mem, g_mem)
    k = (m @ w_xk).reshape(B, N_MEM, XA_HEADS, XA_DH)
    v = (m @ w_xv).reshape(B, N_MEM, XA_HEADS, XA_DH)
    return k, v


def mixer_block(h, pos, s_gla, s_ret, w_in, w_gla_a2, b_gla_a, g_gla_head, g_ret_head, w_mix_out):
    B, L, _ = h.shape
    z = h @ w_in
    (gq, gk, gv, gr, ga_low, rq, rk, rv, rg, gate_a, gate_b) = jnp.split(
        z, [int(s) for s in np.cumsum(IN_WIDTHS)[:-1]], axis=-1)

    def to_heads(t, H):
        return t.reshape(B, L, H, -1).transpose(0, 2, 1, 3)

    log_alpha = jax.nn.log_sigmoid((ga_low @ w_gla_a2 + b_gla_a).astype(jnp.float32)) / GLA_TAU
    o_g, s_gla_new = chunked_decay_linear_attn(
        to_heads(gq, GLA_HEADS) * (GLA_DK ** -0.5), to_heads(gk, GLA_HEADS), to_heads(gv, GLA_HEADS),
        to_heads(log_alpha, GLA_HEADS), s_gla)
    o_g = rmsnorm(o_g.transpose(0, 2, 1, 3).astype(h.dtype), g_gla_head).reshape(B, L, GLA_V)
    o_g = o_g * jax.nn.silu(gr)

    rq_h = rotary(rq.reshape(B, L, RET_HEADS, RET_DK), pos).transpose(0, 2, 1, 3)
    rk_h = rotary(rk.reshape(B, L, RET_HEADS, RET_DK), pos).transpose(0, 2, 1, 3) * (RET_DK ** -0.5)
    log_gamma = jnp.log1p(-jnp.exp2(-5.0 - jnp.arange(RET_HEADS, dtype=jnp.float32)))
    log_dec = jnp.broadcast_to(log_gamma[None, :, None, None], (B, RET_HEADS, L, 1))
    o_r, s_ret_new = chunked_decay_linear_attn(rq_h, rk_h, to_heads(rv, RET_HEADS), log_dec, s_ret)
    o_r = groupnorm(o_r.transpose(0, 2, 1, 3).astype(h.dtype), g_ret_head).reshape(B, L, RET_V)
    o_r = o_r * jax.nn.silu(rg)

    merged = jax.nn.sigmoid(gate_a) * o_g + jax.nn.sigmoid(gate_b) * o_r
    return merged @ w_mix_out, s_gla_new, s_ret_new


def cross_attn(h, mem_k, mem_v, w_xq, w_xo):
    B, L, _ = h.shape
    q = (h @ w_xq).reshape(B, L, XA_HEADS, XA_DH)
    s = jnp.einsum('blhd,bmhd->bhlm', q, mem_k).astype(jnp.float32) * (XA_DH ** -0.5)
    p = jax.nn.softmax(s, axis=-1).astype(h.dtype)
    o = jnp.einsum('bhlm,bmhd->blhd', p, mem_v).reshape(B, L, D_MODEL)
    return o @ w_xo


def layer(x, pos, s_gla, s_ret, mem_k, mem_v, w_in, w_gla_a2, b_gla_a, g_gla_head, g_ret_head, w_mix_out,
          w_xq, w_xo, w_up, w_down, g_pre_mix, g_post_mix, g_pre_xa, g_post_xa, g_pre_ffn, g_post_ffn):
    m, s_gla_new, s_ret_new = mixer_block(rmsnorm(x, g_pre_mix), pos, s_gla, s_ret, w_in, w_gla_a2, b_gla_a,
                                          g_gla_head, g_ret_head, w_mix_out)
    x = x + rmsnorm(m, g_post_mix)
    x = x + rmsnorm(cross_attn(rmsnorm(x, g_pre_xa), mem_k, mem_v, w_xq, w_xo), g_post_xa)
    hf = rmsnorm(x, g_pre_ffn)
    x = x + rmsnorm(jnp.square(jax.nn.relu(hf @ w_up)) @ w_down, g_post_ffn)
    return x, s_gla_new, s_ret_new


def setup_inputs(seed: int = 0) -> dict:
    key = jax.random.key(seed)
    ks = jax.random.split(key, 32)
    f32 = jnp.float32

    def w(k, shape, fan_in):
        return jax.random.normal(k, shape, f32) * (fan_in ** -0.5)

    def gain(k, shape):
        return 1.0 + 0.01 * jax.random.normal(k, shape, f32)

    return {
        "x_prompt": jax.random.normal(ks[0], (BATCH, SEQ, D_MODEL), f32),
        "x_sample": jax.random.normal(ks[1], (DEC_BATCH, DEC_SEQ, D_MODEL), f32),
        "state_gla": 0.1 * jax.random.normal(ks[2], (DEPTH, DEC_BATCH, GLA_HEADS, GLA_DK, GLA_DV), f32),
        "state_ret": 0.1 * jax.random.normal(ks[3], (DEPTH, DEC_BATCH, RET_HEADS, RET_DK, RET_DV), f32),
        "cache_mem_k": jax.random.normal(ks[4], (DEPTH, DEC_BATCH, N_MEM, XA_HEADS, XA_DH), f32),
        "cache_mem_v": jax.random.normal(ks[5], (DEPTH, DEC_BATCH, N_MEM, XA_HEADS, XA_DH), f32),
        "mem_prompt": jax.random.normal(ks[6], (BATCH, N_MEM, D_MODEL), f32),
        "w_in": w(ks[7], (DEPTH, D_MODEL, N_IN), D_MODEL),
        "w_gla_a2": w(ks[8], (DEPTH, GLA_GATE_RANK, GLA_QK), GLA_GATE_RANK),
        "b_gla_a": 0.01 * jax.random.normal(ks[9], (DEPTH, GLA_QK), f32),
        "g_gla_head": gain(ks[10], (DEPTH, GLA_HEADS, GLA_DV)),
        "g_ret_head": gain(ks[11], (DEPTH, RET_HEADS, RET_DV)),
        "w_mix_out": w(ks[12], (DEPTH, D_MODEL, D_MODEL), D_MODEL),
        "w_xq": w(ks[13], (DEPTH, D_MODEL, D_MODEL), D_MODEL),
        "w_xk": w(ks[14], (DEPTH, D_MODEL, D_MODEL), D_MODEL),
        "w_xv": w(ks[15], (DEPTH, D_MODEL, D_MODEL), D_MODEL),
        "w_xo": w(ks[16], (DEPTH, D_MODEL, D_MODEL), D_MODEL),
        "g_mem": gain(ks[17], (DEPTH, D_MODEL)),
        "w_up": w(ks[18], (DEPTH, D_MODEL, D_FF), D_MODEL),
        "w_down": w(ks[19], (DEPTH, D_FF, D_MODEL), D_FF),
        "g_pre_mix": gain(ks[20], (DEPTH, D_MODEL)),
        "g_post_mix": gain(ks[21], (DEPTH, D_MODEL)),
        "g_pre_xa": gain(ks[22], (DEPTH, D_MODEL)),
        "g_post_xa": gain(ks[23], (DEPTH, D_MODEL)),
        "g_pre_ffn": gain(ks[24], (DEPTH, D_MODEL)),
        "g_post_ffn": gain(ks[25], (DEPTH, D_MODEL)),
    }


def reference(x_prompt, x_sample, state_gla, state_ret, cache_mem_k, cache_mem_v, mem_prompt,
              w_in, w_gla_a2, b_gla_a, g_gla_head, g_ret_head, w_mix_out,
              w_xq, w_xk, w_xv, w_xo, g_mem, w_up, w_down,
              g_pre_mix, g_post_mix, g_pre_xa, g_post_xa, g_pre_ffn, g_post_ffn):
    Bp, Lp, _ = x_prompt.shape
    pos_p = jnp.arange(Lp, dtype=jnp.int32)
    pos_s = PAST_LEN + jnp.arange(x_sample.shape[1], dtype=jnp.int32)
    zero_gla = jnp.zeros((Bp, GLA_HEADS, GLA_DK, GLA_DV), jnp.float32)
    zero_ret = jnp.zeros((Bp, RET_HEADS, RET_DK, RET_DV), jnp.float32)

    yp, ys = x_prompt, x_sample
    gla_p, ret_p, mk_p, mv_p, gla_s, ret_s = [], [], [], [], [], []
    for l in range(DEPTH):
        shared = (w_in[l], w_gla_a2[l], b_gla_a[l], g_gla_head[l], g_ret_head[l], w_mix_out[l],
                  w_xq[l], w_xo[l], w_up[l], w_down[l],
                  g_pre_mix[l], g_post_mix[l], g_pre_xa[l], g_post_xa[l], g_pre_ffn[l], g_post_ffn[l])
        mk, mv = memory_kv(mem_prompt, g_mem[l], w_xk[l], w_xv[l])
        yp, sg, sr = layer(yp, pos_p, zero_gla, zero_ret, mk, mv, *shared)
        gla_p.append(sg); ret_p.append(sr); mk_p.append(mk); mv_p.append(mv)
        ys, sg2, sr2 = layer(ys, pos_s, state_gla[l], state_ret[l], cache_mem_k[l], cache_mem_v[l], *shared)
        gla_s.append(sg2); ret_s.append(sr2)

    new_state_gla_prompt = jnp.stack(gla_p)
    new_state_ret_prompt = jnp.stack(ret_p)
    new_cache_mem_k_prompt = jnp.stack(mk_p)
    new_cache_mem_v_prompt = jnp.stack(mv_p)
    new_state_gla_sample = jnp.stack(gla_s)
    new_state_ret_sample = jnp.stack(ret_s)
    return (yp, ys, new_state_gla_prompt, new_state_ret_prompt, new_cache_mem_k_prompt, new_cache_mem_v_prompt,
            new_state_gla_sample, new_state_ret_sample)
```

```python
import functools
import math

import jax
import jax.numpy as jnp
from jax import lax
from jax.experimental import pallas as pl
from jax.experimental.pallas import tpu as pltpu

F32 = jnp.float32
BF16 = jnp.bfloat16

D_MODEL = 1024
N_HEADS = 4
D_K = 128
D_V = 256
N_MEM = 256
XA_DH = 256
D_FF = 4 * D_MODEL
GATE_RANK = 16
GATE_RANK_PAD = 128
GLA_TAU = 16.0
CHUNK = 64
ROPE_BASE = 10000.0
PAST_LEN = 16384
EPS = 1e-6
LOG_GAMMA = tuple(math.log1p(-(2.0 ** (-5.0 - h))) for h in range(N_HEADS))

C_GQ, C_GK, C_GV, C_GR = 0, 512, 1024, 2048
C_RQ, C_RK, C_RV, C_RG = 3072, 3584, 4096, 5120
C_GA, C_GB = 6144, 7168
N_MAIN = 8192
N_QK = N_HEADS * D_K

VMEM_LIMIT_BYTES = 56 * 1024 * 1024


def _mm(a, b):
    return jnp.dot(a, b, preferred_element_type=F32)


def _mm_nt(a, b):
    return lax.dot_general(a, b, (((1,), (1,)), ((), ())), preferred_element_type=F32)


def _mm_tn(a, b):
    return lax.dot_general(a, b, (((0,), (0,)), ((), ())), preferred_element_type=F32)


def _rms(x, g):
    return x * lax.rsqrt(jnp.mean(x * x, axis=-1, keepdims=True) + EPS) * g


def _sigmoid(x):
    return 1.0 / (1.0 + jnp.exp(-x))


def _const_spec(shape):
    nd = len(shape)
    return pl.BlockSpec(shape, lambda *_: (0,) * nd, pipeline_mode=pl.Buffered(1))


def _params(sem):
    return pltpu.CompilerParams(dimension_semantics=sem, vmem_limit_bytes=VMEM_LIMIT_BYTES)


def _inproj(h_bf, wmain_ref, wga_ref, w2_ref, b2_ref, z_ref, la_ref):
    for j in range(0, N_MAIN, 1024):
        z_ref[:, j:j + 1024] = _mm(h_bf, wmain_ref[:, j:j + 1024])
    ga = _mm(h_bf, wga_ref[...])
    xg = _mm(ga.astype(BF16), w2_ref[...]) + b2_ref[...]
    log_sig = jnp.minimum(xg, 0.0) - jnp.log1p(jnp.exp(-jnp.abs(xg)))
    la_ref[...] = log_sig * (1.0 / GLA_TAU)


def _chunk_consts(c):
    row = lax.broadcasted_iota(jnp.int32, (c, c), 0)
    col = lax.broadcasted_iota(jnp.int32, (c, c), 1)
    causal = row >= col
    tri = jnp.where(causal, 1.0, 0.0).astype(BF16)
    dist = (row - col).astype(F32)
    tcol = lax.broadcasted_iota(jnp.int32, (c, 1), 0).astype(F32)
    dmat, qdec, kdec = [], [], []
    for h in range(N_HEADS):
        lg = LOG_GAMMA[h]
        dmat.append(jnp.where(causal, jnp.exp(lg * dist), 0.0))
        qdec.append(jnp.exp(lg * (tcol + 1.0)))
        kdec.append(jnp.exp(lg * (float(c - 1) - tcol)))
    return causal, tri, dmat, qdec, kdec


def _gla_chunk(z_ref, la_ref, rows, c, s_in, s_out, g_ref, o_ref, causal, tri):
    la = la_ref[rows, :]
    la_hi = la.astype(BF16)
    la_lo = (la - la_hi.astype(F32)).astype(BF16)
    b = _mm(tri, la_hi) + _mm(tri, la_lo)
    b_last = b[c - 1:c, :]
    eb = jnp.exp(b)
    enb = jnp.exp(-b)
    ekd = jnp.exp(b_last - b)
    dec = jnp.exp(b_last)
    for h in range(N_HEADS):
        ks = slice(h * D_K, (h + 1) * D_K)
        q = z_ref[rows, C_GQ + h * D_K:C_GQ + (h + 1) * D_K] * (D_K ** -0.5)
        k = z_ref[rows, C_GK + h * D_K:C_GK + (h + 1) * D_K]
        v = z_ref[rows, C_GV + h * D_V:C_GV + (h + 1) * D_V].astype(BF16)
        qe = (q * eb[:, ks]).astype(BF16)
        ke = (k * enb[:, ks]).astype(BF16)
        kd = (k * ekd[:, ks]).astype(BF16)
        sc = jnp.where(causal, _mm_nt(qe, ke), 0.0).astype(BF16)
        s_old = s_in[h]
        o = _mm(sc, v) + _mm(qe, s_old.astype(BF16))
        dcol = jnp.transpose(jnp.broadcast_to(dec[:, ks], (D_K, D_K)))
        s_out[h] = jnp.concatenate([dcol, dcol], axis=1) * s_old + _mm_tn(kd, v)
        ms = jnp.mean(o * o, axis=-1, keepdims=True)
        o_ref[rows, h * D_V:(h + 1) * D_V] = o * lax.rsqrt(ms + EPS) * g_ref[:, h * D_V:(h + 1) * D_V]


def _ret_chunk(z_ref, rows, c, cos, sin, s_in, s_out, g_ref, o_ref, dmat, qdec, kdec):
    for h in range(N_HEADS):
        q = z_ref[rows, C_RQ + h * D_K:C_RQ + (h + 1) * D_K]
        k = z_ref[rows, C_RK + h * D_K:C_RK + (h + 1) * D_K]
        v = z_ref[rows, C_RV + h * D_V:C_RV + (h + 1) * D_V].astype(BF16)
        q = q * cos + pltpu.roll(q, D_K // 2, 1) * sin
        k = (k * cos + pltpu.roll(k, D_K // 2, 1) * sin) * (D_K ** -0.5)
        sc = (_mm_nt(q.astype(BF16), k.astype(BF16)) * dmat[h]).astype(BF16)
        s_old = s_in[h]
        o = _mm(sc, v) + _mm((q * qdec[h]).astype(BF16), s_old.astype(BF16))
        s_out[h] = math.exp(LOG_GAMMA[h] * c) * s_old + _mm_tn((k * kdec[h]).astype(BF16), v)
        mu = jnp.mean(o, axis=-1, keepdims=True)
        oc = o - mu
        var = jnp.mean(oc * oc, axis=-1, keepdims=True)
        o_ref[rows, h * D_V:(h + 1) * D_V] = oc * lax.rsqrt(var + EPS) * g_ref[:, h * D_V:(h + 1) * D_V]


def _merge(z_ref, og_ref, or_ref, merged_ref):
    for j in range(0, D_MODEL, 256):
        cs = slice(j, j + 256)
        gr = z_ref[:, C_GR + j:C_GR + j + 256]
        rg = z_ref[:, C_RG + j:C_RG + j + 256]
        ga = z_ref[:, C_GA + j:C_GA + j + 256]
        gb = z_ref[:, C_GB + j:C_GB + j + 256]
        o_g = og_ref[:, cs] * (gr * _sigmoid(gr))
        o_r = or_ref[:, cs] * (rg * _sigmoid(rg))
        merged_ref[:, cs] = (_sigmoid(ga) * o_g + _sigmoid(gb) * o_r).astype(merged_ref.dtype)


def _attend(q, k_ref, v_ref, o_ref, rows):
    for h in range(N_HEADS):
        hs = slice(h * XA_DH, (h + 1) * XA_DH)
        s = _mm_nt(q[:, hs].astype(BF16), k_ref[:, hs].astype(BF16)) * (XA_DH ** -0.5)
        p = jnp.exp(s - jnp.max(s, axis=-1, keepdims=True))
        p = p * (1.0 / jnp.sum(p, axis=-1, keepdims=True))
        o_ref[rows, hs] = _mm(p.astype(BF16), v_ref[:, hs].astype(BF16)).astype(o_ref.dtype)


def _memkv_kernel(m_ref, g_ref, wk_ref, wv_ref, k_ref, v_ref, kb_ref, vb_ref):
    m = _rms(m_ref[...], g_ref[...]).astype(BF16)
    k = _mm(m, wk_ref[...])
    v = _mm(m, wv_ref[...])
    k_ref[...] = k
    v_ref[...] = v
    kb_ref[...] = k.astype(BF16)
    vb_ref[...] = v.astype(BF16)


def _mixer_prompt_kernel(x_ref, cos_ref, sin_ref, wmain_ref, wga_ref, w2_ref, b2_ref, ggla_ref, gret_ref,
                         wmix_ref, gpre_ref, gpost_ref,
                         y_ref, sg_ref, sr_ref,
                         z_ref, la_ref, og_ref, or_ref, merged_ref, *, tl):
    @pl.when(pl.program_id(1) == 0)
    def _():
        sg_ref[...] = jnp.zeros_like(sg_ref)
        sr_ref[...] = jnp.zeros_like(sr_ref)

    x = x_ref[0]
    h_bf = _rms(x, gpre_ref[...]).astype(BF16)
    _inproj(h_bf, wmain_ref, wga_ref, w2_ref, b2_ref, z_ref, la_ref)

    causal, tri, dmat, qdec, kdec = _chunk_consts(CHUNK)
    sg = sg_ref.at[0]
    sr = sr_ref.at[0]

    def chunk_body(ci, carry):
        rows = pl.ds(pl.multiple_of(ci * CHUNK, CHUNK), CHUNK)
        _gla_chunk(z_ref, la_ref, rows, CHUNK, sg, sg, ggla_ref, og_ref, causal, tri)
        _ret_chunk(z_ref, rows, CHUNK, cos_ref[rows, :], sin_ref[rows, :], sr, sr, gret_ref, or_ref,
                   dmat, qdec, kdec)
        return carry

    lax.fori_loop(0, tl // CHUNK, chunk_body, 0)

    _merge(z_ref, og_ref, or_ref, merged_ref)
    m = _mm(merged_ref[...], wmix_ref[...])
    y_ref[0] = x + _rms(m, gpost_ref[...])


def _xattn_prompt_kernel(x_ref, mk_ref, mv_ref, wq_ref, wo_ref, gpre_ref, gpost_ref, y_ref, o_ref):
    x = x_ref[0]
    q = _mm(_rms(x, gpre_ref[...]).astype(BF16), wq_ref[...])
    _attend(q, mk_ref.at[0], mv_ref.at[0], o_ref, slice(None))
    a = _mm(o_ref[...], wo_ref[...])
    y_ref[0] = x + _rms(a, gpost_ref[...])


def _mlp_kernel(x_ref, wup_ref, wdown_ref, gpre_ref, gpost_ref, y_ref):
    x = x_ref[...]
    h_bf = _rms(x, gpre_ref[...]).astype(BF16)
    acc = None
    for j in range(0, D_FF, 1024):
        u = jnp.maximum(_mm(h_bf, wup_ref[:, j:j + 1024]), 0.0)
        part = _mm((u * u).astype(BF16), wdown_ref[j:j + 1024, :])
        acc = part if acc is None else acc + part
    y_ref[...] = x + _rms(acc, gpost_ref[...])


def _inproj_kernel(x_ref, wmain_ref, wga_ref, w2_ref, b2_ref, gpre_ref, z_ref, la_ref):
    h_bf = _rms(x_ref[...], gpre_ref[...]).astype(BF16)
    _inproj(h_bf, wmain_ref, wga_ref, w2_ref, b2_ref, z_ref, la_ref)


def _mixer_sample_kernel(z_ref, la_ref, cos_ref, sin_ref, sgi_ref, sri_ref, ggla_ref, gret_ref,
                         merged_ref, sgo_ref, sro_ref, og_ref, or_ref, *, nb, ls):
    causal, tri, dmat, qdec, kdec = _chunk_consts(ls)
    cos = cos_ref[...]
    sin = sin_ref[...]

    def seq_body(e, carry):
        rows = pl.ds(pl.multiple_of(e * ls, ls), ls)
        _gla_chunk(z_ref, la_ref, rows, ls, sgi_ref.at[e], sgo_ref.at[e], ggla_ref, og_ref, causal, tri)
        _ret_chunk(z_ref, rows, ls, cos, sin, sri_ref.at[e], sro_ref.at[e], gret_ref, or_ref,
                   dmat, qdec, kdec)
        return carry

    lax.fori_loop(0, nb, seq_body, 0)
    _merge(z_ref, og_ref, or_ref, merged_ref)


def _proj_res_kernel(a_ref, res_ref, w_ref, g_ref, y_ref):
    y_ref[...] = res_ref[...] + _rms(_mm(a_ref[...], w_ref[...]), g_ref[...])


def _norm_mm_kernel(x_ref, g_ref, w_ref, y_ref):
    y_ref[...] = _mm(_rms(x_ref[...], g_ref[...]).astype(BF16), w_ref[...])


def _xattn_sample_kernel(q_ref, k_ref, v_ref, o_ref, *, nb, ls):
    def seq_body(e, carry):
        rows = pl.ds(pl.multiple_of(e * ls, ls), ls)
        _attend(q_ref[rows, :], k_ref.at[e], v_ref.at[e], o_ref, rows)
        return carry

    lax.fori_loop(0, nb, seq_body, 0)


def _rope_tables(pos):
    half = D_K // 2
    inv = ROPE_BASE ** (-jnp.arange(half, dtype=F32) / half)
    ang = pos.astype(F32)[:, None] * inv[None, :]
    cos, sin = jnp.cos(ang), jnp.sin(ang)
    return jnp.concatenate([cos, cos], axis=-1), jnp.concatenate([-sin, sin], axis=-1)


def _row_spec(tm, n):
    return pl.BlockSpec((tm, n), lambda i: (i, 0))


def _mlp(x, wup, wdown, gpre, gpost, tm):
    t = x.shape[0]
    return pl.pallas_call(
        _mlp_kernel,
        grid=(t // tm,),
        in_specs=[_row_spec(tm, D_MODEL), _const_spec((D_MODEL, D_FF)), _const_spec((D_FF, D_MODEL)),
                  _const_spec((1, D_MODEL)), _const_spec((1, D_MODEL))],
        out_specs=_row_spec(tm, D_MODEL),
        out_shape=jax.ShapeDtypeStruct((t, D_MODEL), F32),
        compiler_params=_params(("arbitrary",)),
        name="mlp",
    )(x, wup, wdown, gpre, gpost)


def kernel(x_prompt, x_sample, state_gla, state_ret, cache_mem_k, cache_mem_v, mem_prompt, w_in, w_gla_a2, b_gla_a, g_gla_head, g_ret_head, w_mix_out, w_xq, w_xk, w_xv, w_xo, g_mem, w_up, w_down, g_pre_mix, g_post_mix, g_pre_xa, g_post_xa, g_pre_ffn, g_post_ffn):
    depth = w_in.shape[0]
    assert depth == 1
    bp, lp, _ = x_prompt.shape
    bs, ls, _ = x_sample.shape
    tl = 256
    tm = 512
    nb_mix = 8
    nb_xa = 8
    assert lp % tl == 0 and tl % CHUNK == 0 and lp % tm == 0
    assert (bs * ls) % tm == 0 and bs % nb_mix == 0 and bs % nb_xa == 0 and ls % 8 == 0 and ls <= CHUNK
    assert (bp * N_MEM) % tm == 0

    w = w_in[0]
    c_low = 2 * N_QK + 2 * D_MODEL
    wmain = jnp.concatenate([w[:, :c_low], w[:, c_low + GATE_RANK:]], axis=1).astype(BF16)
    wga = jnp.pad(w[:, c_low:c_low + GATE_RANK], ((0, 0), (0, GATE_RANK_PAD - GATE_RANK))).astype(BF16)
    w2 = jnp.pad(w_gla_a2[0], ((0, GATE_RANK_PAD - GATE_RANK), (0, 0))).astype(BF16)
    b2 = b_gla_a[0].reshape(1, N_QK)
    ggla = g_gla_head[0].reshape(1, D_MODEL)
    gret = g_ret_head[0].reshape(1, D_MODEL)
    wmix = w_mix_out[0].astype(BF16)
    wxq, wxk, wxv, wxo = (t[0].astype(BF16) for t in (w_xq, w_xk, w_xv, w_xo))
    wup, wdown = w_up[0].astype(BF16), w_down[0].astype(BF16)
    row = lambda g: g[0].reshape(1, D_MODEL)
    gmem, gpre_mix, gpost_mix, gpre_xa, gpost_xa, gpre_ffn, gpost_ffn = (
        row(g) for g in (g_mem, g_pre_mix, g_post_mix, g_pre_xa, g_post_xa, g_pre_ffn, g_post_ffn))
    cos_p, sin_p = _rope_tables(jnp.arange(lp, dtype=jnp.int32))
    cos_s, sin_s = _rope_tables(PAST_LEN + jnp.arange(ls, dtype=jnp.int32))

    nmem_rows = bp * N_MEM
    mk, mv, mk_bf, mv_bf = pl.pallas_call(
        _memkv_kernel,
        grid=(nmem_rows // tm,),
        in_specs=[_row_spec(tm, D_MODEL), _const_spec((1, D_MODEL)),
                  _const_spec((D_MODEL, D_MODEL)), _const_spec((D_MODEL, D_MODEL))],
        out_specs=[_row_spec(tm, D_MODEL)] * 4,
        out_shape=[jax.ShapeDtypeStruct((nmem_rows, D_MODEL), F32)] * 2
        + [jax.ShapeDtypeStruct((nmem_rows, D_MODEL), BF16)] * 2,
        compiler_params=_params(("arbitrary",)),
        name="memkv",
    )(mem_prompt.reshape(nmem_rows, D_MODEL), gmem, wxk, wxv)

    state_spec = pl.BlockSpec((1, N_HEADS, D_K, D_V), lambda b, t: (b, 0, 0, 0))
    x1p, sg_p, sr_p = pl.pallas_call(
        functools.partial(_mixer_prompt_kernel, tl=tl),
        grid=(bp, lp // tl),
        in_specs=[pl.BlockSpec((1, tl, D_MODEL), lambda b, t: (b, t, 0)),
                  pl.BlockSpec((tl, D_K), lambda b, t: (t, 0)),
                  pl.BlockSpec((tl, D_K), lambda b, t: (t, 0)),
                  _const_spec((D_MODEL, N_MAIN)), _const_spec((D_MODEL, GATE_RANK_PAD)),
                  _const_spec((GATE_RANK_PAD, N_QK)), _const_spec((1, N_QK)),
                  _const_spec((1, D_MODEL)), _const_spec((1, D_MODEL)),
                  _const_spec((D_MODEL, D_MODEL)), _const_spec((1, D_MODEL)), _const_spec((1, D_MODEL))],
        out_specs=[pl.BlockSpec((1, tl, D_MODEL), lambda b, t: (b, t, 0)), state_spec, state_spec],
        out_shape=[jax.ShapeDtypeStruct((bp, lp, D_MODEL), F32),
                   jax.ShapeDtypeStruct((bp, N_HEADS, D_K, D_V), F32),
                   jax.ShapeDtypeStruct((bp, N_HEADS, D_K, D_V), F32)],
        scratch_shapes=[pltpu.VMEM((tl, N_MAIN), F32), pltpu.VMEM((tl, N_QK), F32),
                        pltpu.VMEM((tl, D_MODEL), F32), pltpu.VMEM((tl, D_MODEL), F32),
                        pltpu.VMEM((tl, D_MODEL), BF16)],
        compiler_params=_params(("arbitrary", "arbitrary")),
        name="mixer_prompt",
    )(x_prompt, cos_p, sin_p, wmain, wga, w2, b2, ggla, gret, wmix, gpre_mix, gpost_mix)

    mem_spec = pl.BlockSpec((1, N_MEM, D_MODEL), lambda b, t: (b, 0, 0))
    x2p = pl.pallas_call(
        _xattn_prompt_kernel,
        grid=(bp, lp // tm),
        in_specs=[pl.BlockSpec((1, tm, D_MODEL), lambda b, t: (b, t, 0)), mem_spec, mem_spec,
                  _const_spec((D_MODEL, D_MODEL)), _const_spec((D_MODEL, D_MODEL)),
                  _const_spec((1, D_MODEL)), _const_spec((1, D_MODEL))],
        out_specs=pl.BlockSpec((1, tm, D_MODEL), lambda b, t: (b, t, 0)),
        out_shape=jax.ShapeDtypeStruct((bp, lp, D_MODEL), F32),
        scratch_shapes=[pltpu.VMEM((tm, D_MODEL), BF16)],
        compiler_params=_params(("arbitrary", "arbitrary")),
        name="xattn_prompt",
    )(x1p, mk_bf.reshape(bp, N_MEM, D_MODEL), mv_bf.reshape(bp, N_MEM, D_MODEL), wxq, wxo, gpre_xa, gpost_xa)

    yp = _mlp(x2p.reshape(bp * lp, D_MODEL), wup, wdown, gpre_ffn, gpost_ffn, tm).reshape(bp, lp, D_MODEL)

    ts = bs * ls
    xs = x_sample.reshape(ts, D_MODEL)
    z_s, la_s = pl.pallas_call(
        _inproj_kernel,
        grid=(ts // tm,),
        in_specs=[_row_spec(tm, D_MODEL), _const_spec((D_MODEL, N_MAIN)), _const_spec((D_MODEL, GATE_RANK_PAD)),
                  _const_spec((GATE_RANK_PAD, N_QK)), _const_spec((1, N_QK)), _const_spec((1, D_MODEL))],
        out_specs=[_row_spec(tm, N_MAIN), _row_spec(tm, N_QK)],
        out_shape=[jax.ShapeDtypeStruct((ts, N_MAIN), F32), jax.ShapeDtypeStruct((ts, N_QK), F32)],
        compiler_params=_params(("arbitrary",)),
        name="inproj_sample",
    )(xs, wmain, wga, w2, b2, gpre_mix)

    rows_mix = nb_mix * ls
    st_spec = pl.BlockSpec((nb_mix, N_HEADS, D_K, D_V), lambda i: (i, 0, 0, 0))
    merged_s, sg_s, sr_s = pl.pallas_call(
        functools.partial(_mixer_sample_kernel, nb=nb_mix, ls=ls),
        grid=(bs // nb_mix,),
        in_specs=[_row_spec(rows_mix, N_MAIN), _row_spec(rows_mix, N_QK),
                  _const_spec((ls, D_K)), _const_spec((ls, D_K)), st_spec, st_spec,
                  _const_spec((1, D_MODEL)), _const_spec((1, D_MODEL))],
        out_specs=[_row_spec(rows_mix, D_MODEL), st_spec, st_spec],
        out_shape=[jax.ShapeDtypeStruct((ts, D_MODEL), BF16),
                   jax.ShapeDtypeStruct((bs, N_HEADS, D_K, D_V), F32),
                   jax.ShapeDtypeStruct((bs, N_HEADS, D_K, D_V), F32)],
        scratch_shapes=[pltpu.VMEM((rows_mix, D_MODEL), F32), pltpu.VMEM((rows_mix, D_MODEL), F32)],
        compiler_params=_params(("arbitrary",)),
        name="mixer_sample",
    )(z_s, la_s, cos_s, sin_s, state_gla[0], state_ret[0], ggla, gret)

    def proj_res(a, res, wgt, g):
        return pl.pallas_call(
            _proj_res_kernel,
            grid=(ts // tm,),
            in_specs=[_row_spec(tm, D_MODEL), _row_spec(tm, D_MODEL), _const_spec((D_MODEL, D_MODEL)),
                      _const_spec((1, D_MODEL))],
            out_specs=_row_spec(tm, D_MODEL),
            out_shape=jax.ShapeDtypeStruct((ts, D_MODEL), F32),
            compiler_params=_params(("arbitrary",)),
            name="proj_res_sample",
        )(a, res, wgt, g)

    x1s = proj_res(merged_s, xs, wmix, gpost_mix)
    q_s = pl.pallas_call(
        _norm_mm_kernel,
        grid=(ts // tm,),
        in_specs=[_row_spec(tm, D_MODEL), _const_spec((1, D_MODEL)), _const_spec((D_MODEL, D_MODEL))],
        out_specs=_row_spec(tm, D_MODEL),
        out_shape=jax.ShapeDtypeStruct((ts, D_MODEL), F32),
        compiler_params=_params(("arbitrary",)),
        name="xq_sample",
    )(x1s, gpre_xa, wxq)

    rows_xa = nb_xa * ls
    kv_spec = pl.BlockSpec((nb_xa, N_MEM, D_MODEL), lambda i: (i, 0, 0))
    o_s = pl.pallas_call(
        functools.partial(_xattn_sample_kernel, nb=nb_xa, ls=ls),
        grid=(bs // nb_xa,),
        in_specs=[_row_spec(rows_xa, D_MODEL), kv_spec, kv_spec],
        out_specs=_row_spec(rows_xa, D_MODEL),
        out_shape=jax.ShapeDtypeStruct((ts, D_MODEL), BF16),
        compiler_params=_params(("arbitrary",)),
        name="xattn_sample",
    )(q_s, cache_mem_k[0].reshape(bs, N_MEM, D_MODEL), cache_mem_v[0].reshape(bs, N_MEM, D_MODEL))

    x2s = proj_res(o_s, x1s, wxo, gpost_xa)
    ys = _mlp(x2s, wup, wdown, gpre_ffn, gpost_ffn, tm).reshape(bs, ls, D_MODEL)

    hshape = (1, bp, N_MEM, N_HEADS, XA_DH)
    return (yp, ys, sg_p[None], sr_p[None], mk.reshape(hshape), mv.reshape(hshape), sg_s[None], sr_s[None])
```

```python
import functools
import math

import jax
import jax.numpy as jnp
from jax import lax
from jax.experimental import pallas as pl
from jax.experimental.pallas import tpu as pltpu

F32 = jnp.float32
BF16 = jnp.bfloat16

D_MODEL = 1024
N_HEADS = 4
D_K = 128
D_V = 256
N_MEM = 256
XA_DH = 256
D_FF = 4 * D_MODEL
GATE_RANK = 16
GATE_RANK_PAD = 128
GLA_TAU = 16.0
CHUNK = 64
ROPE_BASE = 10000.0
PAST_LEN = 16384
EPS = 1e-6
LOG_GAMMA = tuple(math.log1p(-(2.0 ** (-5.0 - h))) for h in range(N_HEADS))

C_GQ, C_GK, C_GV, C_GR = 0, 512, 1024, 2048
C_RQ, C_RK, C_RV, C_RG = 3072, 3584, 4096, 5120
C_GA, C_GB = 6144, 7168
N_MAIN = 8192
N_QK = N_HEADS * D_K

VMEM_LIMIT_BYTES = 56 * 1024 * 1024


def _mm(a, b):
    return jnp.dot(a, b, preferred_element_type=F32)


def _mm_nt(a, b):
    return lax.dot_general(a, b, (((1,), (1,)), ((), ())), preferred_element_type=F32)


def _mm_tn(a, b):
    return lax.dot_general(a, b, (((0,), (0,)), ((), ())), preferred_element_type=F32)


def _rms(x, g):
    return x * lax.rsqrt(jnp.mean(x * x, axis=-1, keepdims=True) + EPS) * g


def _sigmoid(x):
    return 1.0 / (1.0 + jnp.exp(-x))


def _const_spec(shape):
    nd = len(shape)
    return pl.BlockSpec(shape, lambda *_: (0,) * nd, pipeline_mode=pl.Buffered(1))


def _params(sem):
    return pltpu.CompilerParams(dimension_semantics=sem, vmem_limit_bytes=VMEM_LIMIT_BYTES)


def _inproj(h_bf, wmain_ref, wga_ref, w2_ref, b2_ref, z_ref, la_ref):
    for j in range(0, N_MAIN, 1024):
        z_ref[:, j:j + 1024] = _mm(h_bf, wmain_ref[:, j:j + 1024])
    ga = _mm(h_bf, wga_ref[...])
    xg = _mm(ga.astype(BF16), w2_ref[...]) + b2_ref[...]
    log_sig = jnp.minimum(xg, 0.0) - jnp.log1p(jnp.exp(-jnp.abs(xg)))
    la_ref[...] = log_sig * (1.0 / GLA_TAU)


def _block_causal(n, c):
    shift = c.bit_length() - 1
    row = lax.broadcasted_iota(jnp.int32, (n, n), 0)
    col = lax.broadcasted_iota(jnp.int32, (n, n), 1)
    return ((row >> shift) == (col >> shift)) & (row >= col)


def _gla_tile(z_ref, la_ref, n, c, s_get, s_put, g_ref, o_ref):
    causal = _block_causal(n, c)
    tri = jnp.where(causal, 1.0, 0.0).astype(BF16)
    la = la_ref[...]
    la_hi = la.astype(BF16)
    la_lo = (la - la_hi.astype(F32)).astype(BF16)
    b = _mm(tri, la_hi) + _mm(tri, la_lo)
    n_chunks = n // c
    b_last = [b[(ci + 1) * c - 1:(ci + 1) * c, :] for ci in range(n_chunks)]
    b_last_full = jnp.concatenate([jnp.broadcast_to(bl, (c, N_QK)) for bl in b_last], axis=0)
    eb = jnp.exp(b)
    enb = jnp.exp(-b)
    ekd = jnp.exp(b_last_full - b)
    for h in range(N_HEADS):
        ks = slice(h * D_K, (h + 1) * D_K)
        q = z_ref[:, C_GQ + h * D_K:C_GQ + (h + 1) * D_K] * (D_K ** -0.5)
        k = z_ref[:, C_GK + h * D_K:C_GK + (h + 1) * D_K]
        v = z_ref[:, C_GV + h * D_V:C_GV + (h + 1) * D_V].astype(BF16)
        qe = (q * eb[:, ks]).astype(BF16)
        ke = (k * enb[:, ks]).astype(BF16)
        kd = (k * ekd[:, ks]).astype(BF16)
        sc = jnp.where(causal, _mm_nt(qe, ke), 0.0).astype(BF16)
        o_intra = _mm(sc, v)
        outs = []
        for ci in range(n_chunks):
            r = slice(ci * c, (ci + 1) * c)
            s_old = s_get(ci, h)
            outs.append(o_intra[r] + _mm(qe[r], s_old.astype(BF16)))
            dcol = jnp.transpose(jnp.broadcast_to(jnp.exp(b_last[ci][:, ks]), (D_K, D_K)))
            s_put(ci, h, jnp.concatenate([dcol, dcol], axis=1) * s_old + _mm_tn(kd[r], v[r]))
        o = jnp.concatenate(outs, axis=0) if n_chunks > 1 else outs[0]
        ms = jnp.mean(o * o, axis=-1, keepdims=True)
        o_ref[:, h * D_V:(h + 1) * D_V] = o * lax.rsqrt(ms + EPS) * g_ref[:, h * D_V:(h + 1) * D_V]


def _ret_tile(z_ref, cos, sin, n, c, s_get, s_put, g_ref, o_ref, dm_ref):
    tpos = (lax.broadcasted_iota(jnp.int32, (n, 1), 0) & (c - 1)).astype(F32)
    n_chunks = n // c
    for h in range(N_HEADS):
        lg = LOG_GAMMA[h]
        q = z_ref[:, C_RQ + h * D_K:C_RQ + (h + 1) * D_K]
        k = z_ref[:, C_RK + h * D_K:C_RK + (h + 1) * D_K]
        v = z_ref[:, C_RV + h * D_V:C_RV + (h + 1) * D_V].astype(BF16)
        q = q * cos + pltpu.roll(q, D_K // 2, 1) * sin
        k = (k * cos + pltpu.roll(k, D_K // 2, 1) * sin) * (D_K ** -0.5)
        sc = (_mm_nt(q.astype(BF16), k.astype(BF16)) * dm_ref[h]).astype(BF16)
        o_intra = _mm(sc, v)
        qd = (q * jnp.exp(lg * (tpos + 1.0))).astype(BF16)
        kd = (k * jnp.exp(lg * (float(c - 1) - tpos))).astype(BF16)
        outs = []
        for ci in range(n_chunks):
            r = slice(ci * c, (ci + 1) * c)
            s_old = s_get(ci, h)
            outs.append(o_intra[r] + _mm(qd[r], s_old.astype(BF16)))
            s_put(ci, h, math.exp(lg * c) * s_old + _mm_tn(kd[r], v[r]))
        o = jnp.concatenate(outs, axis=0) if n_chunks > 1 else outs[0]
        mu = jnp.mean(o, axis=-1, keepdims=True)
        oc = o - mu
        var = jnp.mean(oc * oc, axis=-1, keepdims=True)
        o_ref[:, h * D_V:(h + 1) * D_V] = oc * lax.rsqrt(var + EPS) * g_ref[:, h * D_V:(h + 1) * D_V]


def _merge(z_ref, og_ref, or_ref, merged_ref):
    for j in range(0, D_MODEL, 256):
        cs = slice(j, j + 256)
        gr = z_ref[:, C_GR + j:C_GR + j + 256]
        rg = z_ref[:, C_RG + j:C_RG + j + 256]
        ga = z_ref[:, C_GA + j:C_GA + j + 256]
        gb = z_ref[:, C_GB + j:C_GB + j + 256]
        o_g = og_ref[:, cs] * (gr * _sigmoid(gr))
        o_r = or_ref[:, cs] * (rg * _sigmoid(rg))
        merged_ref[:, cs] = (_sigmoid(ga) * o_g + _sigmoid(gb) * o_r).astype(merged_ref.dtype)


def _attend(q, k_ref, v_ref, o_ref, rows):
    for h in range(N_HEADS):
        hs = slice(h * XA_DH, (h + 1) * XA_DH)
        s = _mm_nt(q[:, hs].astype(BF16), k_ref[:, hs].astype(BF16)) * (XA_DH ** -0.5)
        p = jnp.exp(s - jnp.max(s, axis=-1, keepdims=True))
        p = p * (1.0 / jnp.sum(p, axis=-1, keepdims=True))
        o_ref[rows, hs] = _mm(p.astype(BF16), v_ref[:, hs].astype(BF16)).astype(o_ref.dtype)


def _memkv_kernel(m_ref, g_ref, wk_ref, wv_ref, k_ref, v_ref, kb_ref, vb_ref):
    m = _rms(m_ref[...], g_ref[...]).astype(BF16)
    k = _mm(m, wk_ref[...])
    v = _mm(m, wv_ref[...])
    k_ref[...] = k
    v_ref[...] = v
    kb_ref[...] = k.astype(BF16)
    vb_ref[...] = v.astype(BF16)


def _mixer_prompt_kernel(x_ref, cos_ref, sin_ref, dm_ref, wmain_ref, wga_ref, w2_ref, b2_ref, ggla_ref, gret_ref,
                         wmix_ref, gpre_ref, gpost_ref,
                         y_ref, sg_ref, sr_ref,
                         z_ref, la_ref, og_ref, or_ref, merged_ref, *, tl):
    @pl.when(pl.program_id(1) == 0)
    def _():
        sg_ref[...] = jnp.zeros_like(sg_ref)
        sr_ref[...] = jnp.zeros_like(sr_ref)

    x = x_ref[0]
    h_bf = _rms(x, gpre_ref[...]).astype(BF16)
    _inproj(h_bf, wmain_ref, wga_ref, w2_ref, b2_ref, z_ref, la_ref)

    n_chunks = tl // CHUNK
    carried = {}

    def gla_get(ci, h):
        return sg_ref[0, h] if ci == 0 else carried[h]

    def gla_put(ci, h, val):
        carried[h] = val
        if ci == n_chunks - 1:
            sg_ref[0, h] = val

    _gla_tile(z_ref, la_ref, tl, CHUNK, gla_get, gla_put, ggla_ref, og_ref)

    def ret_put(ci, h, val):
        sr_ref[0, h] = val

    _ret_tile(z_ref, cos_ref[...], sin_ref[...], tl, tl, lambda ci, h: sr_ref[0, h], ret_put,
              gret_ref, or_ref, dm_ref)

    _merge(z_ref, og_ref, or_ref, merged_ref)
    m = _mm(merged_ref[...], wmix_ref[...])
    y_ref[0] = x + _rms(m, gpost_ref[...])


def _xattn_prompt_kernel(x_ref, mk_ref, mv_ref, wq_ref, wo_ref, gpre_ref, gpost_ref, y_ref, o_ref):
    x = x_ref[0]
    q = _mm(_rms(x, gpre_ref[...]).astype(BF16), wq_ref[...])
    _attend(q, mk_ref.at[0], mv_ref.at[0], o_ref, slice(None))
    a = _mm(o_ref[...], wo_ref[...])
    y_ref[0] = x + _rms(a, gpost_ref[...])


def _mlp_kernel(x_ref, wup_ref, wdown_ref, gpre_ref, gpost_ref, y_ref):
    x = x_ref[...]
    h_bf = _rms(x, gpre_ref[...]).astype(BF16)
    acc = None
    for j in range(0, D_FF, 1024):
        u = jnp.maximum(_mm(h_bf, wup_ref[:, j:j + 1024]), 0.0)
        part = _mm((u * u).astype(BF16), wdown_ref[j:j + 1024, :])
        acc = part if acc is None else acc + part
    y_ref[...] = x + _rms(acc, gpost_ref[...])


def _inproj_kernel(x_ref, wmain_ref, wga_ref, w2_ref, b2_ref, gpre_ref, z_ref, la_ref):
    h_bf = _rms(x_ref[...], gpre_ref[...]).astype(BF16)
    _inproj(h_bf, wmain_ref, wga_ref, w2_ref, b2_ref, z_ref, la_ref)


def _mixer_sample_kernel(z_ref, la_ref, cos_ref, sin_ref, dm_ref, sgi_ref, sri_ref, ggla_ref, gret_ref,
                         merged_ref, sgo_ref, sro_ref, og_ref, or_ref, *, nb, ls):
    def gla_put(ci, h, val):
        sgo_ref[ci, h] = val

    def ret_put(ci, h, val):
        sro_ref[ci, h] = val

    _gla_tile(z_ref, la_ref, nb * ls, ls, lambda ci, h: sgi_ref[ci, h], gla_put, ggla_ref, og_ref)
    _ret_tile(z_ref, cos_ref[...], sin_ref[...], nb * ls, ls, lambda ci, h: sri_ref[ci, h], ret_put,
              gret_ref, or_ref, dm_ref)
    _merge(z_ref, og_ref, or_ref, merged_ref)


def _proj_res_kernel(a_ref, res_ref, w_ref, g_ref, y_ref):
    y_ref[...] = res_ref[...] + _rms(_mm(a_ref[...], w_ref[...]), g_ref[...])


def _norm_mm_kernel(x_ref, g_ref, w_ref, y_ref):
    y_ref[...] = _mm(_rms(x_ref[...], g_ref[...]).astype(BF16), w_ref[...])


def _xattn_sample_kernel(q_ref, k_ref, v_ref, o_ref, *, nb, ls):
    def seq_body(e, carry):
        rows = pl.ds(pl.multiple_of(e * ls, ls), ls)
        _attend(q_ref[rows, :], k_ref.at[e], v_ref.at[e], o_ref, rows)
        return carry

    lax.fori_loop(0, nb, seq_body, 0)


def _rope_tables(pos):
    half = D_K // 2
    inv = ROPE_BASE ** (-jnp.arange(half, dtype=F32) / half)
    ang = pos.astype(F32)[:, None] * inv[None, :]
    cos, sin = jnp.cos(ang), jnp.sin(ang)
    return jnp.concatenate([cos, cos], axis=-1), jnp.concatenate([-sin, sin], axis=-1)


def _decay_masks(n, c):
    t = jnp.arange(n, dtype=jnp.int32)
    keep = ((t[:, None] // c) == (t[None, :] // c)) & (t[:, None] >= t[None, :])
    dist = (t[:, None] - t[None, :]).astype(F32)
    lg = jnp.asarray(LOG_GAMMA, F32)[:, None, None]
    return jnp.where(keep[None], jnp.exp(lg * dist[None]), 0.0)


def _row_spec(tm, n):
    return pl.BlockSpec((tm, n), lambda i: (i, 0))


def _mlp(x, wup, wdown, gpre, gpost, tm):
    t = x.shape[0]
    return pl.pallas_call(
        _mlp_kernel,
        grid=(t // tm,),
        in_specs=[_row_spec(tm, D_MODEL), _const_spec((D_MODEL, D_FF)), _const_spec((D_FF, D_MODEL)),
                  _const_spec((1, D_MODEL)), _const_spec((1, D_MODEL))],
        out_specs=_row_spec(tm, D_MODEL),
        out_shape=jax.ShapeDtypeStruct((t, D_MODEL), F32),
        compiler_params=_params(("arbitrary",)),
        name="mlp",
    )(x, wup, wdown, gpre, gpost)


def kernel(x_prompt, x_sample, state_gla, state_ret, cache_mem_k, cache_mem_v, mem_prompt, w_in, w_gla_a2, b_gla_a, g_gla_head, g_ret_head, w_mix_out, w_xq, w_xk, w_xv, w_xo, g_mem, w_up, w_down, g_pre_mix, g_post_mix, g_pre_xa, g_post_xa, g_pre_ffn, g_post_ffn):
    depth = w_in.shape[0]
    assert depth == 1
    bp, lp, _ = x_prompt.shape
    bs, ls, _ = x_sample.shape
    tl = 256
    tm = 512
    nb_mix = 8
    nb_xa = 8
    assert lp % tl == 0 and tl % CHUNK == 0 and lp % tm == 0
    assert (bs * ls) % tm == 0 and bs % nb_mix == 0 and bs % nb_xa == 0
    assert ls % 8 == 0 and ls <= CHUNK and ls & (ls - 1) == 0 and tl & (tl - 1) == 0
    assert (bp * N_MEM) % tm == 0

    w = w_in[0]
    c_low = 2 * N_QK + 2 * D_MODEL
    wmain = jnp.concatenate([w[:, :c_low], w[:, c_low + GATE_RANK:]], axis=1).astype(BF16)
    wga = jnp.pad(w[:, c_low:c_low + GATE_RANK], ((0, 0), (0, GATE_RANK_PAD - GATE_RANK))).astype(BF16)
    w2 = jnp.pad(w_gla_a2[0], ((0, GATE_RANK_PAD - GATE_RANK), (0, 0))).astype(BF16)
    b2 = b_gla_a[0].reshape(1, N_QK)
    ggla = g_gla_head[0].reshape(1, D_MODEL)
    gret = g_ret_head[0].reshape(1, D_MODEL)
    wmix = w_mix_out[0].astype(BF16)
    wxq, wxk, wxv, wxo = (t[0].astype(BF16) for t in (w_xq, w_xk, w_xv, w_xo))
    wup, wdown = w_up[0].astype(BF16), w_down[0].astype(BF16)
    row = lambda g: g[0].reshape(1, D_MODEL)
    gmem, gpre_mix, gpost_mix, gpre_xa, gpost_xa, gpre_ffn, gpost_ffn = (
        row(g) for g in (g_mem, g_pre_mix, g_post_mix, g_pre_xa, g_post_xa, g_pre_ffn, g_post_ffn))
    cos_p, sin_p = _rope_tables(jnp.arange(lp, dtype=jnp.int32))
    cos_s, sin_s = _rope_tables(PAST_LEN + jnp.arange(ls, dtype=jnp.int32))
    rows_mix = nb_mix * ls
    cos_s, sin_s = jnp.tile(cos_s, (nb_mix, 1)), jnp.tile(sin_s, (nb_mix, 1))
    dm_p = _decay_masks(tl, tl)
    dm_s = _decay_masks(rows_mix, ls)

    nmem_rows = bp * N_MEM
    mk, mv, mk_bf, mv_bf = pl.pallas_call(
        _memkv_kernel,
        grid=(nmem_rows // tm,),
        in_specs=[_row_spec(tm, D_MODEL), _const_spec((1, D_MODEL)),
                  _const_spec((D_MODEL, D_MODEL)), _const_spec((D_MODEL, D_MODEL))],
        out_specs=[_row_spec(tm, D_MODEL)] * 4,
        out_shape=[jax.ShapeDtypeStruct((nmem_rows, D_MODEL), F32)] * 2
        + [jax.ShapeDtypeStruct((nmem_rows, D_MODEL), BF16)] * 2,
        compiler_params=_params(("arbitrary",)),
        name="memkv",
    )(mem_prompt.reshape(nmem_rows, D_MODEL), gmem, wxk, wxv)

    state_spec = pl.BlockSpec((1, N_HEADS, D_K, D_V), lambda b, t: (b, 0, 0, 0))
    x1p, sg_p, sr_p = pl.pallas_call(
        functools.partial(_mixer_prompt_kernel, tl=tl),
        grid=(bp, lp // tl),
        in_specs=[pl.BlockSpec((1, tl, D_MODEL), lambda b, t: (b, t, 0)),
                  pl.BlockSpec((tl, D_K), lambda b, t: (t, 0)),
                  pl.BlockSpec((tl, D_K), lambda b, t: (t, 0)),
                  _const_spec((N_HEADS, tl, tl)),
                  _const_spec((D_MODEL, N_MAIN)), _const_spec((D_MODEL, GATE_RANK_PAD)),
                  _const_spec((GATE_RANK_PAD, N_QK)), _const_spec((1, N_QK)),
                  _const_spec((1, D_MODEL)), _const_spec((1, D_MODEL)),
                  _const_spec((D_MODEL, D_MODEL)), _const_spec((1, D_MODEL)), _const_spec((1, D_MODEL))],
        out_specs=[pl.BlockSpec((1, tl, D_MODEL), lambda b, t: (b, t, 0)), state_spec, state_spec],
        out_shape=[jax.ShapeDtypeStruct((bp, lp, D_MODEL), F32),
                   jax.ShapeDtypeStruct((bp, N_HEADS, D_K, D_V), F32),
                   jax.ShapeDtypeStruct((bp, N_HEADS, D_K, D_V), F32)],
        scratch_shapes=[pltpu.VMEM((tl, N_MAIN), F32), pltpu.VMEM((tl, N_QK), F32),
                        pltpu.VMEM((tl, D_MODEL), F32), pltpu.VMEM((tl, D_MODEL), F32),
                        pltpu.VMEM((tl, D_MODEL), BF16)],
        compiler_params=_params(("arbitrary", "arbitrary")),
        name="mixer_prompt",
    )(x_prompt, cos_p, sin_p, dm_p, wmain, wga, w2, b2, ggla, gret, wmix, gpre_mix, gpost_mix)

    mem_spec = pl.BlockSpec((1, N_MEM, D_MODEL), lambda b, t: (b, 0, 0))
    x2p = pl.pallas_call(
        _xattn_prompt_kernel,
        grid=(bp, lp // tm),
        in_specs=[pl.BlockSpec((1, tm, D_MODEL), lambda b, t: (b, t, 0)), mem_spec, mem_spec,
                  _const_spec((D_MODEL, D_MODEL)), _const_spec((D_MODEL, D_MODEL)),
                  _const_spec((1, D_MODEL)), _const_spec((1, D_MODEL))],
        out_specs=pl.BlockSpec((1, tm, D_MODEL), lambda b, t: (b, t, 0)),
        out_shape=jax.ShapeDtypeStruct((bp, lp, D_MODEL), F32),
        scratch_shapes=[pltpu.VMEM((tm, D_MODEL), BF16)],
        compiler_params=_params(("arbitrary", "arbitrary")),
        name="xattn_prompt",
    )(x1p, mk_bf.reshape(bp, N_MEM, D_MODEL), mv_bf.reshape(bp, N_MEM, D_MODEL), wxq, wxo, gpre_xa, gpost_xa)

    yp = _mlp(x2p.reshape(bp * lp, D_MODEL), wup, wdown, gpre_ffn, gpost_ffn, tm).reshape(bp, lp, D_MODEL)

    ts = bs * ls
    xs = x_sample.reshape(ts, D_MODEL)
    z_s, la_s = pl.pallas_call(
        _inproj_kernel,
        grid=(ts // tm,),
        in_specs=[_row_spec(tm, D_MODEL), _const_spec((D_MODEL, N_MAIN)), _const_spec((D_MODEL, GATE_RANK_PAD)),
                  _const_spec((GATE_RANK_PAD, N_QK)), _const_spec((1, N_QK)), _const_spec((1, D_MODEL))],
        out_specs=[_row_spec(tm, N_MAIN), _row_spec(tm, N_QK)],
        out_shape=[jax.ShapeDtypeStruct((ts, N_MAIN), F32), jax.ShapeDtypeStruct((ts, N_QK), F32)],
        compiler_params=_params(("arbitrary",)),
        name="inproj_sample",
    )(xs, wmain, wga, w2, b2, gpre_mix)

    st_spec = pl.BlockSpec((nb_mix, N_HEADS, D_K, D_V), lambda i: (i, 0, 0, 0))
    merged_s, sg_s, sr_s = pl.pallas_call(
        functools.partial(_mixer_sample_kernel, nb=nb_mix, ls=ls),
        grid=(bs // nb_mix,),
        in_specs=[_row_spec(rows_mix, N_MAIN), _row_spec(rows_mix, N_QK),
                  _const_spec((rows_mix, D_K)), _const_spec((rows_mix, D_K)),
                  _const_spec((N_HEADS, rows_mix, rows_mix)), st_spec, st_spec,
                  _const_spec((1, D_MODEL)), _const_spec((1, D_MODEL))],
        out_specs=[_row_spec(rows_mix, D_MODEL), st_spec, st_spec],
        out_shape=[jax.ShapeDtypeStruct((ts, D_MODEL), BF16),
                   jax.ShapeDtypeStruct((bs, N_HEADS, D_K, D_V), F32),
                   jax.ShapeDtypeStruct((bs, N_HEADS, D_K, D_V), F32)],
        scratch_shapes=[pltpu.VMEM((rows_mix, D_MODEL), F32), pltpu.VMEM((rows_mix, D_MODEL), F32)],
        compiler_params=_params(("arbitrary",)),
        name="mixer_sample",
    )(z_s, la_s, cos_s, sin_s, dm_s, state_gla[0], state_ret[0], ggla, gret)

    def proj_res(a, res, wgt, g):
        return pl.pallas_call(
            _proj_res_kernel,
            grid=(ts // tm,),
            in_specs=[_row_spec(tm, D_MODEL), _row_spec(tm, D_MODEL), _const_spec((D_MODEL, D_MODEL)),
                      _const_spec((1, D_MODEL))],
            out_specs=_row_spec(tm, D_MODEL),
            out_shape=jax.ShapeDtypeStruct((ts, D_MODEL), F32),
            compiler_params=_params(("arbitrary",)),
            name="proj_res_sample",
        )(a, res, wgt, g)

    x1s = proj_res(merged_s, xs, wmix, gpost_mix)
    q_s = pl.pallas_call(
        _norm_mm_kernel,
        grid=(ts // tm,),
        in_specs=[_row_spec(tm, D_MODEL), _const_spec((1, D_MODEL)), _const_spec((D_MODEL, D_MODEL))],
        out_specs=_row_spec(tm, D_MODEL),
        out_shape=jax.ShapeDtypeStruct((ts, D_MODEL), F32),
        compiler_params=_params(("arbitrary",)),
        name="xq_sample",
    )(x1s, gpre_xa, wxq)

    rows_xa = nb_xa * ls
    kv_spec = pl.BlockSpec((nb_xa, N_MEM, D_MODEL), lambda i: (i, 0, 0))
    o_s = pl.pallas_call(
        functools.partial(_xattn_sample_kernel, nb=nb_xa, ls=ls),
        grid=(bs // nb_xa,),
        in_specs=[_row_spec(rows_xa, D_MODEL), kv_spec, kv_spec],
        out_specs=_row_spec(rows_xa, D_MODEL),
        out_shape=jax.ShapeDtypeStruct((ts, D_MODEL), BF16),
        compiler_params=_params(("arbitrary",)),
        name="xattn_sample",
    )(q_s, cache_mem_k[0].reshape(bs, N_MEM, D_MODEL), cache_mem_v[0].reshape(bs, N_MEM, D_MODEL))

    x2s = proj_res(o_s, x1s, wxo, gpost_xa)
    ys = _mlp(x2s, wup, wdown, gpre_ffn, gpost_ffn, tm).reshape(bs, ls, D_MODEL)

    hshape = (1, bp, N_MEM, N_HEADS, XA_DH)
    return (yp, ys, sg_p[None], sr_p[None], mk.reshape(hshape), mv.reshape(hshape), sg_s[None], sr_s[None])
```

```python
import functools
import math

import jax
import jax.numpy as jnp
from jax import lax
from jax.experimental import pallas as pl
from jax.experimental.pallas import tpu as pltpu

F32 = jnp.float32
BF16 = jnp.bfloat16

D_MODEL = 1024
N_HEADS = 4
D_K = 128
D_V = 256
N_MEM = 256
XA_DH = 256
D_FF = 4 * D_MODEL
GATE_RANK = 16
GATE_RANK_PAD = 128
GLA_TAU = 16.0
CHUNK = 64
ROPE_BASE = 10000.0
PAST_LEN = 16384
EPS = 1e-6
LOG_GAMMA = tuple(math.log1p(-(2.0 ** (-5.0 - h))) for h in range(N_HEADS))

C_GQ, C_GK, C_GV, C_GR = 0, 512, 1024, 2048
C_RQ, C_RK, C_RV, C_RG = 3072, 3584, 4096, 5120
C_GA, C_GB = 6144, 7168
N_MAIN = 8192
N_QK = N_HEADS * D_K

VMEM_LIMIT_BYTES = 56 * 1024 * 1024


def _mm(a, b):
    return jnp.dot(a, b, preferred_element_type=F32)


def _mm_nt(a, b):
    return lax.dot_general(a, b, (((1,), (1,)), ((), ())), preferred_element_type=F32)


def _mm_tn(a, b):
    return lax.dot_general(a, b, (((0,), (0,)), ((), ())), preferred_element_type=F32)


def _rms(x, g):
    return x * lax.rsqrt(jnp.mean(x * x, axis=-1, keepdims=True) + EPS) * g


def _sigmoid(x):
    return 1.0 / (1.0 + jnp.exp(-x))


def _const_spec(shape):
    nd = len(shape)
    return pl.BlockSpec(shape, lambda *_: (0,) * nd, pipeline_mode=pl.Buffered(1))


def _params(sem):
    return pltpu.CompilerParams(dimension_semantics=sem, vmem_limit_bytes=VMEM_LIMIT_BYTES)


def _inproj(h_bf, wmain_ref, wga_ref, w2_ref, b2_ref, z_ref, la_ref):
    for j in range(0, N_MAIN, 1024):
        z_ref[:, j:j + 1024] = _mm(h_bf, wmain_ref[:, j:j + 1024])
    ga = _mm(h_bf, wga_ref[...])
    xg = _mm(ga.astype(BF16), w2_ref[...]) + b2_ref[...]
    log_sig = jnp.minimum(xg, 0.0) - jnp.log1p(jnp.exp(-jnp.abs(xg)))
    la_ref[...] = log_sig * (1.0 / GLA_TAU)


def _block_causal(n, c):
    shift = c.bit_length() - 1
    row = lax.broadcasted_iota(jnp.int32, (n, n), 0)
    col = lax.broadcasted_iota(jnp.int32, (n, n), 1)
    return ((row >> shift) == (col >> shift)) & (row >= col)


def _gla_tile(z_ref, la_ref, n, c, s_get, s_put, g_ref, o_ref):
    causal = _block_causal(n, c)
    tri = jnp.where(causal, 1.0, 0.0).astype(BF16)
    la = la_ref[...]
    la_hi = la.astype(BF16)
    la_lo = (la - la_hi.astype(F32)).astype(BF16)
    b = _mm(tri, la_hi) + _mm(tri, la_lo)
    n_chunks = n // c
    b_last = [b[(ci + 1) * c - 1:(ci + 1) * c, :] for ci in range(n_chunks)]
    b_last_full = jnp.concatenate([jnp.broadcast_to(bl, (c, N_QK)) for bl in b_last], axis=0)
    eb = jnp.exp(b)
    enb = jnp.exp(-b)
    ekd = jnp.exp(b_last_full - b)
    for h in range(N_HEADS):
        ks = slice(h * D_K, (h + 1) * D_K)
        q = z_ref[:, C_GQ + h * D_K:C_GQ + (h + 1) * D_K] * (D_K ** -0.5)
        k = z_ref[:, C_GK + h * D_K:C_GK + (h + 1) * D_K]
        v = z_ref[:, C_GV + h * D_V:C_GV + (h + 1) * D_V].astype(BF16)
        qe = (q * eb[:, ks]).astype(BF16)
        ke = (k * enb[:, ks]).astype(BF16)
        kd = (k * ekd[:, ks]).astype(BF16)
        sc = jnp.where(causal, _mm_nt(qe, ke), 0.0).astype(BF16)
        o_intra = _mm(sc, v)
        outs = []
        for ci in range(n_chunks):
            r = slice(ci * c, (ci + 1) * c)
            s_old = s_get(ci, h)
            outs.append(o_intra[r] + _mm(qe[r], s_old.astype(BF16)))
            dcol = jnp.transpose(jnp.broadcast_to(jnp.exp(b_last[ci][:, ks]), (D_K, D_K)))
            s_put(ci, h, jnp.concatenate([dcol, dcol], axis=1) * s_old + _mm_tn(kd[r], v[r]))
        o = jnp.concatenate(outs, axis=0) if n_chunks > 1 else outs[0]
        ms = jnp.mean(o * o, axis=-1, keepdims=True)
        o_ref[:, h * D_V:(h + 1) * D_V] = o * lax.rsqrt(ms + EPS) * g_ref[:, h * D_V:(h + 1) * D_V]


def _ret_tile(z_ref, cos, sin, n, c, s_get, s_put, g_ref, o_ref, dm_ref):
    tpos = (lax.broadcasted_iota(jnp.int32, (n, 1), 0) & (c - 1)).astype(F32)
    n_chunks = n // c
    for h in range(N_HEADS):
        lg = LOG_GAMMA[h]
        q = z_ref[:, C_RQ + h * D_K:C_RQ + (h + 1) * D_K]
        k = z_ref[:, C_RK + h * D_K:C_RK + (h + 1) * D_K]
        v = z_ref[:, C_RV + h * D_V:C_RV + (h + 1) * D_V].astype(BF16)
        q = q * cos + pltpu.roll(q, D_K // 2, 1) * sin
        k = (k * cos + pltpu.roll(k, D_K // 2, 1) * sin) * (D_K ** -0.5)
        sc = (_mm_nt(q.astype(BF16), k.astype(BF16)) * dm_ref[h]).astype(BF16)
        o_intra = _mm(sc, v)
        qd = (q * jnp.exp(lg * (tpos + 1.0))).astype(BF16)
        kd = (k * jnp.exp(lg * (float(c - 1) - tpos))).astype(BF16)
        outs = []
        for ci in range(n_chunks):
            r = slice(ci * c, (ci + 1) * c)
            s_old = s_get(ci, h)
            outs.append(o_intra[r] + _mm(qd[r], s_old.astype(BF16)))
            s_put(ci, h, math.exp(lg * c) * s_old + _mm_tn(kd[r], v[r]))
        o = jnp.concatenate(outs, axis=0) if n_chunks > 1 else outs[0]
        mu = jnp.mean(o, axis=-1, keepdims=True)
        oc = o - mu
        var = jnp.mean(oc * oc, axis=-1, keepdims=True)
        o_ref[:, h * D_V:(h + 1) * D_V] = oc * lax.rsqrt(var + EPS) * g_ref[:, h * D_V:(h + 1) * D_V]


def _merge(z_ref, og_ref, or_ref, merged_ref):
    for j in range(0, D_MODEL, 256):
        cs = slice(j, j + 256)
        gr = z_ref[:, C_GR + j:C_GR + j + 256]
        rg = z_ref[:, C_RG + j:C_RG + j + 256]
        ga = z_ref[:, C_GA + j:C_GA + j + 256]
        gb = z_ref[:, C_GB + j:C_GB + j + 256]
        o_g = og_ref[:, cs] * (gr * _sigmoid(gr))
        o_r = or_ref[:, cs] * (rg * _sigmoid(rg))
        merged_ref[:, cs] = (_sigmoid(ga) * o_g + _sigmoid(gb) * o_r).astype(merged_ref.dtype)


def _attend(q, k_ref, v_ref, o_ref, rows):
    for h in range(N_HEADS):
        hs = slice(h * XA_DH, (h + 1) * XA_DH)
        s = _mm_nt(q[:, hs].astype(BF16), k_ref[:, hs].astype(BF16)) * (XA_DH ** -0.5)
        p = jnp.exp(s - jnp.max(s, axis=-1, keepdims=True))
        p = p * (1.0 / jnp.sum(p, axis=-1, keepdims=True))
        o_ref[rows, hs] = _mm(p.astype(BF16), v_ref[:, hs].astype(BF16)).astype(o_ref.dtype)


def _memkv_kernel(m_ref, g_ref, wk_ref, wv_ref, k_ref, v_ref, kb_ref, vb_ref):
    m = _rms(m_ref[...], g_ref[...]).astype(BF16)
    k = _mm(m, wk_ref[...])
    v = _mm(m, wv_ref[...])
    k_ref[...] = k
    v_ref[...] = v
    kb_ref[...] = k.astype(BF16)
    vb_ref[...] = v.astype(BF16)


def _mixer_prompt_kernel(x_ref, cos_ref, sin_ref, dm_ref, wmain_ref, wga_ref, w2_ref, b2_ref, ggla_ref, gret_ref,
                         wmix_ref, gpre_ref, gpost_ref,
                         y_ref, sg_ref, sr_ref,
                         z_ref, la_ref, og_ref, or_ref, merged_ref, *, tl):
    @pl.when(pl.program_id(1) == 0)
    def _():
        sg_ref[...] = jnp.zeros_like(sg_ref)
        sr_ref[...] = jnp.zeros_like(sr_ref)

    x = x_ref[0]
    h_bf = _rms(x, gpre_ref[...]).astype(BF16)
    _inproj(h_bf, wmain_ref, wga_ref, w2_ref, b2_ref, z_ref, la_ref)

    n_chunks = tl // CHUNK
    carried = {}

    def gla_get(ci, h):
        return sg_ref[0, h] if ci == 0 else carried[h]

    def gla_put(ci, h, val):
        carried[h] = val
        if ci == n_chunks - 1:
            sg_ref[0, h] = val

    _gla_tile(z_ref, la_ref, tl, CHUNK, gla_get, gla_put, ggla_ref, og_ref)

    def ret_put(ci, h, val):
        sr_ref[0, h] = val

    _ret_tile(z_ref, cos_ref[...], sin_ref[...], tl, tl, lambda ci, h: sr_ref[0, h], ret_put,
              gret_ref, or_ref, dm_ref)

    _merge(z_ref, og_ref, or_ref, merged_ref)
    m = _mm(merged_ref[...], wmix_ref[...])
    y_ref[0] = x + _rms(m, gpost_ref[...])


def _xattn_prompt_kernel(x_ref, mk_ref, mv_ref, wq_ref, wo_ref, gpre_ref, gpost_ref, y_ref, o_ref):
    x = x_ref[0]
    q = _mm(_rms(x, gpre_ref[...]).astype(BF16), wq_ref[...])
    _attend(q, mk_ref.at[0], mv_ref.at[0], o_ref, slice(None))
    a = _mm(o_ref[...], wo_ref[...])
    y_ref[0] = x + _rms(a, gpost_ref[...])


def _mlp_kernel(x_ref, wup_ref, wdown_ref, gpre_ref, gpost_ref, y_ref):
    x = x_ref[...]
    h_bf = _rms(x, gpre_ref[...]).astype(BF16)
    acc = None
    for j in range(0, D_FF, 1024):
        u = jnp.maximum(_mm(h_bf, wup_ref[:, j:j + 1024]), 0.0)
        part = _mm((u * u).astype(BF16), wdown_ref[j:j + 1024, :])
        acc = part if acc is None else acc + part
    y_ref[...] = x + _rms(acc, gpost_ref[...])


def _inproj_kernel(x_ref, wmain_ref, wga_ref, w2_ref, b2_ref, gpre_ref, z_ref, la_ref):
    h_bf = _rms(x_ref[...], gpre_ref[...]).astype(BF16)
    _inproj(h_bf, wmain_ref, wga_ref, w2_ref, b2_ref, z_ref, la_ref)


def _mixer_sample_kernel(z_ref, la_ref, cos_ref, sin_ref, dm_ref, sgi_ref, sri_ref, ggla_ref, gret_ref,
                         merged_ref, sgo_ref, sro_ref, og_ref, or_ref, *, nb, ls):
    def gla_put(ci, h, val):
        sgo_ref[ci, h] = val

    def ret_put(ci, h, val):
        sro_ref[ci, h] = val

    _gla_tile(z_ref, la_ref, nb * ls, ls, lambda ci, h: sgi_ref[ci, h], gla_put, ggla_ref, og_ref)
    _ret_tile(z_ref, cos_ref[...], sin_ref[...], nb * ls, ls, lambda ci, h: sri_ref[ci, h], ret_put,
              gret_ref, or_ref, dm_ref)
    _merge(z_ref, og_ref, or_ref, merged_ref)


def _proj_res_kernel(a_ref, res_ref, w_ref, g_ref, y_ref):
    y_ref[...] = res_ref[...] + _rms(_mm(a_ref[...], w_ref[...]), g_ref[...])


def _norm_mm_kernel(x_ref, g_ref, w_ref, y_ref):
    y_ref[...] = _mm(_rms(x_ref[...], g_ref[...]).astype(BF16), w_ref[...])


def _xattn_sample_kernel(q_ref, k_hbm, v_hbm, o_ref, kbuf, vbuf, sem, *, nb, ls):
    i, h = pl.program_id(0), pl.program_id(1)
    step = i * N_HEADS + h
    n_steps = pl.num_programs(0) * N_HEADS

    def copies(blk, head, slot):
        src = pl.ds(blk * nb, nb)
        return (pltpu.make_async_copy(k_hbm.at[src, :, head, :], kbuf.at[slot], sem.at[0, slot]),
                pltpu.make_async_copy(v_hbm.at[src, :, head, :], vbuf.at[slot], sem.at[1, slot]))

    @pl.when(step == 0)
    def _():
        for c in copies(i, h, 0):
            c.start()

    @pl.when(step + 1 < n_steps)
    def _():
        nxt = step + 1
        for c in copies(nxt // N_HEADS, nxt % N_HEADS, nxt % 2):
            c.start()

    slot = step % 2
    for c in copies(i, h, slot):
        c.wait()

    def seq_body(e, carry):
        rows = pl.ds(pl.multiple_of(e * ls, ls), ls)
        s = _mm_nt(q_ref[rows, :].astype(BF16), kbuf[slot, e].astype(BF16)) * (XA_DH ** -0.5)
        p = jnp.exp(s - jnp.max(s, axis=-1, keepdims=True))
        p = p * (1.0 / jnp.sum(p, axis=-1, keepdims=True))
        o_ref[rows, :] = _mm(p.astype(BF16), vbuf[slot, e].astype(BF16)).astype(o_ref.dtype)
        return carry

    lax.fori_loop(0, nb, seq_body, 0)


def _rope_tables(pos):
    half = D_K // 2
    inv = ROPE_BASE ** (-jnp.arange(half, dtype=F32) / half)
    ang = pos.astype(F32)[:, None] * inv[None, :]
    cos, sin = jnp.cos(ang), jnp.sin(ang)
    return jnp.concatenate([cos, cos], axis=-1), jnp.concatenate([-sin, sin], axis=-1)


def _decay_masks(n, c):
    t = jnp.arange(n, dtype=jnp.int32)
    keep = ((t[:, None] // c) == (t[None, :] // c)) & (t[:, None] >= t[None, :])
    dist = (t[:, None] - t[None, :]).astype(F32)
    lg = jnp.asarray(LOG_GAMMA, F32)[:, None, None]
    return jnp.where(keep[None], jnp.exp(lg * dist[None]), 0.0)


def _row_spec(tm, n):
    return pl.BlockSpec((tm, n), lambda i: (i, 0))


def _mlp(x, wup, wdown, gpre, gpost, tm):
    t = x.shape[0]
    return pl.pallas_call(
        _mlp_kernel,
        grid=(t // tm,),
        in_specs=[_row_spec(tm, D_MODEL), _const_spec((D_MODEL, D_FF)), _const_spec((D_FF, D_MODEL)),
                  _const_spec((1, D_MODEL)), _const_spec((1, D_MODEL))],
        out_specs=_row_spec(tm, D_MODEL),
        out_shape=jax.ShapeDtypeStruct((t, D_MODEL), F32),
        compiler_params=_params(("arbitrary",)),
        name="mlp",
    )(x, wup, wdown, gpre, gpost)


def kernel(x_prompt, x_sample, state_gla, state_ret, cache_mem_k, cache_mem_v, mem_prompt, w_in, w_gla_a2, b_gla_a, g_gla_head, g_ret_head, w_mix_out, w_xq, w_xk, w_xv, w_xo, g_mem, w_up, w_down, g_pre_mix, g_post_mix, g_pre_xa, g_post_xa, g_pre_ffn, g_post_ffn):
    depth = w_in.shape[0]
    assert depth == 1
    bp, lp, _ = x_prompt.shape
    bs, ls, _ = x_sample.shape
    tl = 256
    tm = 512
    nb_mix = 8
    nb_xa = 8
    assert lp % tl == 0 and tl % CHUNK == 0 and lp % tm == 0
    assert (bs * ls) % tm == 0 and bs % nb_mix == 0 and bs % nb_xa == 0
    assert ls % 8 == 0 and ls <= CHUNK and ls & (ls - 1) == 0 and tl & (tl - 1) == 0
    assert (bp * N_MEM) % tm == 0

    w = w_in[0]
    c_low = 2 * N_QK + 2 * D_MODEL
    wmain = jnp.concatenate([w[:, :c_low], w[:, c_low + GATE_RANK:]], axis=1).astype(BF16)
    wga = jnp.pad(w[:, c_low:c_low + GATE_RANK], ((0, 0), (0, GATE_RANK_PAD - GATE_RANK))).astype(BF16)
    w2 = jnp.pad(w_gla_a2[0], ((0, GATE_RANK_PAD - GATE_RANK), (0, 0))).astype(BF16)
    b2 = b_gla_a[0].reshape(1, N_QK)
    ggla = g_gla_head[0].reshape(1, D_MODEL)
    gret = g_ret_head[0].reshape(1, D_MODEL)
    wmix = w_mix_out[0].astype(BF16)
    wxq, wxk, wxv, wxo = (t[0].astype(BF16) for t in (w_xq, w_xk, w_xv, w_xo))
    wup, wdown = w_up[0].astype(BF16), w_down[0].astype(BF16)
    row = lambda g: g[0].reshape(1, D_MODEL)
    gmem, gpre_mix, gpost_mix, gpre_xa, gpost_xa, gpre_ffn, gpost_ffn = (
        row(g) for g in (g_mem, g_pre_mix, g_post_mix, g_pre_xa, g_post_xa, g_pre_ffn, g_post_ffn))
    cos_p, sin_p = _rope_tables(jnp.arange(lp, dtype=jnp.int32))
    cos_s, sin_s = _rope_tables(PAST_LEN + jnp.arange(ls, dtype=jnp.int32))
    rows_mix = nb_mix * ls
    cos_s, sin_s = jnp.tile(cos_s, (nb_mix, 1)), jnp.tile(sin_s, (nb_mix, 1))
    dm_p = _decay_masks(tl, tl)
    dm_s = _decay_masks(rows_mix, ls)

    nmem_rows = bp * N_MEM
    mk, mv, mk_bf, mv_bf = pl.pallas_call(
        _memkv_kernel,
        grid=(nmem_rows // tm,),
        in_specs=[_row_spec(tm, D_MODEL), _const_spec((1, D_MODEL)),
                  _const_spec((D_MODEL, D_MODEL)), _const_spec((D_MODEL, D_MODEL))],
        out_specs=[_row_spec(tm, D_MODEL)] * 4,
        out_shape=[jax.ShapeDtypeStruct((nmem_rows, D_MODEL), F32)] * 2
        + [jax.ShapeDtypeStruct((nmem_rows, D_MODEL), BF16)] * 2,
        compiler_params=_params(("arbitrary",)),
        name="memkv",
    )(mem_prompt.reshape(nmem_rows, D_MODEL), gmem, wxk, wxv)

    state_spec = pl.BlockSpec((1, N_HEADS, D_K, D_V), lambda b, t: (b, 0, 0, 0))
    x1p, sg_p, sr_p = pl.pallas_call(
        functools.partial(_mixer_prompt_kernel, tl=tl),
        grid=(bp, lp // tl),
        in_specs=[pl.BlockSpec((1, tl, D_MODEL), lambda b, t: (b, t, 0)),
                  pl.BlockSpec((tl, D_K), lambda b, t: (t, 0)),
                  pl.BlockSpec((tl, D_K), lambda b, t: (t, 0)),
                  _const_spec((N_HEADS, tl, tl)),
                  _const_spec((D_MODEL, N_MAIN)), _const_spec((D_MODEL, GATE_RANK_PAD)),
                  _const_spec((GATE_RANK_PAD, N_QK)), _const_spec((1, N_QK)),
                  _const_spec((1, D_MODEL)), _const_spec((1, D_MODEL)),
                  _const_spec((D_MODEL, D_MODEL)), _const_spec((1, D_MODEL)), _const_spec((1, D_MODEL))],
        out_specs=[pl.BlockSpec((1, tl, D_MODEL), lambda b, t: (b, t, 0)), state_spec, state_spec],
        out_shape=[jax.ShapeDtypeStruct((bp, lp, D_MODEL), F32),
                   jax.ShapeDtypeStruct((bp, N_HEADS, D_K, D_V), F32),
                   jax.ShapeDtypeStruct((bp, N_HEADS, D_K, D_V), F32)],
        scratch_shapes=[pltpu.VMEM((tl, N_MAIN), F32), pltpu.VMEM((tl, N_QK), F32),
                        pltpu.VMEM((tl, D_MODEL), F32), pltpu.VMEM((tl, D_MODEL), F32),
                        pltpu.VMEM((tl, D_MODEL), BF16)],
        compiler_params=_params(("arbitrary", "arbitrary")),
        name="mixer_prompt",
    )(x_prompt, cos_p, sin_p, dm_p, wmain, wga, w2, b2, ggla, gret, wmix, gpre_mix, gpost_mix)

    mem_spec = pl.BlockSpec((1, N_MEM, D_MODEL), lambda b, t: (b, 0, 0))
    x2p = pl.pallas_call(
        _xattn_prompt_kernel,
        grid=(bp, lp // tm),
        in_specs=[pl.BlockSpec((1, tm, D_MODEL), lambda b, t: (b, t, 0)), mem_spec, mem_spec,
                  _const_spec((D_MODEL, D_MODEL)), _const_spec((D_MODEL, D_MODEL)),
                  _const_spec((1, D_MODEL)), _const_spec((1, D_MODEL))],
        out_specs=pl.BlockSpec((1, tm, D_MODEL), lambda b, t: (b, t, 0)),
        out_shape=jax.ShapeDtypeStruct((bp, lp, D_MODEL), F32),
        scratch_shapes=[pltpu.VMEM((tm, D_MODEL), BF16)],
        compiler_params=_params(("arbitrary", "arbitrary")),
        name="xattn_prompt",
    )(x1p, mk_bf.reshape(bp, N_MEM, D_MODEL), mv_bf.reshape(bp, N_MEM, D_MODEL), wxq, wxo, gpre_xa, gpost_xa)

    yp = _mlp(x2p.reshape(bp * lp, D_MODEL), wup, wdown, gpre_ffn, gpost_ffn, tm).reshape(bp, lp, D_MODEL)

    ts = bs * ls
    xs = x_sample.reshape(ts, D_MODEL)
    z_s, la_s = pl.pallas_call(
        _inproj_kernel,
        grid=(ts // tm,),
        in_specs=[_row_spec(tm, D_MODEL), _const_spec((D_MODEL, N_MAIN)), _const_spec((D_MODEL, GATE_RANK_PAD)),
                  _const_spec((GATE_RANK_PAD, N_QK)), _const_spec((1, N_QK)), _const_spec((1, D_MODEL))],
        out_specs=[_row_spec(tm, N_MAIN), _row_spec(tm, N_QK)],
        out_shape=[jax.ShapeDtypeStruct((ts, N_MAIN), F32), jax.ShapeDtypeStruct((ts, N_QK), F32)],
        compiler_params=_params(("arbitrary",)),
        name="inproj_sample",
    )(xs, wmain, wga, w2, b2, gpre_mix)

    st_spec = pl.BlockSpec((nb_mix, N_HEADS, D_K, D_V), lambda i: (i, 0, 0, 0))
    merged_s, sg_s, sr_s = pl.pallas_call(
        functools.partial(_mixer_sample_kernel, nb=nb_mix, ls=ls),
        grid=(bs // nb_mix,),
        in_specs=[_row_spec(rows_mix, N_MAIN), _row_spec(rows_mix, N_QK),
                  _const_spec((rows_mix, D_K)), _const_spec((rows_mix, D_K)),
                  _const_spec((N_HEADS, rows_mix, rows_mix)), st_spec, st_spec,
                  _const_spec((1, D_MODEL)), _const_spec((1, D_MODEL))],
        out_specs=[_row_spec(rows_mix, D_MODEL), st_spec, st_spec],
        out_shape=[jax.ShapeDtypeStruct((ts, D_MODEL), BF16),
                   jax.ShapeDtypeStruct((bs, N_HEADS, D_K, D_V), F32),
                   jax.ShapeDtypeStruct((bs, N_HEADS, D_K, D_V), F32)],
        scratch_shapes=[pltpu.VMEM((rows_mix, D_MODEL), F32), pltpu.VMEM((rows_mix, D_MODEL), F32)],
        compiler_params=_params(("arbitrary",)),
        name="mixer_sample",
    )(z_s, la_s, cos_s, sin_s, dm_s, state_gla[0], state_ret[0], ggla, gret)

    def proj_res(a, res, wgt, g):
        return pl.pallas_call(
            _proj_res_kernel,
            grid=(ts // tm,),
            in_specs=[_row_spec(tm, D_MODEL), _row_spec(tm, D_MODEL), _const_spec((D_MODEL, D_MODEL)),
                      _const_spec((1, D_MODEL))],
            out_specs=_row_spec(tm, D_MODEL),
            out_shape=jax.ShapeDtypeStruct((ts, D_MODEL), F32),
            compiler_params=_params(("arbitrary",)),
            name="proj_res_sample",
        )(a, res, wgt, g)

    x1s = proj_res(merged_s, xs, wmix, gpost_mix)
    q_s = pl.pallas_call(
        _norm_mm_kernel,
        grid=(ts // tm,),
        in_specs=[_row_spec(tm, D_MODEL), _const_spec((1, D_MODEL)), _const_spec((D_MODEL, D_MODEL))],
        out_specs=_row_spec(tm, D_MODEL),
        out_shape=jax.ShapeDtypeStruct((ts, D_MODEL), F32),
        compiler_params=_params(("arbitrary",)),
        name="xq_sample",
    )(x1s, gpre_xa, wxq)

    rows_xa = nb_xa * ls
    kv_spec = pl.BlockSpec(memory_space=pl.ANY)
    qo_spec = pl.BlockSpec((rows_xa, XA_DH), lambda i, h: (i, h))
    o_s = pl.pallas_call(
        functools.partial(_xattn_sample_kernel, nb=nb_xa, ls=ls),
        grid=(bs // nb_xa, N_HEADS),
        in_specs=[qo_spec, kv_spec, kv_spec],
        out_specs=qo_spec,
        out_shape=jax.ShapeDtypeStruct((ts, D_MODEL), BF16),
        scratch_shapes=[pltpu.VMEM((2, nb_xa, N_MEM, XA_DH), F32), pltpu.VMEM((2, nb_xa, N_MEM, XA_DH), F32),
                        pltpu.SemaphoreType.DMA((2, 2))],
        compiler_params=_params(("arbitrary", "arbitrary")),
        name="xattn_sample",
    )(q_s, cache_mem_k[0], cache_mem_v[0])

    x2s = proj_res(o_s, x1s, wxo, gpost_xa)
    ys = _mlp(x2s, wup, wdown, gpre_ffn, gpost_ffn, tm).reshape(bs, ls, D_MODEL)

    hshape = (1, bp, N_MEM, N_HEADS, XA_DH)
    return (yp, ys, sg_p[None], sr_p[None], mk.reshape(hshape), mv.reshape(hshape), sg_s[None], sr_s[None])
```

```python
import functools
import math

import jax
import jax.numpy as jnp
from jax import lax
from jax.experimental import pallas as pl
from jax.experimental.pallas import tpu as pltpu

F32 = jnp.float32
BF16 = jnp.bfloat16

D_MODEL = 1024
N_HEADS = 4
D_K = 128
D_V = 256
N_MEM = 256
XA_DH = 256
D_FF = 4 * D_MODEL
GATE_RANK = 16
GATE_RANK_PAD = 128
GLA_TAU = 16.0
CHUNK = 64
ROPE_BASE = 10000.0
PAST_LEN = 16384
EPS = 1e-6
LOG_GAMMA = tuple(math.log1p(-(2.0 ** (-5.0 - h))) for h in range(N_HEADS))

C_GQ, C_GK, C_GV, C_GR = 0, 512, 1024, 2048
C_RQ, C_RK, C_RV, C_RG = 3072, 3584, 4096, 5120
C_GA, C_GB = 6144, 7168
N_MAIN = 8192
N_QK = N_HEADS * D_K

LANE = 128
VMEM_LIMIT_BYTES = 56 * 1024 * 1024


def _mm(a, b):
    return jnp.dot(a, b, preferred_element_type=F32)


def _mm_nt(a, b):
    return lax.dot_general(a, b, (((1,), (1,)), ((), ())), preferred_element_type=F32)


def _mm_tn(a, b):
    return lax.dot_general(a, b, (((0,), (0,)), ((), ())), preferred_element_type=F32)


def _rms(x, g):
    return x * lax.rsqrt(jnp.mean(x * x, axis=-1, keepdims=True) + EPS) * g


def _sigmoid(x):
    return 1.0 / (1.0 + jnp.exp(-x))


def _const_spec(shape):
    nd = len(shape)
    return pl.BlockSpec(shape, lambda *_: (0,) * nd, pipeline_mode=pl.Buffered(1))


def _params(sem):
    return pltpu.CompilerParams(dimension_semantics=sem, vmem_limit_bytes=VMEM_LIMIT_BYTES)


def _inproj(h_bf, wmain_ref, wga_ref, w2_ref, b2_ref, z_ref, la_ref):
    for j in range(0, N_MAIN, 1024):
        z_ref[:, j:j + 1024] = _mm(h_bf, wmain_ref[:, j:j + 1024])
    ga = _mm(h_bf, wga_ref[...])
    xg = _mm(ga.astype(BF16), w2_ref[...]) + b2_ref[...]
    log_sig = jnp.minimum(xg, 0.0) - jnp.log1p(jnp.exp(-jnp.abs(xg)))
    la_ref[...] = log_sig * (1.0 / GLA_TAU)


def _block_causal(n, c):
    shift = c.bit_length() - 1
    row = lax.broadcasted_iota(jnp.int32, (n, n), 0)
    col = lax.broadcasted_iota(jnp.int32, (n, n), 1)
    return ((row >> shift) == (col >> shift)) & (row >= col)


def _gla_tile(z_ref, la_ref, n, c, s_get, s_put, g_ref, o_ref):
    causal = _block_causal(n, c)
    tri = jnp.where(causal, 1.0, 0.0).astype(BF16)
    la = la_ref[...]
    la_hi = la.astype(BF16)
    la_lo = (la - la_hi.astype(F32)).astype(BF16)
    b = _mm(tri, la_hi) + _mm(tri, la_lo)
    n_chunks = n // c
    b_last = [b[(ci + 1) * c - 1:(ci + 1) * c, :] for ci in range(n_chunks)]
    b_last_full = jnp.concatenate([jnp.broadcast_to(bl, (c, N_QK)) for bl in b_last], axis=0)
    eb = jnp.exp(b)
    enb = jnp.exp(-b)
    ekd = jnp.exp(b_last_full - b)
    for h in range(N_HEADS):
        ks = slice(h * D_K, (h + 1) * D_K)
        q = z_ref[:, C_GQ + h * D_K:C_GQ + (h + 1) * D_K] * (D_K ** -0.5)
        k = z_ref[:, C_GK + h * D_K:C_GK + (h + 1) * D_K]
        v = z_ref[:, C_GV + h * D_V:C_GV + (h + 1) * D_V].astype(BF16)
        qe = (q * eb[:, ks]).astype(BF16)
        ke = (k * enb[:, ks]).astype(BF16)
        kd = (k * ekd[:, ks]).astype(BF16)
        sc = jnp.where(causal, _mm_nt(qe, ke), 0.0).astype(BF16)
        o_intra = _mm(sc, v)
        outs = []
        for ci in range(n_chunks):
            r = slice(ci * c, (ci + 1) * c)
            s_old = s_get(ci, h)
            outs.append(o_intra[r] + _mm(qe[r], s_old.astype(BF16)))
            dcol = jnp.transpose(jnp.broadcast_to(jnp.exp(b_last[ci][:, ks]), (D_K, D_K)))
            s_put(ci, h, jnp.concatenate([dcol, dcol], axis=1) * s_old + _mm_tn(kd[r], v[r]))
        o = jnp.concatenate(outs, axis=0) if n_chunks > 1 else outs[0]
        ms = jnp.mean(o * o, axis=-1, keepdims=True)
        o_ref[:, h * D_V:(h + 1) * D_V] = o * lax.rsqrt(ms + EPS) * g_ref[:, h * D_V:(h + 1) * D_V]


def _ret_tile(z_ref, cos, sin, n, c, s_get, s_put, g_ref, o_ref, dm_ref):
    tpos = (lax.broadcasted_iota(jnp.int32, (n, 1), 0) & (c - 1)).astype(F32)
    n_chunks = n // c
    for h in range(N_HEADS):
        lg = LOG_GAMMA[h]
        q = z_ref[:, C_RQ + h * D_K:C_RQ + (h + 1) * D_K]
        k = z_ref[:, C_RK + h * D_K:C_RK + (h + 1) * D_K]
        v = z_ref[:, C_RV + h * D_V:C_RV + (h + 1) * D_V].astype(BF16)
        q = q * cos + pltpu.roll(q, D_K // 2, 1) * sin
        k = (k * cos + pltpu.roll(k, D_K // 2, 1) * sin) * (D_K ** -0.5)
        sc = (_mm_nt(q.astype(BF16), k.astype(BF16)) * dm_ref[h]).astype(BF16)
        o_intra = _mm(sc, v)
        qd = (q * jnp.exp(lg * (tpos + 1.0))).astype(BF16)
        kd = (k * jnp.exp(lg * (float(c - 1) - tpos))).astype(BF16)
        outs = []
        for ci in range(n_chunks):
            r = slice(ci * c, (ci + 1) * c)
            s_old = s_get(ci, h)
            outs.append(o_intra[r] + _mm(qd[r], s_old.astype(BF16)))
            s_put(ci, h, math.exp(lg * c) * s_old + _mm_tn(kd[r], v[r]))
        o = jnp.concatenate(outs, axis=0) if n_chunks > 1 else outs[0]
        mu = jnp.mean(o, axis=-1, keepdims=True)
        oc = o - mu
        var = jnp.mean(oc * oc, axis=-1, keepdims=True)
        o_ref[:, h * D_V:(h + 1) * D_V] = oc * lax.rsqrt(var + EPS) * g_ref[:, h * D_V:(h + 1) * D_V]


def _merge(z_ref, og_ref, or_ref, merged_ref):
    for j in range(0, D_MODEL, 256):
        cs = slice(j, j + 256)
        gr = z_ref[:, C_GR + j:C_GR + j + 256]
        rg = z_ref[:, C_RG + j:C_RG + j + 256]
        ga = z_ref[:, C_GA + j:C_GA + j + 256]
        gb = z_ref[:, C_GB + j:C_GB + j + 256]
        o_g = og_ref[:, cs] * (gr * _sigmoid(gr))
        o_r = or_ref[:, cs] * (rg * _sigmoid(rg))
        merged_ref[:, cs] = (_sigmoid(ga) * o_g + _sigmoid(gb) * o_r).astype(merged_ref.dtype)


def _attend(q, k_ref, v_ref, o_ref, rows):
    for h in range(N_HEADS):
        hs = slice(h * XA_DH, (h + 1) * XA_DH)
        s = _mm_nt(q[:, hs].astype(BF16), k_ref[:, hs].astype(BF16)) * (XA_DH ** -0.5)
        p = jnp.exp(s - jnp.max(s, axis=-1, keepdims=True))
        p = p * (1.0 / jnp.sum(p, axis=-1, keepdims=True))
        o_ref[rows, hs] = _mm(p.astype(BF16), v_ref[:, hs].astype(BF16)).astype(o_ref.dtype)


def _memkv_kernel(m_ref, g_ref, wk_ref, wv_ref, k_ref, v_ref, kb_ref, vb_ref):
    m = _rms(m_ref[...], g_ref[...]).astype(BF16)
    k = _mm(m, wk_ref[...])
    v = _mm(m, wv_ref[...])
    k_ref[...] = k
    v_ref[...] = v
    kb_ref[...] = k.astype(BF16)
    vb_ref[...] = v.astype(BF16)


def _mixer_prompt_kernel(x_ref, cos_ref, sin_ref, dm_ref, wmain_ref, wga_ref, w2_ref, b2_ref, ggla_ref, gret_ref,
                         wmix_ref, gpre_ref, gpost_ref,
                         y_ref, sg_ref, sr_ref,
                         z_ref, la_ref, og_ref, or_ref, merged_ref, *, tl):
    @pl.when(pl.program_id(1) == 0)
    def _():
        sg_ref[...] = jnp.zeros_like(sg_ref)
        sr_ref[...] = jnp.zeros_like(sr_ref)

    x = x_ref[0]
    h_bf = _rms(x, gpre_ref[...]).astype(BF16)
    _inproj(h_bf, wmain_ref, wga_ref, w2_ref, b2_ref, z_ref, la_ref)

    n_chunks = tl // CHUNK
    carried = {}

    def gla_get(ci, h):
        return sg_ref[0, h] if ci == 0 else carried[h]

    def gla_put(ci, h, val):
        carried[h] = val
        if ci == n_chunks - 1:
            sg_ref[0, h] = val

    _gla_tile(z_ref, la_ref, tl, CHUNK, gla_get, gla_put, ggla_ref, og_ref)

    def ret_put(ci, h, val):
        sr_ref[0, h] = val

    _ret_tile(z_ref, cos_ref[...], sin_ref[...], tl, tl, lambda ci, h: sr_ref[0, h], ret_put,
              gret_ref, or_ref, dm_ref)

    _merge(z_ref, og_ref, or_ref, merged_ref)
    m = _mm(merged_ref[...], wmix_ref[...])
    y_ref[0] = x + _rms(m, gpost_ref[...])


def _xattn_prompt_kernel(x_ref, mk_ref, mv_ref, wq_ref, wo_ref, gpre_ref, gpost_ref, y_ref, o_ref):
    x = x_ref[0]
    q = _mm(_rms(x, gpre_ref[...]).astype(BF16), wq_ref[...])
    _attend(q, mk_ref.at[0], mv_ref.at[0], o_ref, slice(None))
    a = _mm(o_ref[...], wo_ref[...])
    y_ref[0] = x + _rms(a, gpost_ref[...])


def _mlp_kernel(x_ref, wup_ref, wdown_ref, gpre_ref, gpost_ref, y_ref):
    x = x_ref[...]
    h_bf = _rms(x, gpre_ref[...]).astype(BF16)
    acc = None
    for j in range(0, D_FF, 1024):
        u = jnp.maximum(_mm(h_bf, wup_ref[:, j:j + 1024]), 0.0)
        part = _mm((u * u).astype(BF16), wdown_ref[j:j + 1024, :])
        acc = part if acc is None else acc + part
    y_ref[...] = x + _rms(acc, gpost_ref[...])


def _inproj_kernel(x_ref, wmain_ref, wga_ref, w2_ref, b2_ref, gpre_ref, z_ref, la_ref):
    h_bf = _rms(x_ref[...], gpre_ref[...]).astype(BF16)
    _inproj(h_bf, wmain_ref, wga_ref, w2_ref, b2_ref, z_ref, la_ref)


def _mixer_sample_kernel(z_ref, la_ref, cos_ref, sin_ref, dm_ref, sgi_ref, sri_ref, ggla_ref, gret_ref,
                         merged_ref, sgo_ref, sro_ref, og_ref, or_ref, *, nb, ls):
    def gla_put(ci, h, val):
        sgo_ref[ci, h] = val

    def ret_put(ci, h, val):
        sro_ref[ci, h] = val

    _gla_tile(z_ref, la_ref, nb * ls, ls, lambda ci, h: sgi_ref[ci, h], gla_put, ggla_ref, og_ref)
    _ret_tile(z_ref, cos_ref[...], sin_ref[...], nb * ls, ls, lambda ci, h: sri_ref[ci, h], ret_put,
              gret_ref, or_ref, dm_ref)
    _merge(z_ref, og_ref, or_ref, merged_ref)


def _proj_res_kernel(a_ref, res_ref, w_ref, g_ref, y_ref):
    y_ref[...] = res_ref[...] + _rms(_mm(a_ref[...], w_ref[...]), g_ref[...])


def _norm_mm_kernel(x_ref, g_ref, w_ref, y_ref):
    y_ref[...] = _mm(_rms(x_ref[...], g_ref[...]).astype(BF16), w_ref[...])


def _xattn_sample_kernel(q_ref, k_ref, v_ref, o_ref, *, nb, ls):
    n_rows = N_HEADS * ls
    n_cols = N_MEM * N_HEADS
    row_head = lax.broadcasted_iota(jnp.int32, (n_rows, n_cols), 0) // ls
    col_head = lax.broadcasted_iota(jnp.int32, (n_rows, n_cols), 1) & (N_HEADS - 1)
    own = row_head == col_head

    def seq_body(e, carry):
        rows = pl.ds(pl.multiple_of(e * ls, ls), ls)
        q = q_ref[rows, :]
        q_hm = jnp.concatenate([q[:, h * XA_DH:(h + 1) * XA_DH] for h in range(N_HEADS)], axis=0)
        k_all = k_ref[e].reshape(n_cols, XA_DH)
        v_all = v_ref[e].reshape(n_cols, XA_DH)
        s = _mm_nt(q_hm.astype(BF16), k_all.astype(BF16)) * (XA_DH ** -0.5)
        s = jnp.where(own, s, -jnp.inf)
        p = jnp.exp(s - jnp.max(s, axis=-1, keepdims=True))
        p = p * (1.0 / jnp.sum(p, axis=-1, keepdims=True))
        o = _mm(p.astype(BF16), v_all.astype(BF16))
        for h in range(N_HEADS):
            o_ref[rows, h * XA_DH:(h + 1) * XA_DH] = o[h * ls:(h + 1) * ls, :].astype(o_ref.dtype)
        return carry

    lax.fori_loop(0, nb, seq_body, 0, unroll=2)


def _rope_tables(pos):
    half = D_K // 2
    inv = ROPE_BASE ** (-jnp.arange(half, dtype=F32) / half)
    ang = pos.astype(F32)[:, None] * inv[None, :]
    cos, sin = jnp.cos(ang), jnp.sin(ang)
    return jnp.concatenate([cos, cos], axis=-1), jnp.concatenate([-sin, sin], axis=-1)


def _decay_masks(n, c):
    t = jnp.arange(n, dtype=jnp.int32)
    keep = ((t[:, None] // c) == (t[None, :] // c)) & (t[:, None] >= t[None, :])
    dist = (t[:, None] - t[None, :]).astype(F32)
    lg = jnp.asarray(LOG_GAMMA, F32)[:, None, None]
    return jnp.where(keep[None], jnp.exp(lg * dist[None]), 0.0)


def _row_spec(tm, n):
    return pl.BlockSpec((tm, n), lambda i: (i, 0))


def _mlp(x, wup, wdown, gpre, gpost, tm):
    t = x.shape[0]
    return pl.pallas_call(
        _mlp_kernel,
        grid=(t // tm,),
        in_specs=[_row_spec(tm, D_MODEL), _const_spec((D_MODEL, D_FF)), _const_spec((D_FF, D_MODEL)),
                  _const_spec((1, D_MODEL)), _const_spec((1, D_MODEL))],
        out_specs=_row_spec(tm, D_MODEL),
        out_shape=jax.ShapeDtypeStruct((t, D_MODEL), F32),
        compiler_params=_params(("arbitrary",)),
        name="mlp",
    )(x, wup, wdown, gpre, gpost)


def kernel(x_prompt, x_sample, state_gla, state_ret, cache_mem_k, cache_mem_v, mem_prompt, w_in, w_gla_a2, b_gla_a, g_gla_head, g_ret_head, w_mix_out, w_xq, w_xk, w_xv, w_xo, g_mem, w_up, w_down, g_pre_mix, g_post_mix, g_pre_xa, g_post_xa, g_pre_ffn, g_post_ffn):
    depth = w_in.shape[0]
    assert depth == 1
    bp, lp, _ = x_prompt.shape
    bs, ls, _ = x_sample.shape
    tl = 256
    tm = 512
    nb_mix = 8
    nb_xa = 8
    assert lp % tl == 0 and tl % CHUNK == 0 and lp % tm == 0
    assert (bs * ls) % tm == 0 and bs % nb_mix == 0 and bs % nb_xa == 0
    assert ls % 8 == 0 and ls <= CHUNK and ls & (ls - 1) == 0 and tl & (tl - 1) == 0
    assert (bp * N_MEM) % tm == 0

    w = w_in[0]
    c_low = 2 * N_QK + 2 * D_MODEL
    wmain = jnp.concatenate([w[:, :c_low], w[:, c_low + GATE_RANK:]], axis=1).astype(BF16)
    wga = jnp.pad(w[:, c_low:c_low + GATE_RANK], ((0, 0), (0, GATE_RANK_PAD - GATE_RANK))).astype(BF16)
    w2 = jnp.pad(w_gla_a2[0], ((0, GATE_RANK_PAD - GATE_RANK), (0, 0))).astype(BF16)
    b2 = b_gla_a[0].reshape(1, N_QK)
    ggla = g_gla_head[0].reshape(1, D_MODEL)
    gret = g_ret_head[0].reshape(1, D_MODEL)
    wmix = w_mix_out[0].astype(BF16)
    wxq, wxk, wxv, wxo = (t[0].astype(BF16) for t in (w_xq, w_xk, w_xv, w_xo))
    wup, wdown = w_up[0].astype(BF16), w_down[0].astype(BF16)
    row = lambda g: g[0].reshape(1, D_MODEL)
    gmem, gpre_mix, gpost_mix, gpre_xa, gpost_xa, gpre_ffn, gpost_ffn = (
        row(g) for g in (g_mem, g_pre_mix, g_post_mix, g_pre_xa, g_post_xa, g_pre_ffn, g_post_ffn))
    cos_p, sin_p = _rope_tables(jnp.arange(lp, dtype=jnp.int32))
    cos_s, sin_s = _rope_tables(PAST_LEN + jnp.arange(ls, dtype=jnp.int32))
    rows_mix = nb_mix * ls
    cos_s, sin_s = jnp.tile(cos_s, (nb_mix, 1)), jnp.tile(sin_s, (nb_mix, 1))
    dm_p = _decay_masks(tl, tl)
    dm_s = _decay_masks(rows_mix, ls)

    nmem_rows = bp * N_MEM
    mk, mv, mk_bf, mv_bf = pl.pallas_call(
        _memkv_kernel,
        grid=(nmem_rows // tm,),
        in_specs=[_row_spec(tm, D_MODEL), _const_spec((1, D_MODEL)),
                  _const_spec((D_MODEL, D_MODEL)), _const_spec((D_MODEL, D_MODEL))],
        out_specs=[_row_spec(tm, D_MODEL)] * 4,
        out_shape=[jax.ShapeDtypeStruct((nmem_rows, D_MODEL), F32)] * 2
        + [jax.ShapeDtypeStruct((nmem_rows, D_MODEL), BF16)] * 2,
        compiler_params=_params(("arbitrary",)),
        name="memkv",
    )(mem_prompt.reshape(nmem_rows, D_MODEL), gmem, wxk, wxv)

    state_spec = pl.BlockSpec((1, N_HEADS, D_K, D_V), lambda b, t: (b, 0, 0, 0))
    x1p, sg_p, sr_p = pl.pallas_call(
        functools.partial(_mixer_prompt_kernel, tl=tl),
        grid=(bp, lp // tl),
        in_specs=[pl.BlockSpec((1, tl, D_MODEL), lambda b, t: (b, t, 0)),
                  pl.BlockSpec((tl, D_K), lambda b, t: (t, 0)),
                  pl.BlockSpec((tl, D_K), lambda b, t: (t, 0)),
                  _const_spec((N_HEADS, tl, tl)),
                  _const_spec((D_MODEL, N_MAIN)), _const_spec((D_MODEL, GATE_RANK_PAD)),
                  _const_spec((GATE_RANK_PAD, N_QK)), _const_spec((1, N_QK)),
                  _const_spec((1, D_MODEL)), _const_spec((1, D_MODEL)),
                  _const_spec((D_MODEL, D_MODEL)), _const_spec((1, D_MODEL)), _const_spec((1, D_MODEL))],
        out_specs=[pl.BlockSpec((1, tl, D_MODEL), lambda b, t: (b, t, 0)), state_spec, state_spec],
        out_shape=[jax.ShapeDtypeStruct((bp, lp, D_MODEL), F32),
                   jax.ShapeDtypeStruct((bp, N_HEADS, D_K, D_V), F32),
                   jax.ShapeDtypeStruct((bp, N_HEADS, D_K, D_V), F32)],
        scratch_shapes=[pltpu.VMEM((tl, N_MAIN), F32), pltpu.VMEM((tl, N_QK), F32),
                        pltpu.VMEM((tl, D_MODEL), F32), pltpu.VMEM((tl, D_MODEL), F32),
                        pltpu.VMEM((tl, D_MODEL), BF16)],
        compiler_params=_params(("arbitrary", "arbitrary")),
        name="mixer_prompt",
    )(x_prompt, cos_p, sin_p, dm_p, wmain, wga, w2, b2, ggla, gret, wmix, gpre_mix, gpost_mix)

    mem_spec = pl.BlockSpec((1, N_MEM, D_MODEL), lambda b, t: (b, 0, 0))
    x2p = pl.pallas_call(
        _xattn_prompt_kernel,
        grid=(bp, lp // tm),
        in_specs=[pl.BlockSpec((1, tm, D_MODEL), lambda b, t: (b, t, 0)), mem_spec, mem_spec,
                  _const_spec((D_MODEL, D_MODEL)), _const_spec((D_MODEL, D_MODEL)),
                  _const_spec((1, D_MODEL)), _const_spec((1, D_MODEL))],
        out_specs=pl.BlockSpec((1, tm, D_MODEL), lambda b, t: (b, t, 0)),
        out_shape=jax.ShapeDtypeStruct((bp, lp, D_MODEL), F32),
        scratch_shapes=[pltpu.VMEM((tm, D_MODEL), BF16)],
        compiler_params=_params(("arbitrary", "arbitrary")),
        name="xattn_prompt",
    )(x1p, mk_bf.reshape(bp, N_MEM, D_MODEL), mv_bf.reshape(bp, N_MEM, D_MODEL), wxq, wxo, gpre_xa, gpost_xa)

    yp = _mlp(x2p.reshape(bp * lp, D_MODEL), wup, wdown, gpre_ffn, gpost_ffn, tm).reshape(bp, lp, D_MODEL)

    ts = bs * ls
    xs = x_sample.reshape(ts, D_MODEL)
    z_s, la_s = pl.pallas_call(
        _inproj_kernel,
        grid=(ts // tm,),
        in_specs=[_row_spec(tm, D_MODEL), _const_spec((D_MODEL, N_MAIN)), _const_spec((D_MODEL, GATE_RANK_PAD)),
                  _const_spec((GATE_RANK_PAD, N_QK)), _const_spec((1, N_QK)), _const_spec((1, D_MODEL))],
        out_specs=[_row_spec(tm, N_MAIN), _row_spec(tm, N_QK)],
        out_shape=[jax.ShapeDtypeStruct((ts, N_MAIN), F32), jax.ShapeDtypeStruct((ts, N_QK), F32)],
        compiler_params=_params(("arbitrary",)),
        name="inproj_sample",
    )(xs, wmain, wga, w2, b2, gpre_mix)

    st_spec = pl.BlockSpec((nb_mix, N_HEADS, D_K, D_V), lambda i: (i, 0, 0, 0))
    merged_s, sg_s, sr_s = pl.pallas_call(
        functools.partial(_mixer_sample_kernel, nb=nb_mix, ls=ls),
        grid=(bs // nb_mix,),
        in_specs=[_row_spec(rows_mix, N_MAIN), _row_spec(rows_mix, N_QK),
                  _const_spec((rows_mix, D_K)), _const_spec((rows_mix, D_K)),
                  _const_spec((N_HEADS, rows_mix, rows_mix)), st_spec, st_spec,
                  _const_spec((1, D_MODEL)), _const_spec((1, D_MODEL))],
        out_specs=[_row_spec(rows_mix, D_MODEL), st_spec, st_spec],
        out_shape=[jax.ShapeDtypeStruct((ts, D_MODEL), BF16),
                   jax.ShapeDtypeStruct((bs, N_HEADS, D_K, D_V), F32),
                   jax.ShapeDtypeStruct((bs, N_HEADS, D_K, D_V), F32)],
        scratch_shapes=[pltpu.VMEM((rows_mix, D_MODEL), F32), pltpu.VMEM((rows_mix, D_MODEL), F32)],
        compiler_params=_params(("arbitrary",)),
        name="mixer_sample",
    )(z_s, la_s, cos_s, sin_s, dm_s, state_gla[0], state_ret[0], ggla, gret)

    def proj_res(a, res, wgt, g):
        return pl.pallas_call(
            _proj_res_kernel,
            grid=(ts // tm,),
            in_specs=[_row_spec(tm, D_MODEL), _row_spec(tm, D_MODEL), _const_spec((D_MODEL, D_MODEL)),
                      _const_spec((1, D_MODEL))],
            out_specs=_row_spec(tm, D_MODEL),
            out_shape=jax.ShapeDtypeStruct((ts, D_MODEL), F32),
            compiler_params=_params(("arbitrary",)),
            name="proj_res_sample",
        )(a, res, wgt, g)

    x1s = proj_res(merged_s, xs, wmix, gpost_mix)
    q_s = pl.pallas_call(
        _norm_mm_kernel,
        grid=(ts // tm,),
        in_specs=[_row_spec(tm, D_MODEL), _const_spec((1, D_MODEL)), _const_spec((D_MODEL, D_MODEL))],
        out_specs=_row_spec(tm, D_MODEL),
        out_shape=jax.ShapeDtypeStruct((ts, D_MODEL), F32),
        compiler_params=_params(("arbitrary",)),
        name="xq_sample",
    )(x1s, gpre_xa, wxq)

    rows_xa = nb_xa * ls
    kv_spec = pl.BlockSpec((nb_xa, N_MEM, N_HEADS, XA_DH), lambda i: (i, 0, 0, 0))
    o_s = pl.pallas_call(
        functools.partial(_xattn_sample_kernel, nb=nb_xa, ls=ls),
        grid=(bs // nb_xa,),
        in_specs=[_row_spec(rows_xa, D_MODEL), kv_spec, kv_spec],
        out_specs=_row_spec(rows_xa, D_MODEL),
        out_shape=jax.ShapeDtypeStruct((ts, D_MODEL), BF16),
        compiler_params=_params(("arbitrary",)),
        name="xattn_sample",
    )(q_s, cache_mem_k[0], cache_mem_v[0])

    x2s = proj_res(o_s, x1s, wxo, gpost_xa)
    ys = _mlp(x2s, wup, wdown, gpre_ffn, gpost_ffn, tm).reshape(bs, ls, D_MODEL)

    hshape = (1, bp, N_MEM, N_HEADS, XA_DH)
    return (yp, ys, sg_p[None], sr_p[None], mk.reshape(hshape), mv.reshape(hshape), sg_s[None], sr_s[None])
```

```python
import functools
import math

import jax
import jax.numpy as jnp
from jax import lax
from jax.experimental import pallas as pl
from jax.experimental.pallas import tpu as pltpu

F32 = jnp.float32
BF16 = jnp.bfloat16

D_MODEL = 1024
N_HEADS = 4
D_K = 128
D_V = 256
N_MEM = 256
XA_DH = 256
D_FF = 4 * D_MODEL
GATE_RANK = 16
GATE_RANK_PAD = 128
GLA_TAU = 16.0
CHUNK = 64
ROPE_BASE = 10000.0
PAST_LEN = 16384
EPS = 1e-6
LOG_GAMMA = tuple(math.log1p(-(2.0 ** (-5.0 - h))) for h in range(N_HEADS))

C_GQ, C_GK, C_GV, C_GR = 0, 512, 1024, 2048
C_RQ, C_RK, C_RV, C_RG = 3072, 3584, 4096, 5120
C_GA, C_GB = 6144, 7168
N_MAIN = 8192
N_WA = 3072
N_WB = N_MAIN - N_WA
N_QK = N_HEADS * D_K
PROJ_BLOCK = 512

LANE = 128
VMEM_LIMIT_BYTES = 60 * 1024 * 1024


def _mm(a, b):
    return jnp.dot(a, b, preferred_element_type=F32)


def _mm_nt(a, b):
    return lax.dot_general(a, b, (((1,), (1,)), ((), ())), preferred_element_type=F32)


def _mm_tn(a, b):
    return lax.dot_general(a, b, (((0,), (0,)), ((), ())), preferred_element_type=F32)


def _rms(x, g):
    return x * lax.rsqrt(jnp.mean(x * x, axis=-1, keepdims=True) + EPS) * g


def _sigmoid(x):
    return 1.0 / (1.0 + jnp.exp(-x))


def _const_spec(shape):
    nd = len(shape)
    return pl.BlockSpec(shape, lambda *_: (0,) * nd, pipeline_mode=pl.Buffered(1))


def _params(sem):
    return pltpu.CompilerParams(dimension_semantics=sem, vmem_limit_bytes=VMEM_LIMIT_BYTES)


def _inproj_pieces(h_bf, wa_ref, wb_ref, wga_ref, w2_ref, b2_ref, z_ref, la_ref):
    def z_block(j):
        def run():
            w_blk = wa_ref[:, j:j + PROJ_BLOCK] if j < N_WA else wb_ref[:, j - N_WA:j - N_WA + PROJ_BLOCK]
            z_ref[:, j:j + PROJ_BLOCK] = _mm(h_bf, w_blk)
        return run

    def gate():
        ga = _mm(h_bf, wga_ref[...])
        xg = _mm(ga.astype(BF16), w2_ref[...]) + b2_ref[...]
        log_sig = jnp.minimum(xg, 0.0) - jnp.log1p(jnp.exp(-jnp.abs(xg)))
        la_ref[...] = log_sig * (1.0 / GLA_TAU)

    return [z_block(j) for j in range(0, N_MAIN, PROJ_BLOCK)] + [gate]


def _inproj(h_bf, wa_ref, wb_ref, wga_ref, w2_ref, b2_ref, z_ref, la_ref):
    for piece in _inproj_pieces(h_bf, wa_ref, wb_ref, wga_ref, w2_ref, b2_ref, z_ref, la_ref):
        piece()


def _run_next(pieces):
    if pieces:
        pieces.pop(0)()


def _block_causal(n, c):
    shift = c.bit_length() - 1
    row = lax.broadcasted_iota(jnp.int32, (n, n), 0)
    col = lax.broadcasted_iota(jnp.int32, (n, n), 1)
    return ((row >> shift) == (col >> shift)) & (row >= col)


def _gla_tile(z_ref, la_ref, n, c, s_get, s_put, g_ref, o_ref, between=()):
    causal = _block_causal(n, c)
    tri = jnp.where(causal, 1.0, 0.0).astype(BF16)
    la = la_ref[...]
    la_hi = la.astype(BF16)
    la_lo = (la - la_hi.astype(F32)).astype(BF16)
    b = _mm(tri, la_hi) + _mm(tri, la_lo)
    n_chunks = n // c
    b_last = [b[(ci + 1) * c - 1:(ci + 1) * c, :] for ci in range(n_chunks)]
    b_last_full = jnp.concatenate([jnp.broadcast_to(bl, (c, N_QK)) for bl in b_last], axis=0)
    eb = jnp.exp(b)
    enb = jnp.exp(-b)
    ekd = jnp.exp(b_last_full - b)
    for h in range(N_HEADS):
        ks = slice(h * D_K, (h + 1) * D_K)
        q = z_ref[:, C_GQ + h * D_K:C_GQ + (h + 1) * D_K] * (D_K ** -0.5)
        k = z_ref[:, C_GK + h * D_K:C_GK + (h + 1) * D_K]
        v = z_ref[:, C_GV + h * D_V:C_GV + (h + 1) * D_V].astype(BF16)
        qe = (q * eb[:, ks]).astype(BF16)
        ke = (k * enb[:, ks]).astype(BF16)
        kd = (k * ekd[:, ks]).astype(BF16)
        sc = jnp.where(causal, _mm_nt(qe, ke), 0.0).astype(BF16)
        o_intra = _mm(sc, v)
        _run_next(between)
        outs = []
        for ci in range(n_chunks):
            r = slice(ci * c, (ci + 1) * c)
            s_old = s_get(ci, h)
            outs.append(o_intra[r] + _mm(qe[r], s_old.astype(BF16)))
            dcol = jnp.transpose(jnp.broadcast_to(jnp.exp(b_last[ci][:, ks]), (D_K, D_K)))
            s_put(ci, h, jnp.concatenate([dcol, dcol], axis=1) * s_old + _mm_tn(kd[r], v[r]))
        o = jnp.concatenate(outs, axis=0) if n_chunks > 1 else outs[0]
        ms = jnp.mean(o * o, axis=-1, keepdims=True)
        o_ref[:, h * D_V:(h + 1) * D_V] = o * lax.rsqrt(ms + EPS) * g_ref[:, h * D_V:(h + 1) * D_V]
        _run_next(between)


def _ret_tile(z_ref, cos, sin, n, c, s_get, s_put, g_ref, o_ref, dm_ref, between=()):
    tpos = (lax.broadcasted_iota(jnp.int32, (n, 1), 0) & (c - 1)).astype(F32)
    n_chunks = n // c
    for h in range(N_HEADS):
        lg = LOG_GAMMA[h]
        q = z_ref[:, C_RQ + h * D_K:C_RQ + (h + 1) * D_K]
        k = z_ref[:, C_RK + h * D_K:C_RK + (h + 1) * D_K]
        v = z_ref[:, C_RV + h * D_V:C_RV + (h + 1) * D_V].astype(BF16)
        q = q * cos + pltpu.roll(q, D_K // 2, 1) * sin
        k = (k * cos + pltpu.roll(k, D_K // 2, 1) * sin) * (D_K ** -0.5)
        sc = (_mm_nt(q.astype(BF16), k.astype(BF16)) * dm_ref[h]).astype(BF16)
        o_intra = _mm(sc, v)
        _run_next(between)
        qd =(q * jnp.exp(lg * (tpos + 1.0))).astype(BF16)
        kd = (k * jnp.exp(lg * (float(c - 1) - tpos))).astype(BF16)
        outs = []
        for ci in range(n_chunks):
            r = slice(ci * c, (ci + 1) * c)
            s_old = s_get(ci, h)
            outs.append(o_intra[r] + _mm(qd[r], s_old.astype(BF16)))
            s_put(ci, h, math.exp(lg * c) * s_old + _mm_tn(kd[r], v[r]))
        o = jnp.concatenate(outs, axis=0) if n_chunks > 1 else outs[0]
        mu = jnp.mean(o, axis=-1, keepdims=True)
        oc = o - mu
        var = jnp.mean(oc * oc, axis=-1, keepdims=True)
        o_ref[:, h * D_V:(h + 1) * D_V] = oc * lax.rsqrt(var + EPS) * g_ref[:, h * D_V:(h + 1) * D_V]
        _run_next(between)


def _merge(z_ref, og_ref, or_ref, merged_ref):
    for j in range(0, D_MODEL, 256):
        cs = slice(j, j + 256)
        gr = z_ref[:, C_GR + j:C_GR + j + 256]
        rg = z_ref[:, C_RG + j:C_RG + j + 256]
        ga = z_ref[:, C_GA + j:C_GA + j + 256]
        gb = z_ref[:, C_GB + j:C_GB + j + 256]
        o_g = og_ref[:, cs] * (gr * _sigmoid(gr))
        o_r = or_ref[:, cs] * (rg * _sigmoid(rg))
        merged_ref[:, cs] = (_sigmoid(ga) * o_g + _sigmoid(gb) * o_r).astype(merged_ref.dtype)


def _attend(q, k_ref, v_ref, o_ref, rows):
    for h in range(N_HEADS):
        hs = slice(h * XA_DH, (h + 1) * XA_DH)
        s = _mm_nt(q[:, hs].astype(BF16), k_ref[:, hs].astype(BF16)) * (XA_DH ** -0.5)
        p = jnp.exp(s - jnp.max(s, axis=-1, keepdims=True))
        p = p * (1.0 / jnp.sum(p, axis=-1, keepdims=True))
        o_ref[rows, hs] = _mm(p.astype(BF16), v_ref[:, hs].astype(BF16)).astype(o_ref.dtype)


def _memkv_kernel(m_ref, g_ref, wk_ref, wv_ref, k_ref, v_ref, kb_ref, vb_ref):
    m = _rms(m_ref[...], g_ref[...]).astype(BF16)
    k = _mm(m, wk_ref[...])
    v = _mm(m, wv_ref[...])
    k_ref[...] = k.reshape(k_ref.shape)
    v_ref[...] = v.reshape(v_ref.shape)
    kb_ref[...] = k.astype(BF16)
    vb_ref[...] = v.astype(BF16)


def _mixer_prompt_kernel(xc_ref, xn_ref, cos_ref, sin_ref, dm_ref, wa_ref, wb_ref, wga_ref, w2_ref, b2_ref,
                         ggla_ref, gret_ref, wmix_ref, gpre_ref, gpost_ref,
                         y_ref, sg_ref, sr_ref,
                         z_ref, la_ref, og_ref, or_ref, merged_ref, *, tl, steps_per_row):
    step = pl.program_id(0)

    @pl.when(step % steps_per_row == 0)
    def _():
        sg_ref[...] = jnp.zeros_like(sg_ref)
        sr_ref[...] = jnp.zeros_like(sr_ref)

    def project_pieces(x, slot):
        h_bf = _rms(x, gpre_ref[...]).astype(BF16)
        return _inproj_pieces(h_bf, wa_ref, wb_ref, wga_ref, w2_ref, b2_ref, z_ref.at[slot], la_ref.at[slot])

    @pl.when(step == 0)
    def _():
        for piece in project_pieces(xc_ref[0:tl, :], 0):
            piece()

    n_chunks = tl // CHUNK
    for half in range(2):
        rows = slice(half * tl, (half + 1) * tl)
        pending = project_pieces(xc_ref[tl:2 * tl, :] if half == 0 else xn_ref[...], 1 - half)
        _run_next(pending)
        z_cur, la_cur = z_ref.at[half], la_ref.at[half]

        carried = {}

        def gla_get(ci, h, carried=carried):
            return sg_ref[0, h] if ci == 0 else carried[h]

        def gla_put(ci, h, val, carried=carried):
            carried[h] = val
            if ci == n_chunks - 1:
                sg_ref[0, h] = val

        _gla_tile(z_cur, la_cur, tl, CHUNK, gla_get, gla_put, ggla_ref, og_ref, between=pending)

        def ret_put(ci, h, val):
            sr_ref[0, h] = val

        _ret_tile(z_cur, cos_ref[rows, :], sin_ref[rows, :], tl, tl, lambda ci, h: sr_ref[0, h], ret_put,
                  gret_ref, or_ref, dm_ref, between=pending)
        while pending:
            _run_next(pending)

        _merge(z_cur, og_ref, or_ref, merged_ref)
        m = _mm(merged_ref[...], wmix_ref[...])
        y_ref[rows, :] = xc_ref[rows, :] + _rms(m, gpost_ref[...])


def _xattn_prompt_kernel(x_ref, mk_ref, mv_ref, wq_ref, wo_ref, gpre_ref, gpost_ref, y_ref, o_ref):
    x = x_ref[0]
    q = _mm(_rms(x, gpre_ref[...]).astype(BF16), wq_ref[...])
    _attend(q, mk_ref.at[0], mv_ref.at[0], o_ref, slice(None))
    a = _mm(o_ref[...], wo_ref[...])
    y_ref[0] = x + _rms(a, gpost_ref[...])


def _mlp_kernel(x_ref, wup_ref, wdown_ref, gpre_ref, gpost_ref, y_ref):
    x = x_ref[...]
    h_bf = _rms(x, gpre_ref[...]).astype(BF16)
    acc = None
    for j in range(0, D_FF, 1024):
        u = jnp.maximum(_mm(h_bf, wup_ref[:, j:j + 1024]), 0.0)
        part = _mm((u * u).astype(BF16), wdown_ref[j:j + 1024, :])
        acc = part if acc is None else acc + part
    y_ref[...] = x + _rms(acc, gpost_ref[...])


def _inproj_kernel(x_ref, wa_ref, wb_ref, wga_ref, w2_ref, b2_ref, gpre_ref, z_ref, la_ref):
    h_bf = _rms(x_ref[...], gpre_ref[...]).astype(BF16)
    _inproj(h_bf, wa_ref, wb_ref, wga_ref, w2_ref, b2_ref, z_ref, la_ref)


def _mixer_sample_kernel(z_ref, la_ref, cos_ref, sin_ref, dm_ref, sgi_ref, sri_ref, ggla_ref, gret_ref,
                         merged_ref, sgo_ref, sro_ref, og_ref, or_ref, *, nb, ls):
    def gla_put(ci, h, val):
        sgo_ref[ci, h] = val

    def ret_put(ci, h, val):
        sro_ref[ci, h] = val

    _gla_tile(z_ref, la_ref, nb * ls, ls, lambda ci, h: sgi_ref[ci, h], gla_put, ggla_ref, og_ref)
    _ret_tile(z_ref, cos_ref[...], sin_ref[...], nb * ls, ls, lambda ci, h: sri_ref[ci, h], ret_put,
              gret_ref, or_ref, dm_ref)
    _merge(z_ref, og_ref, or_ref, merged_ref)


def _proj_res_kernel(a_ref, res_ref, w_ref, g_ref, y_ref):
    y_ref[...] = res_ref[...] + _rms(_mm(a_ref[...], w_ref[...]), g_ref[...])


def _norm_mm_kernel(x_ref, g_ref, w_ref, y_ref):
    y_ref[...] = _mm(_rms(x_ref[...], g_ref[...]).astype(BF16), w_ref[...])


def _xattn_sample_kernel(q_ref, k_ref, v_ref, o_ref, *, nb, ls):
    n_rows = N_HEADS * ls
    n_cols = N_MEM * N_HEADS
    row_head = lax.broadcasted_iota(jnp.int32, (n_rows, n_cols), 0) // ls
    col_head = lax.broadcasted_iota(jnp.int32, (n_rows, n_cols), 1) & (N_HEADS - 1)
    own = row_head == col_head

    def seq_body(e, carry):
        rows = pl.ds(pl.multiple_of(e * ls, ls), ls)
        q = q_ref[rows, :]
        q_hm = jnp.concatenate([q[:, h * XA_DH:(h + 1) * XA_DH] for h in range(N_HEADS)], axis=0)
        k_all = k_ref[e].reshape(n_cols, XA_DH)
        v_all = v_ref[e].reshape(n_cols, XA_DH)
        s = _mm_nt(q_hm.astype(BF16), k_all.astype(BF16)) * (XA_DH ** -0.5)
        s = jnp.where(own, s, -jnp.inf)
        p = jnp.exp(s - jnp.max(s, axis=-1, keepdims=True))
        p = p * (1.0 / jnp.sum(p, axis=-1, keepdims=True))
        o = _mm(p.astype(BF16), v_all.astype(BF16))
        for h in range(N_HEADS):
            o_ref[rows, h * XA_DH:(h + 1) * XA_DH] = o[h * ls:(h + 1) * ls, :].astype(o_ref.dtype)
        return carry

    lax.fori_loop(0, nb, seq_body, 0, unroll=2)


def _rope_tables(pos):
    half = D_K // 2
    inv = ROPE_BASE ** (-jnp.arange(half, dtype=F32) / half)
    ang = pos.astype(F32)[:, None] * inv[None, :]
    cos, sin = jnp.cos(ang), jnp.sin(ang)
    return jnp.concatenate([cos, cos], axis=-1), jnp.concatenate([-sin, sin], axis=-1)


def _decay_masks(n, c):
    t = jnp.arange(n, dtype=jnp.int32)
    keep = ((t[:, None] // c) == (t[None, :] // c)) & (t[:, None] >= t[None, :])
    dist = (t[:, None] - t[None, :]).astype(F32)
    lg = jnp.asarray(LOG_GAMMA, F32)[:, None, None]
    return jnp.where(keep[None], jnp.exp(lg * dist[None]), 0.0)


def _row_spec(tm, n):
    return pl.BlockSpec((tm, n), lambda i: (i, 0))


def _mlp(x, wup, wdown, gpre, gpost, tm):
    t = x.shape[0]
    return pl.pallas_call(
        _mlp_kernel,
        grid=(t // tm,),
        in_specs=[_row_spec(tm, D_MODEL), _const_spec((D_MODEL, D_FF)), _const_spec((D_FF, D_MODEL)),
                  _const_spec((1, D_MODEL)), _const_spec((1, D_MODEL))],
        out_specs=_row_spec(tm, D_MODEL),
        out_shape=jax.ShapeDtypeStruct((t, D_MODEL), F32),
        compiler_params=_params(("arbitrary",)),
        name="mlp",
    )(x, wup, wdown, gpre, gpost)


def kernel(x_prompt, x_sample, state_gla, state_ret, cache_mem_k, cache_mem_v, mem_prompt, w_in, w_gla_a2, b_gla_a, g_gla_head, g_ret_head, w_mix_out, w_xq, w_xk, w_xv, w_xo, g_mem, w_up, w_down, g_pre_mix, g_post_mix, g_pre_xa, g_post_xa, g_pre_ffn, g_post_ffn):
    depth = w_in.shape[0]
    assert depth == 1
    bp, lp, _ = x_prompt.shape
    bs, ls, _ = x_sample.shape
    tl = 256
    tm = 512
    nb_mix = 8
    nb_xa = 8
    assert lp % (2 * tl) == 0 and tl % CHUNK == 0 and lp % tm == 0
    assert (bs * ls) % tm == 0 and bs % nb_mix == 0 and bs % nb_xa == 0
    assert ls % 8 == 0 and ls <= CHUNK and ls & (ls - 1) == 0 and tl & (tl - 1) == 0
    assert (bp * N_MEM) % tm == 0

    w = w_in[0]
    c_low = 2 * N_QK + 2 * D_MODEL
    assert c_low == N_WA
    wa = w[:, :c_low].astype(BF16)
    wb = w[:, c_low + GATE_RANK:].astype(BF16)
    wga = jnp.pad(w[:, c_low:c_low + GATE_RANK], ((0, 0), (0, GATE_RANK_PAD - GATE_RANK))).astype(BF16)
    w2 = jnp.pad(w_gla_a2[0], ((0, GATE_RANK_PAD - GATE_RANK), (0, 0))).astype(BF16)
    b2 = b_gla_a[0].reshape(1, N_QK)
    ggla = g_gla_head[0].reshape(1, D_MODEL)
    gret = g_ret_head[0].reshape(1, D_MODEL)
    wmix = w_mix_out[0].astype(BF16)
    wxq, wxk, wxv, wxo = (t[0].astype(BF16) for t in (w_xq, w_xk, w_xv, w_xo))
    wup, wdown = w_up[0].astype(BF16), w_down[0].astype(BF16)
    row = lambda g: g[0].reshape(1, D_MODEL)
    gmem, gpre_mix, gpost_mix, gpre_xa, gpost_xa, gpre_ffn, gpost_ffn = (
        row(g) for g in (g_mem, g_pre_mix, g_post_mix, g_pre_xa, g_post_xa, g_pre_ffn, g_post_ffn))
    cos_p, sin_p = _rope_tables(jnp.arange(lp, dtype=jnp.int32))
    cos_s, sin_s = _rope_tables(PAST_LEN + jnp.arange(ls, dtype=jnp.int32))
    rows_mix = nb_mix * ls
    cos_s, sin_s = jnp.tile(cos_s, (nb_mix, 1)), jnp.tile(sin_s, (nb_mix, 1))
    dm_p = _decay_masks(tl, tl)
    dm_s = _decay_masks(rows_mix, ls)

    nmem_rows = bp * N_MEM
    mk, mv, mk_bf, mv_bf = pl.pallas_call(
        _memkv_kernel,
        grid=(nmem_rows // tm,),
        in_specs=[_row_spec(tm, D_MODEL), _const_spec((1, D_MODEL)),
                  _const_spec((D_MODEL, D_MODEL)), _const_spec((D_MODEL, D_MODEL))],
        out_specs=[pl.BlockSpec((tm, N_HEADS, XA_DH), lambda i: (i, 0, 0))] * 2 + [_row_spec(tm, D_MODEL)] * 2,
        out_shape=[jax.ShapeDtypeStruct((nmem_rows, N_HEADS, XA_DH), F32)] * 2
        + [jax.ShapeDtypeStruct((nmem_rows, D_MODEL), BF16)] * 2,
        compiler_params=_params(("arbitrary",)),
        name="memkv",
    )(mem_prompt.reshape(nmem_rows, D_MODEL), gmem, wxk, wxv)

    n_tiles = bp * lp // tl
    steps_per_row = lp // (2 * tl)
    state_spec = pl.BlockSpec((1, N_HEADS, D_K, D_V), lambda s: (s // steps_per_row, 0, 0, 0))
    rope_spec = pl.BlockSpec((2 * tl, D_K), lambda s: (s % steps_per_row, 0))
    x1p, sg_p, sr_p = pl.pallas_call(
        functools.partial(_mixer_prompt_kernel, tl=tl, steps_per_row=steps_per_row),
        grid=(n_tiles // 2,),
        in_specs=[_row_spec(2 * tl, D_MODEL),
                  pl.BlockSpec((tl, D_MODEL), lambda s: (jnp.minimum(2 * s + 2, n_tiles - 1), 0)),
                  rope_spec, rope_spec,
                  _const_spec((N_HEADS, tl, tl)),
                  _const_spec((D_MODEL, N_WA)), _const_spec((D_MODEL, N_WB)), _const_spec((D_MODEL, GATE_RANK_PAD)),
                  _const_spec((GATE_RANK_PAD, N_QK)), _const_spec((1, N_QK)),
                  _const_spec((1, D_MODEL)), _const_spec((1, D_MODEL)),
                  _const_spec((D_MODEL, D_MODEL)), _const_spec((1, D_MODEL)), _const_spec((1, D_MODEL))],
        out_specs=[_row_spec(2 * tl, D_MODEL), state_spec, state_spec],
        out_shape=[jax.ShapeDtypeStruct((bp * lp, D_MODEL), F32),
                   jax.ShapeDtypeStruct((bp, N_HEADS, D_K, D_V), F32),
                   jax.ShapeDtypeStruct((bp, N_HEADS, D_K, D_V), F32)],
        scratch_shapes=[pltpu.VMEM((2, tl, N_MAIN), F32), pltpu.VMEM((2, tl, N_QK), F32),
                        pltpu.VMEM((tl, D_MODEL), F32), pltpu.VMEM((tl, D_MODEL), F32),
                        pltpu.VMEM((tl, D_MODEL), BF16)],
        compiler_params=_params(("arbitrary",)),
        name="mixer_prompt",
    )(x_prompt.reshape(bp * lp, D_MODEL), x_prompt.reshape(bp * lp, D_MODEL), cos_p, sin_p, dm_p,
      wa, wb, wga, w2, b2, ggla, gret, wmix, gpre_mix, gpost_mix)
    x1p = x1p.reshape(bp, lp, D_MODEL)

    mem_spec = pl.BlockSpec((1, N_MEM, D_MODEL), lambda b, t: (b, 0, 0))
    x2p = pl.pallas_call(
        _xattn_prompt_kernel,
        grid=(bp, lp // tm),
        in_specs=[pl.BlockSpec((1, tm, D_MODEL), lambda b, t: (b, t, 0)), mem_spec, mem_spec,
                  _const_spec((D_MODEL, D_MODEL)), _const_spec((D_MODEL, D_MODEL)),
                  _const_spec((1, D_MODEL)), _const_spec((1, D_MODEL))],
        out_specs=pl.BlockSpec((1, tm, D_MODEL), lambda b, t: (b, t, 0)),
        out_shape=jax.ShapeDtypeStruct((bp, lp, D_MODEL), F32),
        scratch_shapes=[pltpu.VMEM((tm, D_MODEL), BF16)],
        compiler_params=_params(("arbitrary", "arbitrary")),
        name="xattn_prompt",
    )(x1p, mk_bf.reshape(bp, N_MEM, D_MODEL), mv_bf.reshape(bp, N_MEM, D_MODEL), wxq, wxo, gpre_xa, gpost_xa)

    yp = _mlp(x2p.reshape(bp * lp, D_MODEL), wup, wdown, gpre_ffn, gpost_ffn, tm).reshape(bp, lp, D_MODEL)

    ts = bs * ls
    xs = x_sample.reshape(ts, D_MODEL)
    z_s, la_s = pl.pallas_call(
        _inproj_kernel,
        grid=(ts // tm,),
        in_specs=[_row_spec(tm, D_MODEL), _const_spec((D_MODEL, N_WA)), _const_spec((D_MODEL, N_WB)),
                  _const_spec((D_MODEL, GATE_RANK_PAD)),
                  _const_spec((GATE_RANK_PAD, N_QK)), _const_spec((1, N_QK)), _const_spec((1, D_MODEL))],
        out_specs=[_row_spec(tm, N_MAIN), _row_spec(tm, N_QK)],
        out_shape=[jax.ShapeDtypeStruct((ts, N_MAIN), F32), jax.ShapeDtypeStruct((ts, N_QK), F32)],
        compiler_params=_params(("arbitrary",)),
        name="inproj_sample",
    )(xs, wa, wb, wga, w2, b2, gpre_mix)

    st_spec = pl.BlockSpec((nb_mix, N_HEADS, D_K, D_V), lambda i: (i, 0, 0, 0))
    merged_s, sg_s, sr_s = pl.pallas_call(
        functools.partial(_mixer_sample_kernel, nb=nb_mix, ls=ls),
        grid=(bs // nb_mix,),
        in_specs=[_row_spec(rows_mix, N_MAIN), _row_spec(rows_mix, N_QK),
                  _const_spec((rows_mix, D_K)), _const_spec((rows_mix, D_K)),
                  _const_spec((N_HEADS, rows_mix, rows_mix)), st_spec, st_spec,
                  _const_spec((1, D_MODEL)), _const_spec((1, D_MODEL))],
        out_specs=[_row_spec(rows_mix, D_MODEL), st_spec, st_spec],
        out_shape=[jax.ShapeDtypeStruct((ts, D_MODEL), BF16),
                   jax.ShapeDtypeStruct((bs, N_HEADS, D_K, D_V), F32),
                   jax.ShapeDtypeStruct((bs, N_HEADS, D_K, D_V), F32)],
        scratch_shapes=[pltpu.VMEM((rows_mix, D_MODEL), F32), pltpu.VMEM((rows_mix, D_MODEL), F32)],
        compiler_params=_params(("arbitrary",)),
        name="mixer_sample",
    )(z_s, la_s, cos_s, sin_s, dm_s, state_gla[0], state_ret[0], ggla, gret)

    def proj_res(a, res, wgt, g):
        return pl.pallas_call(
            _proj_res_kernel,
            grid=(ts // tm,),
            in_specs=[_row_spec(tm, D_MODEL), _row_spec(tm, D_MODEL), _const_spec((D_MODEL, D_MODEL)),
                      _const_spec((1, D_MODEL))],
            out_specs=_row_spec(tm, D_MODEL),
            out_shape=jax.ShapeDtypeStruct((ts, D_MODEL), F32),
            compiler_params=_params(("arbitrary",)),
            name="proj_res_sample",
        )(a, res, wgt, g)

    x1s = proj_res(merged_s, xs, wmix, gpost_mix)
    q_s = pl.pallas_call(
        _norm_mm_kernel,
        grid=(ts // tm,),
        in_specs=[_row_spec(tm, D_MODEL), _const_spec((1, D_MODEL)), _const_spec((D_MODEL, D_MODEL))],
        out_specs=_row_spec(tm, D_MODEL),
        out_shape=jax.ShapeDtypeStruct((ts, D_MODEL), F32),
        compiler_params=_params(("arbitrary",)),
        name="xq_sample",
    )(x1s, gpre_xa, wxq)

    rows_xa = nb_xa * ls
    kv_spec = pl.BlockSpec((nb_xa, N_MEM, N_HEADS, XA_DH), lambda i: (i, 0, 0, 0))
    o_s = pl.pallas_call(
        functools.partial(_xattn_sample_kernel, nb=nb_xa, ls=ls),
        grid=(bs // nb_xa,),
        in_specs=[_row_spec(rows_xa, D_MODEL), kv_spec, kv_spec],
        out_specs=_row_spec(rows_xa, D_MODEL),
        out_shape=jax.ShapeDtypeStruct((ts, D_MODEL), BF16),
        compiler_params=_params(("arbitrary",)),
        name="xattn_sample",
    )(q_s, cache_mem_k[0], cache_mem_v[0])

    x2s = proj_res(o_s, x1s, wxo, gpost_xa)
    ys = _mlp(x2s, wup, wdown, gpre_ffn, gpost_ffn, tm).reshape(bs, ls, D_MODEL)

    hshape = (1, bp, N_MEM, N_HEADS, XA_DH)
    return (yp, ys, sg_p[None], sr_p[None], mk.reshape(hshape), mv.reshape(hshape), sg_s[None], sr_s[None])
```

```python
import functools
import math

import jax
import jax.numpy as jnp
from jax import lax
from jax.experimental import pallas as pl
from jax.experimental.pallas import tpu as pltpu

F32 = jnp.float32
BF16 = jnp.bfloat16

D_MODEL = 1024
N_HEADS = 4
D_K = 128
D_V = 256
N_MEM = 256
XA_DH = 256
D_FF = 4 * D_MODEL
GATE_RANK = 16
GATE_RANK_PAD = 128
GLA_TAU = 16.0
CHUNK = 64
ROPE_BASE = 10000.0
PAST_LEN = 16384
EPS = 1e-6
LOG_GAMMA = tuple(math.log1p(-(2.0 ** (-5.0 - h))) for h in range(N_HEADS))

C_GQ, C_GK, C_GV, C_GR = 0, 512, 1024, 2048
C_RQ, C_RK, C_RV, C_RG = 3072, 3584, 4096, 5120
C_GA, C_GB = 6144, 7168
N_MAIN = 8192
N_WA = 3072
N_WB = N_MAIN - N_WA
N_QK = N_HEADS * D_K
PROJ_BLOCK = 512

LANE = 128
VMEM_LIMIT_BYTES = 60 * 1024 * 1024


def _mm(a, b):
    return jnp.dot(a, b, preferred_element_type=F32)


def _mm_nt(a, b):
    return lax.dot_general(a, b, (((1,), (1,)), ((), ())), preferred_element_type=F32)


def _mm_tn(a, b):
    return lax.dot_general(a, b, (((0,), (0,)), ((), ())), preferred_element_type=F32)


def _rms(x, g):
    return x * lax.rsqrt(jnp.mean(x * x, axis=-1, keepdims=True) + EPS) * g


def _sigmoid(x):
    return 1.0 / (1.0 + jnp.exp(-x))


def _const_spec(shape):
    nd = len(shape)
    return pl.BlockSpec(shape, lambda *_: (0,) * nd, pipeline_mode=pl.Buffered(1))


def _params(sem):
    return pltpu.CompilerParams(dimension_semantics=sem, vmem_limit_bytes=VMEM_LIMIT_BYTES)


def _inproj_pieces(h_bf, wa_ref, wb_ref, wga_ref, w2_ref, b2_ref, z_ref, la_ref):
    def z_block(j):
        def run():
            w_blk = wa_ref[:, j:j + PROJ_BLOCK] if j < N_WA else wb_ref[:, j - N_WA:j - N_WA + PROJ_BLOCK]
            z_ref[:, j:j + PROJ_BLOCK] = _mm(h_bf, w_blk)
        return run

    def gate():
        ga = _mm(h_bf, wga_ref[...])
        xg = _mm(ga.astype(BF16), w2_ref[...]) + b2_ref[...]
        log_sig = jnp.minimum(xg, 0.0) - jnp.log1p(jnp.exp(-jnp.abs(xg)))
        la_ref[...] = log_sig * (1.0 / GLA_TAU)

    return [z_block(j) for j in range(0, N_MAIN, PROJ_BLOCK)] + [gate]


def _inproj(h_bf, wa_ref, wb_ref, wga_ref, w2_ref, b2_ref, z_ref, la_ref):
    for piece in _inproj_pieces(h_bf, wa_ref, wb_ref, wga_ref, w2_ref, b2_ref, z_ref, la_ref):
        piece()


def _run_next(pieces):
    if pieces:
        pieces.pop(0)()


def _block_causal(n, c):
    shift = c.bit_length() - 1
    row = lax.broadcasted_iota(jnp.int32, (n, n), 0)
    col = lax.broadcasted_iota(jnp.int32, (n, n), 1)
    return ((row >> shift) == (col >> shift)) & (row >= col)


def _gla_tile(z_ref, la_ref, n, c, s_get, s_put, g_ref, o_ref, between=()):
    causal = _block_causal(n, c)
    tri = jnp.where(causal, 1.0, 0.0).astype(BF16)
    la = la_ref[...]
    la_hi = la.astype(BF16)
    la_lo = (la - la_hi.astype(F32)).astype(BF16)
    _run_next(between)
    b = _mm(tri, la_hi) + _mm(tri, la_lo)
    _run_next(between)
    n_chunks = n // c
    b_last = [b[(ci + 1) * c - 1:(ci + 1) * c, :] for ci in range(n_chunks)]
    b_last_full = jnp.concatenate([jnp.broadcast_to(bl, (c, N_QK)) for bl in b_last], axis=0)
    eb = jnp.exp(b)
    enb = jnp.exp(-b)
    ekd = jnp.exp(b_last_full - b)
    for h in range(N_HEADS):
        ks = slice(h * D_K, (h + 1) * D_K)
        q = z_ref[:, C_GQ + h * D_K:C_GQ + (h + 1) * D_K] * (D_K ** -0.5)
        k = z_ref[:, C_GK + h * D_K:C_GK + (h + 1) * D_K]
        v = z_ref[:, C_GV + h * D_V:C_GV + (h + 1) * D_V].astype(BF16)
        qe = (q * eb[:, ks]).astype(BF16)
        ke = (k * enb[:, ks]).astype(BF16)
        kd = (k * ekd[:, ks]).astype(BF16)
        sc = jnp.where(causal, _mm_nt(qe, ke), 0.0).astype(BF16)
        o_intra = _mm(sc, v)
        _run_next(between)
        outs = []
        for ci in range(n_chunks):
            r = slice(ci * c, (ci + 1) * c)
            s_old = s_get(ci, h)
            outs.append(o_intra[r] + _mm(qe[r], s_old.astype(BF16)))
            dcol = jnp.transpose(jnp.broadcast_to(jnp.exp(b_last[ci][:, ks]), (D_K, D_K)))
            s_put(ci, h, jnp.concatenate([dcol, dcol], axis=1) * s_old + _mm_tn(kd[r], v[r]))
        o = jnp.concatenate(outs, axis=0) if n_chunks > 1 else outs[0]
        ms = jnp.mean(o * o, axis=-1, keepdims=True)
        o_ref[:, h * D_V:(h + 1) * D_V] = o * lax.rsqrt(ms + EPS) * g_ref[:, h * D_V:(h + 1) * D_V]
        _run_next(between)


def _ret_tile(z_ref, cos, sin, n, c, s_get, s_put, g_ref, o_ref, dm_ref, between=()):
    tpos = (lax.broadcasted_iota(jnp.int32, (n, 1), 0) & (c - 1)).astype(F32)
    n_chunks = n // c
    for h in range(N_HEADS):
        lg = LOG_GAMMA[h]
        q = z_ref[:, C_RQ + h * D_K:C_RQ + (h + 1) * D_K]
        k = z_ref[:, C_RK + h * D_K:C_RK + (h + 1) * D_K]
        v = z_ref[:, C_RV + h * D_V:C_RV + (h + 1) * D_V].astype(BF16)
        q = q * cos + pltpu.roll(q, D_K // 2, 1) * sin
        k = (k * cos + pltpu.roll(k, D_K // 2, 1) * sin) * (D_K ** -0.5)
        sc = (_mm_nt(q.astype(BF16), k.astype(BF16)) * dm_ref[h]).astype(BF16)
        o_intra = _mm(sc, v)
        qd = (q * jnp.exp(lg * (tpos + 1.0))).astype(BF16)
        kd = (k * jnp.exp(lg * (float(c - 1) - tpos))).astype(BF16)
        outs = []
        for ci in range(n_chunks):
            r = slice(ci * c, (ci + 1) * c)
            s_old = s_get(ci, h)
            outs.append(o_intra[r] + _mm(qd[r], s_old.astype(BF16)))
            s_put(ci, h, math.exp(lg * c) * s_old + _mm_tn(kd[r], v[r]))
        o = jnp.concatenate(outs, axis=0) if n_chunks > 1 else outs[0]
        mu = jnp.mean(o, axis=-1, keepdims=True)
        oc = o - mu
        var = jnp.mean(oc * oc, axis=-1, keepdims=True)
        o_ref[:, h * D_V:(h + 1) * D_V] = oc * lax.rsqrt(var + EPS) * g_ref[:, h * D_V:(h + 1) * D_V]
        _run_next(between)


def _merge(z_ref, og_ref, or_ref, merged_ref, between=()):
    for j in range(0, D_MODEL, 256):
        _run_next(between)
        cs = slice(j, j + 256)
        gr = z_ref[:, C_GR + j:C_GR + j + 256]
        rg = z_ref[:, C_RG + j:C_RG + j + 256]
        ga = z_ref[:, C_GA + j:C_GA + j + 256]
        gb = z_ref[:, C_GB + j:C_GB + j + 256]
        o_g = og_ref[:, cs] * (gr * _sigmoid(gr))
        o_r = or_ref[:, cs] * (rg * _sigmoid(rg))
        merged_ref[:, cs] = (_sigmoid(ga) * o_g + _sigmoid(gb) * o_r).astype(merged_ref.dtype)


def _attend(q, k_ref, v_ref, o_ref, rows):
    for h in range(N_HEADS):
        hs = slice(h * XA_DH, (h + 1) * XA_DH)
        s = _mm_nt(q[:, hs].astype(BF16), k_ref[:, hs].astype(BF16)) * (XA_DH ** -0.5)
        p = jnp.exp(s - jnp.max(s, axis=-1, keepdims=True))
        p = p * (1.0 / jnp.sum(p, axis=-1, keepdims=True))
        o_ref[rows, hs] = _mm(p.astype(BF16), v_ref[:, hs].astype(BF16)).astype(o_ref.dtype)


def _split_w_in_kernel(w_ref, wa_ref, wb_ref, wga_ref):
    w = w_ref[...]
    wa_ref[...] = w[:, :N_WA].astype(BF16)
    wb_ref[...] = w[:, N_WA + GATE_RANK:].astype(BF16)
    zeros = jnp.zeros((w.shape[0], GATE_RANK_PAD - GATE_RANK), F32)
    wga_ref[...] = jnp.concatenate([w[:, N_WA:N_WA + GATE_RANK], zeros], axis=1).astype(BF16)


def _memkv_kernel(m_ref, g_ref, wk_ref, wv_ref, k_ref, v_ref, kb_ref, vb_ref):
    m = _rms(m_ref[...], g_ref[...]).astype(BF16)
    k = _mm(m, wk_ref[...])
    v = _mm(m, wv_ref[...])
    k_ref[...] = k.reshape(k_ref.shape)
    v_ref[...] = v.reshape(v_ref.shape)
    kb_ref[...] = k.astype(BF16)
    vb_ref[...] = v.astype(BF16)


def _mixer_prompt_kernel(xc_ref, xn_ref, cos_ref, sin_ref, dm_ref, wa_ref, wb_ref, wga_ref, w2_ref, b2_ref,
                         ggla_ref, gret_ref, wmix_ref, gpre_ref, gpost_ref,
                         y_ref, sg_ref, sr_ref,
                         z_ref, la_ref, og_ref, or_ref, merged_ref, *, tl, steps_per_row):
    step = pl.program_id(0)

    @pl.when(step % steps_per_row == 0)
    def _():
        sg_ref[...] = jnp.zeros_like(sg_ref)
        sr_ref[...] = jnp.zeros_like(sr_ref)

    def project_pieces(x, slot):
        h_bf = _rms(x, gpre_ref[...]).astype(BF16)
        return _inproj_pieces(h_bf, wa_ref, wb_ref, wga_ref, w2_ref, b2_ref, z_ref.at[slot], la_ref.at[slot])

    @pl.when(step == 0)
    def _():
        for piece in project_pieces(xc_ref[0:tl, :], 0):
            piece()

    n_chunks = tl // CHUNK
    for half in range(2):
        rows = slice(half * tl, (half + 1) * tl)
        pending = project_pieces(xc_ref[tl:2 * tl, :] if half == 0 else xn_ref[...], 1 - half)
        z_cur, la_cur = z_ref.at[half], la_ref.at[half]

        carried = {}

        def gla_get(ci, h, carried=carried):
            return sg_ref[0, h] if ci == 0 else carried[h]

        def gla_put(ci, h, val, carried=carried):
            carried[h] = val
            if ci == n_chunks - 1:
                sg_ref[0, h] = val

        _gla_tile(z_cur, la_cur, tl, CHUNK, gla_get, gla_put, ggla_ref, og_ref, between=pending)

        def ret_put(ci, h, val):
            sr_ref[0, h] = val

        _ret_tile(z_cur, cos_ref[rows, :], sin_ref[rows, :], tl, tl, lambda ci, h: sr_ref[0, h], ret_put,
                  gret_ref, or_ref, dm_ref, between=pending)
        _merge(z_cur, og_ref, or_ref, merged_ref, between=pending)
        while pending:
            _run_next(pending)
        m = _mm(merged_ref[...], wmix_ref[...])
        y_ref[rows, :] = xc_ref[rows, :] + _rms(m, gpost_ref[...])


def _xattn_prompt_kernel(x_ref, mk_ref, mv_ref, wq_ref, wo_ref, gpre_ref, gpost_ref, y_ref, o_ref):
    x = x_ref[0]
    q = _mm(_rms(x, gpre_ref[...]).astype(BF16), wq_ref[...])
    _attend(q, mk_ref.at[0], mv_ref.at[0], o_ref, slice(None))
    a = _mm(o_ref[...], wo_ref[...])
    y_ref[0] = x + _rms(a, gpost_ref[...])


def _mlp_kernel(x_ref, wup_ref, wdown_ref, gpre_ref, gpost_ref, y_ref):
    x = x_ref[...]
    h_bf = _rms(x, gpre_ref[...]).astype(BF16)
    acc = None
    for j in range(0, D_FF, 1024):
        u = jnp.maximum(_mm(h_bf, wup_ref[:, j:j + 1024]), 0.0)
        part = _mm((u * u).astype(BF16), wdown_ref[j:j + 1024, :])
        acc = part if acc is None else acc + part
    y_ref[...] = x + _rms(acc, gpost_ref[...])


def _inproj_kernel(x_ref, wa_ref, wb_ref, wga_ref, w2_ref, b2_ref, gpre_ref, z_ref, la_ref):
    h_bf = _rms(x_ref[...], gpre_ref[...]).astype(BF16)
    _inproj(h_bf, wa_ref, wb_ref, wga_ref, w2_ref, b2_ref, z_ref, la_ref)


def _mixer_sample_kernel(z_ref, la_ref, cos_ref, sin_ref, dm_ref, sgi_ref, sri_ref, ggla_ref, gret_ref,
                         merged_ref, sgo_ref, sro_ref, og_ref, or_ref, *, nb, ls):
    def gla_put(ci, h, val):
        sgo_ref[ci, h] = val

    def ret_put(ci, h, val):
        sro_ref[ci, h] = val

    _gla_tile(z_ref, la_ref, nb * ls, ls, lambda ci, h: sgi_ref[ci, h], gla_put, ggla_ref, og_ref)
    _ret_tile(z_ref, cos_ref[...], sin_ref[...], nb * ls, ls, lambda ci, h: sri_ref[ci, h], ret_put,
              gret_ref, or_ref, dm_ref)
    _merge(z_ref, og_ref, or_ref, merged_ref)


def _proj_res_kernel(a_ref, res_ref, w_ref, g_ref, y_ref):
    y_ref[...] = res_ref[...] + _rms(_mm(a_ref[...], w_ref[...]), g_ref[...])


def _norm_mm_kernel(x_ref, g_ref, w_ref, y_ref):
    y_ref[...] = _mm(_rms(x_ref[...], g_ref[...]).astype(BF16), w_ref[...])


def _xattn_sample_kernel(q_ref, k_ref, v_ref, o_ref, *, nb, ls):
    n_rows = N_HEADS * ls
    n_cols = N_MEM * N_HEADS
    row_head = lax.broadcasted_iota(jnp.int32, (n_rows, n_cols), 0) // ls
    col_head = lax.broadcasted_iota(jnp.int32, (n_rows, n_cols), 1) & (N_HEADS - 1)
    own = row_head == col_head

    def seq_body(e, carry):
        rows = pl.ds(pl.multiple_of(e * ls, ls), ls)
        q = q_ref[rows, :]
        q_hm = jnp.concatenate([q[:, h * XA_DH:(h + 1) * XA_DH] for h in range(N_HEADS)], axis=0)
        k_all = k_ref[e].reshape(n_cols, XA_DH)
        v_all = v_ref[e].reshape(n_cols, XA_DH)
        s = _mm_nt(q_hm.astype(BF16), k_all.astype(BF16)) * (XA_DH ** -0.5)
        s = jnp.where(own, s, -jnp.inf)
        p = jnp.exp(s - jnp.max(s, axis=-1, keepdims=True))
        p = p * (1.0 / jnp.sum(p, axis=-1, keepdims=True))
        o = _mm(p.astype(BF16), v_all.astype(BF16))
        for h in range(N_HEADS):
            o_ref[rows, h * XA_DH:(h + 1) * XA_DH] = o[h * ls:(h + 1) * ls, :].astype(o_ref.dtype)
        return carry

    lax.fori_loop(0, nb, seq_body, 0, unroll=2)


def _rope_tables(pos):
    half = D_K // 2
    inv = ROPE_BASE ** (-jnp.arange(half, dtype=F32) / half)
    ang = pos.astype(F32)[:, None] * inv[None, :]
    cos, sin = jnp.cos(ang), jnp.sin(ang)
    return jnp.concatenate([cos, cos], axis=-1), jnp.concatenate([-sin, sin], axis=-1)


def _decay_masks(n, c):
    t = jnp.arange(n, dtype=jnp.int32)
    keep = ((t[:, None] // c) == (t[None, :] // c)) & (t[:, None] >= t[None, :])
    dist = (t[:, None] - t[None, :]).astype(F32)
    lg = jnp.asarray(LOG_GAMMA, F32)[:, None, None]
    return jnp.where(keep[None], jnp.exp(lg * dist[None]), 0.0)


def _row_spec(tm, n):
    return pl.BlockSpec((tm, n), lambda i: (i, 0))


def _mlp(x, wup, wdown, gpre, gpost, tm):
    t = x.shape[0]
    return pl.pallas_call(
        _mlp_kernel,
        grid=(t // tm,),
        in_specs=[_row_spec(tm, D_MODEL), _const_spec((D_MODEL, D_FF)), _const_spec((D_FF, D_MODEL)),
                  _const_spec((1, D_MODEL)), _const_spec((1, D_MODEL))],
        out_specs=_row_spec(tm, D_MODEL),
        out_shape=jax.ShapeDtypeStruct((t, D_MODEL), F32),
        compiler_params=_params(("arbitrary",)),
        name="mlp",
    )(x, wup, wdown, gpre, gpost)


def kernel(x_prompt, x_sample, state_gla, state_ret, cache_mem_k, cache_mem_v, mem_prompt, w_in, w_gla_a2, b_gla_a, g_gla_head, g_ret_head, w_mix_out, w_xq, w_xk, w_xv, w_xo, g_mem, w_up, w_down, g_pre_mix, g_post_mix, g_pre_xa, g_post_xa, g_pre_ffn, g_post_ffn):
    depth = w_in.shape[0]
    assert depth == 1
    bp, lp, _ = x_prompt.shape
    bs, ls, _ = x_sample.shape
    tl = 256
    tm = 512
    nb_mix = 8
    nb_xa = 8
    assert lp % (2 * tl) == 0 and tl % CHUNK == 0 and lp % tm == 0
    assert (bs * ls) % tm == 0 and bs % nb_mix == 0 and bs % nb_xa == 0
    assert ls % 8 == 0 and ls <= CHUNK and ls & (ls - 1) == 0 and tl & (tl - 1) == 0
    assert (bp * N_MEM) % tm == 0

    w = w_in[0]
    c_low = 2 * N_QK + 2 * D_MODEL
    assert c_low == N_WA and w.shape[1] == N_MAIN + GATE_RANK
    w_rows = 128
    wa, wb, wga = pl.pallas_call(
        _split_w_in_kernel,
        grid=(D_MODEL // w_rows,),
        in_specs=[_row_spec(w_rows, N_MAIN + GATE_RANK)],
        out_specs=[_row_spec(w_rows, N_WA), _row_spec(w_rows, N_WB), _row_spec(w_rows, GATE_RANK_PAD)],
        out_shape=[jax.ShapeDtypeStruct((D_MODEL, N_WA), BF16), jax.ShapeDtypeStruct((D_MODEL, N_WB), BF16),
                   jax.ShapeDtypeStruct((D_MODEL, GATE_RANK_PAD), BF16)],
        compiler_params=_params(("arbitrary",)),
        name="split_w_in",
    )(w)
    w2 = jnp.pad(w_gla_a2[0], ((0, GATE_RANK_PAD - GATE_RANK), (0, 0))).astype(BF16)
    b2 = b_gla_a[0].reshape(1, N_QK)
    ggla = g_gla_head[0].reshape(1, D_MODEL)
    gret = g_ret_head[0].reshape(1, D_MODEL)
    wmix = w_mix_out[0].astype(BF16)
    wxq, wxk, wxv, wxo = (t[0].astype(BF16) for t in (w_xq, w_xk, w_xv, w_xo))
    wup, wdown = w_up[0].astype(BF16), w_down[0].astype(BF16)
    row = lambda g: g[0].reshape(1, D_MODEL)
    gmem, gpre_mix, gpost_mix, gpre_xa, gpost_xa, gpre_ffn, gpost_ffn = (
        row(g) for g in (g_mem, g_pre_mix, g_post_mix, g_pre_xa, g_post_xa, g_pre_ffn, g_post_ffn))
    cos_p, sin_p = _rope_tables(jnp.arange(lp, dtype=jnp.int32))
    cos_s, sin_s = _rope_tables(PAST_LEN + jnp.arange(ls, dtype=jnp.int32))
    rows_mix = nb_mix * ls
    cos_s, sin_s = jnp.tile(cos_s, (nb_mix, 1)), jnp.tile(sin_s, (nb_mix, 1))
    dm_p = _decay_masks(tl, tl)
    dm_s = _decay_masks(rows_mix, ls)

    nmem_rows = bp * N_MEM
    mk, mv, mk_bf, mv_bf = pl.pallas_call(
        _memkv_kernel,
        grid=(nmem_rows // tm,),
        in_specs=[_row_spec(tm, D_MODEL), _const_spec((1, D_MODEL)),
                  _const_spec((D_MODEL, D_MODEL)), _const_spec((D_MODEL, D_MODEL))],
        out_specs=[pl.BlockSpec((tm, N_HEADS, XA_DH), lambda i: (i, 0, 0))] * 2 + [_row_spec(tm, D_MODEL)] * 2,
        out_shape=[jax.ShapeDtypeStruct((nmem_rows, N_HEADS, XA_DH), F32)] * 2
        + [jax.ShapeDtypeStruct((nmem_rows, D_MODEL), BF16)] * 2,
        compiler_params=_params(("arbitrary",)),
        name="memkv",
    )(mem_prompt.reshape(nmem_rows, D_MODEL), gmem, wxk, wxv)

    n_tiles = bp * lp // tl
    steps_per_row = lp // (2 * tl)
    state_spec = pl.BlockSpec((1, N_HEADS, D_K, D_V), lambda s: (s // steps_per_row, 0, 0, 0))
    rope_spec = pl.BlockSpec((2 * tl, D_K), lambda s: (s % steps_per_row, 0))
    x1p, sg_p, sr_p = pl.pallas_call(
        functools.partial(_mixer_prompt_kernel, tl=tl, steps_per_row=steps_per_row),
        grid=(n_tiles // 2,),
        in_specs=[_row_spec(2 * tl, D_MODEL),
                  pl.BlockSpec((tl, D_MODEL), lambda s: (jnp.minimum(2 * s + 2, n_tiles - 1), 0)),
                  rope_spec, rope_spec,
                  _const_spec((N_HEADS, tl, tl)),
                  _const_spec((D_MODEL, N_WA)), _const_spec((D_MODEL, N_WB)), _const_spec((D_MODEL, GATE_RANK_PAD)),
                  _const_spec((GATE_RANK_PAD, N_QK)), _const_spec((1, N_QK)),
                  _const_spec((1, D_MODEL)), _const_spec((1, D_MODEL)),
                  _const_spec((D_MODEL, D_MODEL)), _const_spec((1, D_MODEL)), _const_spec((1, D_MODEL))],
        out_specs=[_row_spec(2 * tl, D_MODEL), state_spec, state_spec],
        out_shape=[jax.ShapeDtypeStruct((bp * lp, D_MODEL), F32),
                   jax.ShapeDtypeStruct((bp, N_HEADS, D_K, D_V), F32),
                   jax.ShapeDtypeStruct((bp, N_HEADS, D_K, D_V), F32)],
        scratch_shapes=[pltpu.VMEM((2, tl, N_MAIN), F32), pltpu.VMEM((2, tl, N_QK), F32),
                        pltpu.VMEM((tl, D_MODEL), F32), pltpu.VMEM((tl, D_MODEL), F32),
                        pltpu.VMEM((tl, D_MODEL), BF16)],
        compiler_params=_params(("arbitrary",)),
        name="mixer_prompt",
    )(x_prompt.reshape(bp * lp, D_MODEL), x_prompt.reshape(bp * lp, D_MODEL), cos_p, sin_p, dm_p,
      wa, wb, wga, w2, b2, ggla, gret, wmix, gpre_mix, gpost_mix)
    x1p = x1p.reshape(bp, lp, D_MODEL)

    mem_spec = pl.BlockSpec((1, N_MEM, D_MODEL), lambda b, t: (b, 0, 0))
    x2p = pl.pallas_call(
        _xattn_prompt_kernel,
        grid=(bp, lp // tm),
        in_specs=[pl.BlockSpec((1, tm, D_MODEL), lambda b, t: (b, t, 0)), mem_spec, mem_spec,
                  _const_spec((D_MODEL, D_MODEL)), _const_spec((D_MODEL, D_MODEL)),
                  _const_spec((1, D_MODEL)), _const_spec((1, D_MODEL))],
        out_specs=pl.BlockSpec((1, tm, D_MODEL), lambda b, t: (b, t, 0)),
        out_shape=jax.ShapeDtypeStruct((bp, lp, D_MODEL), F32),
        scratch_shapes=[pltpu.VMEM((tm, D_MODEL), BF16)],
        compiler_params=_params(("arbitrary", "arbitrary")),
        name="xattn_prompt",
    )(x1p, mk_bf.reshape(bp, N_MEM, D_MODEL), mv_bf.reshape(bp, N_MEM, D_MODEL), wxq, wxo, gpre_xa, gpost_xa)

    yp = _mlp(x2p.reshape(bp * lp, D_MODEL), wup, wdown, gpre_ffn, gpost_ffn, tm).reshape(bp, lp, D_MODEL)

    ts = bs * ls
    xs = x_sample.reshape(ts, D_MODEL)
    z_s, la_s = pl.pallas_call(
        _inproj_kernel,
        grid=(ts // tm,),
        in_specs=[_row_spec(tm, D_MODEL), _const_spec((D_MODEL, N_WA)), _const_spec((D_MODEL, N_WB)),
                  _const_spec((D_MODEL, GATE_RANK_PAD)),
                  _const_spec((GATE_RANK_PAD, N_QK)), _const_spec((1, N_QK)), _const_spec((1, D_MODEL))],
        out_specs=[_row_spec(tm, N_MAIN), _row_spec(tm, N_QK)],
        out_shape=[jax.ShapeDtypeStruct((ts, N_MAIN), F32), jax.ShapeDtypeStruct((ts, N_QK), F32)],
        compiler_params=_params(("arbitrary",)),
        name="inproj_sample",
    )(xs, wa, wb, wga, w2, b2, gpre_mix)

    st_spec = pl.BlockSpec((nb_mix, N_HEADS, D_K, D_V), lambda i: (i, 0, 0, 0))
    merged_s, sg_s, sr_s = pl.pallas_call(
        functools.partial(_mixer_sample_kernel, nb=nb_mix, ls=ls),
        grid=(bs // nb_mix,),
        in_specs=[_row_spec(rows_mix, N_MAIN), _row_spec(rows_mix, N_QK),
                  _const_spec((rows_mix, D_K)), _const_spec((rows_mix, D_K)),
                  _const_spec((N_HEADS, rows_mix, rows_mix)), st_spec, st_spec,
                  _const_spec((1, D_MODEL)), _const_spec((1, D_MODEL))],
        out_specs=[_row_spec(rows_mix, D_MODEL), st_spec, st_spec],
        out_shape=[jax.ShapeDtypeStruct((ts, D_MODEL), BF16),
                   jax.ShapeDtypeStruct((bs, N_HEADS, D_K, D_V), F32),
                   jax.ShapeDtypeStruct((bs, N_HEADS, D_K, D_V), F32)],
        scratch_shapes=[pltpu.VMEM((rows_mix, D_MODEL), F32), pltpu.VMEM((rows_mix, D_MODEL), F32)],
        compiler_params=_params(("arbitrary",)),
        name="mixer_sample",
    )(z_s, la_s, cos_s, sin_s, dm_s, state_gla[0], state_ret[0], ggla, gret)

    def proj_res(a, res, wgt, g):
        return pl.pallas_call(
            _proj_res_kernel,
            grid=(ts // tm,),
            in_specs=[_row_spec(tm, D_MODEL), _row_spec(tm, D_MODEL), _const_spec((D_MODEL, D_MODEL)),
                      _const_spec((1, D_MODEL))],
            out_specs=_row_spec(tm, D_MODEL),
            out_shape=jax.ShapeDtypeStruct((ts, D_MODEL), F32),
            compiler_params=_params(("arbitrary",)),
            name="proj_res_sample",
        )(a, res, wgt, g)

    x1s = proj_res(merged_s, xs, wmix, gpost_mix)
    q_s = pl.pallas_call(
        _norm_mm_kernel,
        grid=(ts // tm,),
        in_specs=[_row_spec(tm, D_MODEL), _const_spec((1, D_MODEL)), _const_spec((D_MODEL, D_MODEL))],
        out_specs=_row_spec(tm, D_MODEL),
        out_shape=jax.ShapeDtypeStruct((ts, D_MODEL), F32),
        compiler_params=_params(("arbitrary",)),
        name="xq_sample",
    )(x1s, gpre_xa, wxq)

    rows_xa = nb_xa * ls
    kv_spec = pl.BlockSpec((nb_xa, N_MEM, N_HEADS, XA_DH), lambda i: (i, 0, 0, 0))
    o_s = pl.pallas_call(
        functools.partial(_xattn_sample_kernel, nb=nb_xa, ls=ls),
        grid=(bs // nb_xa,),
        in_specs=[_row_spec(rows_xa, D_MODEL), kv_spec, kv_spec],
        out_specs=_row_spec(rows_xa, D_MODEL),
        out_shape=jax.ShapeDtypeStruct((ts, D_MODEL), BF16),
        compiler_params=_params(("arbitrary",)),
        name="xattn_sample",
    )(q_s, cache_mem_k[0], cache_mem_v[0])

    x2s = proj_res(o_s, x1s, wxo, gpost_xa)
    ys = _mlp(x2s, wup, wdown, gpre_ffn, gpost_ffn, tm).reshape(bs, ls, D_MODEL)

    hshape = (1, bp, N_MEM, N_HEADS, XA_DH)
    return (yp, ys, sg_p[None], sr_p[None], mk.reshape(hshape), mv.reshape(hshape), sg_s[None], sr_s[None])
```

```python
import functools
import math

import jax
import jax.numpy as jnp
from jax import lax
from jax.experimental import pallas as pl
from jax.experimental.pallas import tpu as pltpu

F32 = jnp.float32
BF16 = jnp.bfloat16

D_MODEL = 1024
N_HEADS = 4
D_K = 128
D_V = 256
N_MEM = 256
XA_DH = 256
D_FF = 4 * D_MODEL
GATE_RANK = 16
GATE_RANK_PAD = 128
GLA_TAU = 16.0
CHUNK = 64
ROPE_BASE = 10000.0
PAST_LEN = 16384
EPS = 1e-6
LOG_GAMMA = tuple(math.log1p(-(2.0 ** (-5.0 - h))) for h in range(N_HEADS))

C_GQ, C_GK, C_GV, C_GR = 0, 512, 1024, 2048
C_RQ, C_RK, C_RV, C_RG = 3072, 3584, 4096, 5120
C_GA, C_GB = 6144, 7168
N_MAIN = 8192
N_WA = 3072
N_WB = N_MAIN - N_WA
N_QK = N_HEADS * D_K
PROJ_BLOCK = 512

LANE = 128
VMEM_LIMIT_BYTES = 60 * 1024 * 1024


def _mm(a, b):
    return jnp.dot(a, b, preferred_element_type=F32)


def _mm_nt(a, b):
    return lax.dot_general(a, b, (((1,), (1,)), ((), ())), preferred_element_type=F32)


def _mm_tn(a, b):
    return lax.dot_general(a, b, (((0,), (0,)), ((), ())), preferred_element_type=F32)


def _rms(x, g):
    return x * lax.rsqrt(jnp.mean(x * x, axis=-1, keepdims=True) + EPS) * g


def _sigmoid(x):
    return 1.0 / (1.0 + jnp.exp(-x))


def _const_spec(shape):
    nd = len(shape)
    return pl.BlockSpec(shape, lambda *_: (0,) * nd, pipeline_mode=pl.Buffered(1))


def _params(sem):
    return pltpu.CompilerParams(dimension_semantics=sem, vmem_limit_bytes=VMEM_LIMIT_BYTES)


def _inproj_pieces(h_bf, wmain_ref, wga_ref, w2_ref, b2_ref, z_ref, la_ref):
    def z_block(j):
        def run():
            z_ref[:, j:j + PROJ_BLOCK] = _mm(h_bf, wmain_ref[:, j:j + PROJ_BLOCK])
        return run

    def gate():
        ga = _mm(h_bf, wga_ref[...])
        xg = _mm(ga.astype(BF16), w2_ref[...]) + b2_ref[...]
        log_sig = jnp.minimum(xg, 0.0) - jnp.log1p(jnp.exp(-jnp.abs(xg)))
        la_ref[...] = log_sig * (1.0 / GLA_TAU)

    return [z_block(j) for j in range(0, N_MAIN, PROJ_BLOCK)] + [gate]


def _inproj(h_bf, wmain_ref, wga_ref, w2_ref, b2_ref, z_ref, la_ref):
    for piece in _inproj_pieces(h_bf, wmain_ref, wga_ref, w2_ref, b2_ref, z_ref, la_ref):
        piece()


def _run_next(pieces):
    if pieces:
        pieces.pop(0)()


def _block_causal(n, c):
    shift = c.bit_length() - 1
    row = lax.broadcasted_iota(jnp.int32, (n, n), 0)
    col = lax.broadcasted_iota(jnp.int32, (n, n), 1)
    return ((row >> shift) == (col >> shift)) & (row >= col)


def _gla_tile(z_ref, la_ref, n, c, s_get, s_put, g_ref, o_ref, between=()):
    causal = _block_causal(n, c)
    tri = jnp.where(causal, 1.0, 0.0).astype(BF16)
    la = la_ref[...]
    la_hi = la.astype(BF16)
    la_lo = (la - la_hi.astype(F32)).astype(BF16)
    b = _mm(tri, la_hi) + _mm(tri, la_lo)
    n_chunks = n // c
    b_last = [b[(ci + 1) * c - 1:(ci + 1) * c, :] for ci in range(n_chunks)]
    b_last_full = jnp.concatenate([jnp.broadcast_to(bl, (c, N_QK)) for bl in b_last], axis=0)
    eb = jnp.exp(b)
    enb = jnp.exp(-b)
    ekd = jnp.exp(b_last_full - b)
    for h in range(N_HEADS):
        ks = slice(h * D_K, (h + 1) * D_K)
        q = z_ref[:, C_GQ + h * D_K:C_GQ + (h + 1) * D_K] * (D_K ** -0.5)
        k = z_ref[:, C_GK + h * D_K:C_GK + (h + 1) * D_K]
        v = z_ref[:, C_GV + h * D_V:C_GV + (h + 1) * D_V].astype(BF16)
        qe = (q * eb[:, ks]).astype(BF16)
        ke = (k * enb[:, ks]).astype(BF16)
        kd = (k * ekd[:, ks]).astype(BF16)
        sc = jnp.where(causal, _mm_nt(qe, ke), 0.0).astype(BF16)
        o_intra = _mm(sc, v)
        _run_next(between)
        outs = []
        for ci in range(n_chunks):
            r = slice(ci * c, (ci + 1) * c)
            s_old = s_get(ci, h)
            outs.append(o_intra[r] + _mm(qe[r], s_old.astype(BF16)))
            dcol = jnp.transpose(jnp.broadcast_to(jnp.exp(b_last[ci][:, ks]), (D_K, D_K)))
            s_put(ci, h, jnp.concatenate([dcol, dcol], axis=1) * s_old + _mm_tn(kd[r], v[r]))
        o = jnp.concatenate(outs, axis=0) if n_chunks > 1 else outs[0]
        ms = jnp.mean(o * o, axis=-1, keepdims=True)
        o_ref[:, h * D_V:(h + 1) * D_V] = o * lax.rsqrt(ms + EPS) * g_ref[:, h * D_V:(h + 1) * D_V]
        _run_next(between)


def _ret_tile(z_ref, cos, sin, n, c, s_get, s_put, g_ref, o_ref, dm_ref, between=()):
    tpos = (lax.broadcasted_iota(jnp.int32, (n, 1), 0) & (c - 1)).astype(F32)
    n_chunks = n // c
    for h in range(N_HEADS):
        lg = LOG_GAMMA[h]
        q = z_ref[:, C_RQ + h * D_K:C_RQ + (h + 1) * D_K]
        k = z_ref[:, C_RK + h * D_K:C_RK + (h + 1) * D_K]
        v = z_ref[:, C_RV + h * D_V:C_RV + (h + 1) * D_V].astype(BF16)
        q = q * cos + pltpu.roll(q, D_K // 2, 1) * sin
        k = (k * cos + pltpu.roll(k, D_K // 2, 1) * sin) * (D_K ** -0.5)
        sc = (_mm_nt(q.astype(BF16), k.astype(BF16)) * dm_ref[h]).astype(BF16)
        o_intra = _mm(sc, v)
        _run_next(between)
        qd = (q * jnp.exp(lg * (tpos + 1.0))).astype(BF16)
        kd = (k * jnp.exp(lg * (float(c - 1) - tpos))).astype(BF16)
        outs = []
        for ci in range(n_chunks):
            r = slice(ci * c, (ci + 1) * c)
            s_old = s_get(ci, h)
            outs.append(o_intra[r] + _mm(qd[r], s_old.astype(BF16)))
            s_put(ci, h, math.exp(lg * c) * s_old + _mm_tn(kd[r], v[r]))
        o = jnp.concatenate(outs, axis=0) if n_chunks > 1 else outs[0]
        mu = jnp.mean(o, axis=-1, keepdims=True)
        oc = o - mu
        var = jnp.mean(oc * oc, axis=-1, keepdims=True)
        o_ref[:, h * D_V:(h + 1) * D_V] = oc * lax.rsqrt(var + EPS) * g_ref[:, h * D_V:(h + 1) * D_V]
        _run_next(between)


def _merge(z_ref, og_ref, or_ref, merged_ref):
    for j in range(0, D_MODEL, 256):
        cs = slice(j, j + 256)
        gr = z_ref[:, C_GR + j:C_GR + j + 256]
        rg = z_ref[:, C_RG + j:C_RG + j + 256]
        ga = z_ref[:, C_GA + j:C_GA + j + 256]
        gb = z_ref[:, C_GB + j:C_GB + j + 256]
        o_g = og_ref[:, cs] * (gr * _sigmoid(gr))
        o_r = or_ref[:, cs] * (rg * _sigmoid(rg))
        merged_ref[:, cs] = (_sigmoid(ga) * o_g + _sigmoid(gb) * o_r).astype(merged_ref.dtype)


def _attend(q, k_ref, v_ref, o_ref, rows):
    for h in range(N_HEADS):
        hs = slice(h * XA_DH, (h + 1) * XA_DH)
        s = _mm_nt(q[:, hs].astype(BF16), k_ref[:, hs].astype(BF16)) * (XA_DH ** -0.5)
        p = jnp.exp(s - jnp.max(s, axis=-1, keepdims=True))
        p = p * (1.0 / jnp.sum(p, axis=-1, keepdims=True))
        o_ref[rows, hs] = _mm(p.astype(BF16), v_ref[:, hs].astype(BF16)).astype(o_ref.dtype)


def _split_w_in_kernel(wt_ref, wtg_ref, w_ref, wga_ref):
    w_ref[...] = jnp.transpose(wt_ref[...]).astype(BF16)
    zeros = jnp.zeros((GATE_RANK_PAD - GATE_RANK, D_MODEL), F32)
    wga_ref[...] = jnp.transpose(jnp.concatenate([wtg_ref[...], zeros], axis=0)).astype(BF16)


def _memkv_kernel(m_ref, g_ref, wk_ref, wv_ref, k_ref, v_ref, kb_ref, vb_ref):
    m = _rms(m_ref[...], g_ref[...]).astype(BF16)
    k = _mm(m, wk_ref[...])
    v = _mm(m, wv_ref[...])
    k_ref[...] = k.reshape(k_ref.shape)
    v_ref[...] = v.reshape(v_ref.shape)
    kb_ref[...] = k.astype(BF16)
    vb_ref[...] = v.astype(BF16)


def _mixer_prompt_kernel(xc_ref, xn_ref, cos_ref, sin_ref, dm_ref, wmain_ref, wga_ref, w2_ref, b2_ref,
                         ggla_ref, gret_ref, wmix_ref, gpre_ref, gpost_ref,
                         y_ref, sg_ref, sr_ref,
                         z_ref, la_ref, og_ref, or_ref, merged_ref, *, tl, steps_per_row):
    step = pl.program_id(0)

    @pl.when(step % steps_per_row == 0)
    def _():
        sg_ref[...] = jnp.zeros_like(sg_ref)
        sr_ref[...] = jnp.zeros_like(sr_ref)

    def project_pieces(x, slot):
        h_bf = _rms(x, gpre_ref[...]).astype(BF16)
        return _inproj_pieces(h_bf, wmain_ref, wga_ref, w2_ref, b2_ref, z_ref.at[slot], la_ref.at[slot])

    @pl.when(step == 0)
    def _():
        for piece in project_pieces(xc_ref[0:tl, :], 0):
            piece()

    n_chunks = tl // CHUNK
    for half in range(2):
        rows = slice(half * tl, (half + 1) * tl)
        pending = project_pieces(xc_ref[tl:2 * tl, :] if half == 0 else xn_ref[...], 1 - half)
        _run_next(pending)
        z_cur, la_cur = z_ref.at[half], la_ref.at[half]

        carried = {}

        def gla_get(ci, h, carried=carried):
            return sg_ref[0, h] if ci == 0 else carried[h]

        def gla_put(ci, h, val, carried=carried):
            carried[h] = val
            if ci == n_chunks - 1:
                sg_ref[0, h] = val

        _gla_tile(z_cur, la_cur, tl, CHUNK, gla_get, gla_put, ggla_ref, og_ref, between=pending)

        def ret_put(ci, h, val):
            sr_ref[0, h] = val

        _ret_tile(z_cur, cos_ref[rows, :], sin_ref[rows, :], tl, tl, lambda ci, h: sr_ref[0, h], ret_put,
                  gret_ref, or_ref, dm_ref, between=pending)
        while pending:
            _run_next(pending)

        _merge(z_cur, og_ref, or_ref, merged_ref)
        m = _mm(merged_ref[...], wmix_ref[...])
        y_ref[rows, :] = xc_ref[rows, :] + _rms(m, gpost_ref[...])


def _xattn_prompt_kernel(x_ref, mk_ref, mv_ref, wq_ref, wo_ref, gpre_ref, gpost_ref, y_ref, o_ref):
    x = x_ref[0]
    q = _mm(_rms(x, gpre_ref[...]).astype(BF16), wq_ref[...])
    _attend(q, mk_ref.at[0], mv_ref.at[0], o_ref, slice(None))
    a = _mm(o_ref[...], wo_ref[...])
    y_ref[0] = x + _rms(a, gpost_ref[...])


def _mlp_kernel(x_ref, wup_ref, wdown_ref, gpre_ref, gpost_ref, y_ref):
    x = x_ref[...]
    h_bf = _rms(x, gpre_ref[...]).astype(BF16)
    acc = None
    for j in range(0, D_FF, 1024):
        u = jnp.maximum(_mm(h_bf, wup_ref[:, j:j + 1024]), 0.0)
        part = _mm((u * u).astype(BF16), wdown_ref[j:j + 1024, :])
        acc = part if acc is None else acc + part
    y_ref[...] = x + _rms(acc, gpost_ref[...])


def _inproj_kernel(x_ref, wmain_ref, wga_ref, w2_ref, b2_ref, gpre_ref, z_ref, la_ref):
    h_bf = _rms(x_ref[...], gpre_ref[...]).astype(BF16)
    _inproj(h_bf, wmain_ref, wga_ref, w2_ref, b2_ref, z_ref, la_ref)


def _mixer_sample_kernel(z_ref, la_ref, cos_ref, sin_ref, dm_ref, sgi_ref, sri_ref, ggla_ref, gret_ref,
                         merged_ref, sgo_ref, sro_ref, og_ref, or_ref, *, nb, ls):
    def gla_put(ci, h, val):
        sgo_ref[ci, h] = val

    def ret_put(ci, h, val):
        sro_ref[ci, h] = val

    _gla_tile(z_ref, la_ref, nb * ls, ls, lambda ci, h: sgi_ref[ci, h], gla_put, ggla_ref, og_ref)
    _ret_tile(z_ref, cos_ref[...], sin_ref[...], nb * ls, ls, lambda ci, h: sri_ref[ci, h], ret_put,
              gret_ref, or_ref, dm_ref)
    _merge(z_ref, og_ref, or_ref, merged_ref)


def _proj_res_kernel(a_ref, res_ref, w_ref, g_ref, y_ref):
    y_ref[...] = res_ref[...] + _rms(_mm(a_ref[...], w_ref[...]), g_ref[...])


def _norm_mm_kernel(x_ref, g_ref, w_ref, y_ref):
    y_ref[...] = _mm(_rms(x_ref[...], g_ref[...]).astype(BF16), w_ref[...])


def _xattn_sample_kernel(q_ref, k_ref, v_ref, o_ref, *, nb, ls):
    n_rows = N_HEADS * ls
    n_cols = N_MEM * N_HEADS
    row_head = lax.broadcasted_iota(jnp.int32, (n_rows, n_cols), 0) // ls
    col_head = lax.broadcasted_iota(jnp.int32, (n_rows, n_cols), 1) & (N_HEADS - 1)
    own = row_head == col_head

    def seq_body(e, carry):
        rows = pl.ds(pl.multiple_of(e * ls, ls), ls)
        q = q_ref[rows, :]
        q_hm = jnp.concatenate([q[:, h * XA_DH:(h + 1) * XA_DH] for h in range(N_HEADS)], axis=0)
        k_all = k_ref[e].reshape(n_cols, XA_DH)
        v_all = v_ref[e].reshape(n_cols, XA_DH)
        s = _mm_nt(q_hm.astype(BF16), k_all.astype(BF16)) * (XA_DH ** -0.5)
        s = jnp.where(own, s, -jnp.inf)
        p = jnp.exp(s - jnp.max(s, axis=-1, keepdims=True))
        p = p * (1.0 / jnp.sum(p, axis=-1, keepdims=True))
        o = _mm(p.astype(BF16), v_all.astype(BF16))
        for h in range(N_HEADS):
            o_ref[rows, h * XA_DH:(h + 1) * XA_DH] = o[h * ls:(h + 1) * ls, :].astype(o_ref.dtype)
        return carry

    lax.fori_loop(0, nb, seq_body, 0, unroll=2)


def _rope_tables(pos):
    half = D_K // 2
    inv = ROPE_BASE ** (-jnp.arange(half, dtype=F32) / half)
    ang = pos.astype(F32)[:, None] * inv[None, :]
    cos, sin = jnp.cos(ang), jnp.sin(ang)
    return jnp.concatenate([cos, cos], axis=-1), jnp.concatenate([-sin, sin], axis=-1)


def _decay_masks(n, c):
    t = jnp.arange(n, dtype=jnp.int32)
    keep = ((t[:, None] // c) == (t[None, :] // c)) & (t[:, None] >= t[None, :])
    dist = (t[:, None] - t[None, :]).astype(F32)
    lg = jnp.asarray(LOG_GAMMA, F32)[:, None, None]
    return jnp.where(keep[None], jnp.exp(lg * dist[None]), 0.0)


def _row_spec(tm, n):
    return pl.BlockSpec((tm, n), lambda i: (i, 0))


def _mlp(x, wup, wdown, gpre, gpost, tm):
    t = x.shape[0]
    return pl.pallas_call(
        _mlp_kernel,
        grid=(t // tm,),
        in_specs=[_row_spec(tm, D_MODEL), _const_spec((D_MODEL, D_FF)), _const_spec((D_FF, D_MODEL)),
                  _const_spec((1, D_MODEL)), _const_spec((1, D_MODEL))],
        out_specs=_row_spec(tm, D_MODEL),
        out_shape=jax.ShapeDtypeStruct((t, D_MODEL), F32),
        compiler_params=_params(("arbitrary",)),
        name="mlp",
    )(x, wup, wdown, gpre, gpost)


def kernel(x_prompt, x_sample, state_gla, state_ret, cache_mem_k, cache_mem_v, mem_prompt, w_in, w_gla_a2, b_gla_a, g_gla_head, g_ret_head, w_mix_out, w_xq, w_xk, w_xv, w_xo, g_mem, w_up, w_down, g_pre_mix, g_post_mix, g_pre_xa, g_post_xa, g_pre_ffn, g_post_ffn):
    depth = w_in.shape[0]
    assert depth == 1
    bp, lp, _ = x_prompt.shape
    bs, ls, _ = x_sample.shape
    tl = 256
    tm = 512
    nb_mix = 8
    nb_xa = 8
    assert lp % (2 * tl) == 0 and tl % CHUNK == 0 and lp % tm == 0
    assert (bs * ls) % tm == 0 and bs % nb_mix == 0 and bs % nb_xa == 0
    assert ls % 8 == 0 and ls <= CHUNK and ls & (ls - 1) == 0 and tl & (tl - 1) == 0
    assert (bp * N_MEM) % tm == 0

    w = w_in[0]
    c_low = 2 * N_QK + 2 * D_MODEL
    assert c_low == N_WA and w.shape[1] == N_MAIN + GATE_RANK
    wt = jnp.transpose(w)
    n_wa_blocks = N_WA // PROJ_BLOCK

    def wt_row(i):
        units = PROJ_BLOCK // GATE_RANK
        return GATE_RANK * jnp.where(i < n_wa_blocks, i * units, i * units + 1)

    wmain, wga = pl.pallas_call(
        _split_w_in_kernel,
        grid=(N_MAIN // PROJ_BLOCK,),
        in_specs=[pl.BlockSpec((pl.Element(PROJ_BLOCK), pl.Element(D_MODEL)), lambda i: (wt_row(i), 0)),
                  pl.BlockSpec((GATE_RANK, D_MODEL), lambda i: (N_WA // GATE_RANK, 0))],
        out_specs=[pl.BlockSpec((D_MODEL, PROJ_BLOCK), lambda i: (0, i)),
                   pl.BlockSpec((D_MODEL, GATE_RANK_PAD), lambda i: (0, 0))],
        out_shape=[jax.ShapeDtypeStruct((D_MODEL, N_MAIN), BF16),
                   jax.ShapeDtypeStruct((D_MODEL, GATE_RANK_PAD), BF16)],
        compiler_params=_params(("arbitrary",)),
        name="split_w_in",
    )(wt, wt)
    w2 = jnp.pad(w_gla_a2[0], ((0, GATE_RANK_PAD - GATE_RANK), (0, 0))).astype(BF16)
    b2 = b_gla_a[0].reshape(1, N_QK)
    ggla = g_gla_head[0].reshape(1, D_MODEL)
    gret = g_ret_head[0].reshape(1, D_MODEL)
    wmix = w_mix_out[0].astype(BF16)
    wxq, wxk, wxv, wxo = (t[0].astype(BF16) for t in (w_xq, w_xk, w_xv, w_xo))
    wup, wdown = w_up[0].astype(BF16), w_down[0].astype(BF16)
    row = lambda g: g[0].reshape(1, D_MODEL)
    gmem, gpre_mix, gpost_mix, gpre_xa, gpost_xa, gpre_ffn, gpost_ffn = (
        row(g) for g in (g_mem, g_pre_mix, g_post_mix, g_pre_xa, g_post_xa, g_pre_ffn, g_post_ffn))
    cos_p, sin_p = _rope_tables(jnp.arange(lp, dtype=jnp.int32))
    cos_s, sin_s = _rope_tables(PAST_LEN + jnp.arange(ls, dtype=jnp.int32))
    rows_mix = nb_mix * ls
    cos_s, sin_s = jnp.tile(cos_s, (nb_mix, 1)), jnp.tile(sin_s, (nb_mix, 1))
    dm_p = _decay_masks(tl, tl)
    dm_s = _decay_masks(rows_mix, ls)

    nmem_rows = bp * N_MEM
    mk, mv, mk_bf, mv_bf = pl.pallas_call(
        _memkv_kernel,
        grid=(nmem_rows // tm,),
        in_specs=[_row_spec(tm, D_MODEL), _const_spec((1, D_MODEL)),
                  _const_spec((D_MODEL, D_MODEL)), _const_spec((D_MODEL, D_MODEL))],
        out_specs=[pl.BlockSpec((tm, N_HEADS, XA_DH), lambda i: (i, 0, 0))] * 2 + [_row_spec(tm, D_MODEL)] * 2,
        out_shape=[jax.ShapeDtypeStruct((nmem_rows, N_HEADS, XA_DH), F32)] * 2
        + [jax.ShapeDtypeStruct((nmem_rows, D_MODEL), BF16)] * 2,
        compiler_params=_params(("arbitrary",)),
        name="memkv",
    )(mem_prompt.reshape(nmem_rows, D_MODEL), gmem, wxk, wxv)

    n_tiles = bp * lp // tl
    steps_per_row = lp // (2 * tl)
    state_spec = pl.BlockSpec((1, N_HEADS, D_K, D_V), lambda s: (s // steps_per_row, 0, 0, 0))
    rope_spec = pl.BlockSpec((2 * tl, D_K), lambda s: (s % steps_per_row, 0))
    x1p, sg_p, sr_p = pl.pallas_call(
        functools.partial(_mixer_prompt_kernel, tl=tl, steps_per_row=steps_per_row),
        grid=(n_tiles // 2,),
        in_specs=[_row_spec(2 * tl, D_MODEL),
                  pl.BlockSpec((tl, D_MODEL), lambda s: (jnp.minimum(2 * s + 2, n_tiles - 1), 0)),
                  rope_spec, rope_spec,
                  _const_spec((N_HEADS, tl, tl)),
                  _const_spec((D_MODEL, N_MAIN)), _const_spec((D_MODEL, GATE_RANK_PAD)),
                  _const_spec((GATE_RANK_PAD, N_QK)), _const_spec((1, N_QK)),
                  _const_spec((1, D_MODEL)), _const_spec((1, D_MODEL)),
                  _const_spec((D_MODEL, D_MODEL)), _const_spec((1, D_MODEL)), _const_spec((1, D_MODEL))],
        out_specs=[_row_spec(2 * tl, D_MODEL), state_spec, state_spec],
        out_shape=[jax.ShapeDtypeStruct((bp * lp, D_MODEL), F32),
                   jax.ShapeDtypeStruct((bp, N_HEADS, D_K, D_V), F32),
                   jax.ShapeDtypeStruct((bp, N_HEADS, D_K, D_V), F32)],
        scratch_shapes=[pltpu.VMEM((2, tl, N_MAIN), F32), pltpu.VMEM((2, tl, N_QK), F32),
                        pltpu.VMEM((tl, D_MODEL), F32), pltpu.VMEM((tl, D_MODEL), F32),
                        pltpu.VMEM((tl, D_MODEL), BF16)],
        compiler_params=_params(("arbitrary",)),
        name="mixer_prompt",
    )(x_prompt.reshape(bp * lp, D_MODEL), x_prompt.reshape(bp * lp, D_MODEL), cos_p, sin_p, dm_p,
      wmain, wga, w2, b2, ggla, gret, wmix, gpre_mix, gpost_mix)
    x1p = x1p.reshape(bp, lp, D_MODEL)

    mem_spec = pl.BlockSpec((1, N_MEM, D_MODEL), lambda b, t: (b, 0, 0))
    x2p = pl.pallas_call(
        _xattn_prompt_kernel,
        grid=(bp, lp // tm),
        in_specs=[pl.BlockSpec((1, tm, D_MODEL), lambda b, t: (b, t, 0)), mem_spec, mem_spec,
                  _const_spec((D_MODEL, D_MODEL)), _const_spec((D_MODEL, D_MODEL)),
                  _const_spec((1, D_MODEL)), _const_spec((1, D_MODEL))],
        out_specs=pl.BlockSpec((1, tm, D_MODEL), lambda b, t: (b, t, 0)),
        out_shape=jax.ShapeDtypeStruct((bp, lp, D_MODEL), F32),
        scratch_shapes=[pltpu.VMEM((tm, D_MODEL), BF16)],
        compiler_params=_params(("arbitrary", "arbitrary")),
        name="xattn_prompt",
    )(x1p, mk_bf.reshape(bp, N_MEM, D_MODEL), mv_bf.reshape(bp, N_MEM, D_MODEL), wxq, wxo, gpre_xa, gpost_xa)

    yp = _mlp(x2p.reshape(bp * lp, D_MODEL), wup, wdown, gpre_ffn, gpost_ffn, tm).reshape(bp, lp, D_MODEL)

    ts = bs * ls
    xs = x_sample.reshape(ts, D_MODEL)
    z_s, la_s = pl.pallas_call(
        _inproj_kernel,
        grid=(ts // tm,),
        in_specs=[_row_spec(tm, D_MODEL), _const_spec((D_MODEL, N_MAIN)), _const_spec((D_MODEL, GATE_RANK_PAD)),
                  _const_spec((GATE_RANK_PAD, N_QK)), _const_spec((1, N_QK)), _const_spec((1, D_MODEL))],
        out_specs=[_row_spec(tm, N_MAIN), _row_spec(tm, N_QK)],
        out_shape=[jax.ShapeDtypeStruct((ts, N_MAIN), F32), jax.ShapeDtypeStruct((ts, N_QK), F32)],
        compiler_params=_params(("arbitrary",)),
        name="inproj_sample",
    )(xs, wmain, wga, w2, b2, gpre_mix)

    st_spec = pl.BlockSpec((nb_mix, N_HEADS, D_K, D_V), lambda i: (i, 0, 0, 0))
    merged_s, sg_s, sr_s = pl.pallas_call(
        functools.partial(_mixer_sample_kernel, nb=nb_mix, ls=ls),
        grid=(bs // nb_mix,),
        in_specs=[_row_spec(rows_mix, N_MAIN), _row_spec(rows_mix, N_QK),
                  _const_spec((rows_mix, D_K)), _const_spec((rows_mix, D_K)),
                  _const_spec((N_HEADS, rows_mix, rows_mix)), st_spec, st_spec,
                  _const_spec((1, D_MODEL)), _const_spec((1, D_MODEL))],
        out_specs=[_row_spec(rows_mix, D_MODEL), st_spec, st_spec],
        out_shape=[jax.ShapeDtypeStruct((ts, D_MODEL), BF16),
                   jax.ShapeDtypeStruct((bs, N_HEADS, D_K, D_V), F32),
                   jax.ShapeDtypeStruct((bs, N_HEADS, D_K, D_V), F32)],
        scratch_shapes=[pltpu.VMEM((rows_mix, D_MODEL), F32), pltpu.VMEM((rows_mix, D_MODEL), F32)],
        compiler_params=_params(("arbitrary",)),
        name="mixer_sample",
    )(z_s, la_s, cos_s, sin_s, dm_s, state_gla[0], state_ret[0], ggla, gret)

    def proj_res(a, res, wgt, g):
        return pl.pallas_call(
            _proj_res_kernel,
            grid=(ts // tm,),
            in_specs=[_row_spec(tm, D_MODEL), _row_spec(tm, D_MODEL), _const_spec((D_MODEL, D_MODEL)),
                      _const_spec((1, D_MODEL))],
            out_specs=_row_spec(tm, D_MODEL),
            out_shape=jax.ShapeDtypeStruct((ts, D_MODEL), F32),
            compiler_params=_params(("arbitrary",)),
            name="proj_res_sample",
        )(a, res, wgt, g)

    x1s = proj_res(merged_s, xs, wmix, gpost_mix)
    q_s = pl.pallas_call(
        _norm_mm_kernel,
        grid=(ts // tm,),
        in_specs=[_row_spec(tm, D_MODEL), _const_spec((1, D_MODEL)), _const_spec((D_MODEL, D_MODEL))],
        out_specs=_row_spec(tm, D_MODEL),
        out_shape=jax.ShapeDtypeStruct((ts, D_MODEL), F32),
        compiler_params=_params(("arbitrary",)),
        name="xq_sample",
    )(x1s, gpre_xa, wxq)

    rows_xa = nb_xa * ls
    kv_spec = pl.BlockSpec((nb_xa, N_MEM, N_HEADS, XA_DH), lambda i: (i, 0, 0, 0))
    o_s = pl.pallas_call(
        functools.partial(_xattn_sample_kernel, nb=nb_xa, ls=ls),
        grid=(bs // nb_xa,),
        in_specs=[_row_spec(rows_xa, D_MODEL), kv_spec, kv_spec],
        out_specs=_row_spec(rows_xa, D_MODEL),
        out_shape=jax.ShapeDtypeStruct((ts, D_MODEL), BF16),
        compiler_params=_params(("arbitrary",)),
        name="xattn_sample",
    )(q_s, cache_mem_k[0], cache_mem_v[0])

    x2s = proj_res(o_s, x1s, wxo, gpost_xa)
    ys = _mlp(x2s, wup, wdown, gpre_ffn, gpost_ffn, tm).reshape(bs, ls, D_MODEL)

    hshape = (1, bp, N_MEM, N_HEADS, XA_DH)
    return (yp, ys, sg_p[None], sr_p[None], mk.reshape(hshape), mv.reshape(hshape), sg_s[None], sr_s[None])
```

```python
import functools
import math

import jax
import jax.numpy as jnp
from jax import lax
from jax.experimental import pallas as pl
from jax.experimental.pallas import tpu as pltpu

F32 = jnp.float32
BF16 = jnp.bfloat16

D_MODEL = 1024
N_HEADS = 4
D_K = 128
D_V = 256
N_MEM = 256
XA_DH = 256
D_FF = 4 * D_MODEL
GATE_RANK = 16
GATE_RANK_PAD = 128
GLA_TAU = 16.0
CHUNK = 64
ROPE_BASE = 10000.0
PAST_LEN = 16384
EPS = 1e-6
LOG_GAMMA = tuple(math.log1p(-(2.0 ** (-5.0 - h))) for h in range(N_HEADS))

C_GQ, C_GK, C_GV, C_GR = 0, 512, 1024, 2048
C_RQ, C_RK, C_RV, C_RG = 3072, 3584, 4096, 5120
C_GA, C_GB = 6144, 7168
N_MAIN = 8192
N_WA = 3072
N_WB = N_MAIN - N_WA
N_QK = N_HEADS * D_K
PROJ_BLOCK = 512
MLP_BLOCK = 1024

LANE = 128
VMEM_LIMIT_BYTES = 60 * 1024 * 1024


def _mm(a, b):
    return jnp.dot(a, b, preferred_element_type=F32)


def _mm_nt(a, b):
    return lax.dot_general(a, b, (((1,), (1,)), ((), ())), preferred_element_type=F32)


def _mm_tn(a, b):
    return lax.dot_general(a, b, (((0,), (0,)), ((), ())), preferred_element_type=F32)


def _rms(x, g):
    return x * lax.rsqrt(jnp.mean(x * x, axis=-1, keepdims=True) + EPS) * g


def _sigmoid(x):
    return 1.0 / (1.0 + jnp.exp(-x))


def _const_spec(shape):
    nd = len(shape)
    return pl.BlockSpec(shape, lambda *_: (0,) * nd, pipeline_mode=pl.Buffered(1))


def _params(sem):
    return pltpu.CompilerParams(dimension_semantics=sem, vmem_limit_bytes=VMEM_LIMIT_BYTES)


def _inproj_pieces(h_bf, wmain_ref, wga_ref, w2_ref, b2_ref, z_ref, la_ref):
    def z_block(j):
        def run():
            z_ref[:, j:j + PROJ_BLOCK] = _mm(h_bf, wmain_ref[:, j:j + PROJ_BLOCK])
        return run

    def gate():
        ga = _mm(h_bf, wga_ref[...])
        xg = _mm(ga.astype(BF16), w2_ref[...]) + b2_ref[...]
        log_sig = jnp.minimum(xg, 0.0) - jnp.log1p(jnp.exp(-jnp.abs(xg)))
        la_ref[...] = log_sig * (1.0 / GLA_TAU)

    return [z_block(j) for j in range(0, N_MAIN, PROJ_BLOCK)] + [gate]


def _inproj(h_bf, wmain_ref, wga_ref, w2_ref, b2_ref, z_ref, la_ref):
    for piece in _inproj_pieces(h_bf, wmain_ref, wga_ref, w2_ref, b2_ref, z_ref, la_ref):
        piece()


def _run_next(pieces):
    if pieces:
        pieces.pop(0)()


def _block_causal(n, c):
    shift = c.bit_length() - 1
    row = lax.broadcasted_iota(jnp.int32, (n, n), 0)
    col = lax.broadcasted_iota(jnp.int32, (n, n), 1)
    return ((row >> shift) == (col >> shift)) & (row >= col)


def _gla_tile(z_ref, la_ref, n, c, s_get, s_put, g_ref, o_ref, between=()):
    causal = _block_causal(n, c)
    tri = jnp.where(causal, 1.0, 0.0).astype(BF16)
    la = la_ref[...]
    la_hi = la.astype(BF16)
    la_lo = (la - la_hi.astype(F32)).astype(BF16)
    b = _mm(tri, la_hi) + _mm(tri, la_lo)
    n_chunks = n // c
    b_last = [b[(ci + 1) * c - 1:(ci + 1) * c, :] for ci in range(n_chunks)]
    b_last_full = jnp.concatenate([jnp.broadcast_to(bl, (c, N_QK)) for bl in b_last], axis=0)
    eb = jnp.exp(b)
    enb = jnp.exp(-b)
    ekd = jnp.exp(b_last_full - b)
    for h in range(N_HEADS):
        ks = slice(h * D_K, (h + 1) * D_K)
        q = z_ref[:, C_GQ + h * D_K:C_GQ + (h + 1) * D_K] * (D_K ** -0.5)
        k = z_ref[:, C_GK + h * D_K:C_GK + (h + 1) * D_K]
        v = z_ref[:, C_GV + h * D_V:C_GV + (h + 1) * D_V].astype(BF16)
        qe = (q * eb[:, ks]).astype(BF16)
        ke = (k * enb[:, ks]).astype(BF16)
        kd = (k * ekd[:, ks]).astype(BF16)
        sc = jnp.where(causal, _mm_nt(qe, ke), 0.0).astype(BF16)
        o_intra = _mm(sc, v)
        _run_next(between)
        outs = []
        for ci in range(n_chunks):
            r = slice(ci * c, (ci + 1) * c)
            s_old = s_get(ci, h)
            outs.append(o_intra[r] + _mm(qe[r], s_old.astype(BF16)))
            dcol = jnp.transpose(jnp.broadcast_to(jnp.exp(b_last[ci][:, ks]), (D_K, D_K)))
            s_put(ci, h, jnp.concatenate([dcol, dcol], axis=1) * s_old + _mm_tn(kd[r], v[r]))
        o = jnp.concatenate(outs, axis=0) if n_chunks > 1 else outs[0]
        ms = jnp.mean(o * o, axis=-1, keepdims=True)
        o_ref[:, h * D_V:(h + 1) * D_V] = o * lax.rsqrt(ms + EPS) * g_ref[:, h * D_V:(h + 1) * D_V]
        _run_next(between)


def _ret_tile(z_ref, cos, sin, n, c, s_get, s_put, g_ref, o_ref, dm_ref, between=()):
    tpos = (lax.broadcasted_iota(jnp.int32, (n, 1), 0) & (c - 1)).astype(F32)
    n_chunks = n // c
    for h in range(N_HEADS):
        lg = LOG_GAMMA[h]
        q = z_ref[:, C_RQ + h * D_K:C_RQ + (h + 1) * D_K]
        k = z_ref[:, C_RK + h * D_K:C_RK + (h + 1) * D_K]
        v = z_ref[:, C_RV + h * D_V:C_RV + (h + 1) * D_V].astype(BF16)
        q = q * cos + pltpu.roll(q, D_K // 2, 1) * sin
        k = (k * cos + pltpu.roll(k, D_K // 2, 1) * sin) * (D_K ** -0.5)
        sc = (_mm_nt(q.astype(BF16), k.astype(BF16)) * dm_ref[h]).astype(BF16)
        o_intra = _mm(sc, v)
        _run_next(between)
        qd = (q * jnp.exp(lg * (tpos + 1.0))).astype(BF16)
        kd = (k * jnp.exp(lg * (float(c - 1) - tpos))).astype(BF16)
        outs = []
        for ci in range(n_chunks):
            r = slice(ci * c, (ci + 1) * c)
            s_old = s_get(ci, h)
            outs.append(o_intra[r] + _mm(qd[r], s_old.astype(BF16)))
            s_put(ci, h, math.exp(lg * c) * s_old + _mm_tn(kd[r], v[r]))
        o = jnp.concatenate(outs, axis=0) if n_chunks > 1 else outs[0]
        mu = jnp.mean(o, axis=-1, keepdims=True)
        oc = o - mu
        var = jnp.mean(oc * oc, axis=-1, keepdims=True)
        o_ref[:, h * D_V:(h + 1) * D_V] = oc * lax.rsqrt(var + EPS) * g_ref[:, h * D_V:(h + 1) * D_V]
        _run_next(between)


def _merge(z_ref, og_ref, or_ref, merged_ref):
    for j in range(0, D_MODEL, 256):
        cs = slice(j, j + 256)
        gr = z_ref[:, C_GR + j:C_GR + j + 256]
        rg = z_ref[:, C_RG + j:C_RG + j + 256]
        ga = z_ref[:, C_GA + j:C_GA + j + 256]
        gb = z_ref[:, C_GB + j:C_GB + j + 256]
        o_g = og_ref[:, cs] * (gr * _sigmoid(gr))
        o_r = or_ref[:, cs] * (rg * _sigmoid(rg))
        merged_ref[:, cs] = (_sigmoid(ga) * o_g + _sigmoid(gb) * o_r).astype(merged_ref.dtype)


def _split_w_in_kernel(wt_ref, wtg_ref, w_ref, wga_ref):
    w_ref[...] = jnp.transpose(wt_ref[...]).astype(BF16)
    zeros = jnp.zeros((GATE_RANK_PAD - GATE_RANK, D_MODEL), F32)
    wga_ref[...] = jnp.transpose(jnp.concatenate([wtg_ref[...], zeros], axis=0)).astype(BF16)


def _memkv_kernel(m_ref, g_ref, wk_ref, wv_ref, k_ref, v_ref, kb_ref, vb_ref):
    m = _rms(m_ref[...], g_ref[...]).astype(BF16)
    k = _mm(m, wk_ref[...])
    v = _mm(m, wv_ref[...])
    k_ref[...] = k.reshape(k_ref.shape)
    v_ref[...] = v.reshape(v_ref.shape)
    kb_ref[...] = k.astype(BF16)
    vb_ref[...] = v.astype(BF16)


def _mixer_prompt_kernel(xc_ref, xn_ref, cos_ref, sin_ref, dm_ref, wmain_ref, wga_ref, w2_ref, b2_ref,
                         ggla_ref, gret_ref, wmix_ref, gpre_ref, gpost_ref,
                         y_ref, sg_ref, sr_ref,
                         z_ref, la_ref, og_ref, or_ref, merged_ref, *, tl, steps_per_row):
    step = pl.program_id(0)

    @pl.when(step % steps_per_row == 0)
    def _():
        sg_ref[...] = jnp.zeros_like(sg_ref)
        sr_ref[...] = jnp.zeros_like(sr_ref)

    def project_pieces(x, slot):
        h_bf = _rms(x, gpre_ref[...]).astype(BF16)
        return _inproj_pieces(h_bf, wmain_ref, wga_ref, w2_ref, b2_ref, z_ref.at[slot], la_ref.at[slot])

    @pl.when(step == 0)
    def _():
        for piece in project_pieces(xc_ref[0:tl, :], 0):
            piece()

    n_chunks = tl // CHUNK
    for half in range(2):
        rows = slice(half * tl, (half + 1) * tl)
        pending = project_pieces(xc_ref[tl:2 * tl, :] if half == 0 else xn_ref[...], 1 - half)
        _run_next(pending)
        z_cur, la_cur = z_ref.at[half], la_ref.at[half]

        carried = {}

        def gla_get(ci, h, carried=carried):
            return sg_ref[0, h] if ci == 0 else carried[h]

        def gla_put(ci, h, val, carried=carried):
            carried[h] = val
            if ci == n_chunks - 1:
                sg_ref[0, h] = val

        _gla_tile(z_cur, la_cur, tl, CHUNK, gla_get, gla_put, ggla_ref, og_ref, between=pending)

        def ret_put(ci, h, val):
            sr_ref[0, h] = val

        _ret_tile(z_cur, cos_ref[rows, :], sin_ref[rows, :], tl, tl, lambda ci, h: sr_ref[0, h], ret_put,
                  gret_ref, or_ref, dm_ref, between=pending)
        while pending:
            _run_next(pending)

        _merge(z_cur, og_ref, or_ref, merged_ref)
        m = _mm(merged_ref[...], wmix_ref[...])
        y_ref[rows, :] = xc_ref[rows, :] + _rms(m, gpost_ref[...])


def _stagger(a_stages, b_stages):
    a, b = list(a_stages), list(b_stages)
    a.pop(0)()
    while a or b:
        _run_next(a)
        _run_next(b)


def _xattn_stages(x_ref, rows, mk_ref, mv_ref, wq_ref, wo_ref, gpre_ref, gpost_ref, y_ref, o_ref):
    st = {}

    def norm():
        st["hx"] = _rms(x_ref[rows, :], gpre_ref[...]).astype(BF16)

    def q_proj():
        st["q"] = _mm(st["hx"], wq_ref[...])

    def heads():
        for h in range(N_HEADS):
            hs = slice(h * XA_DH, (h + 1) * XA_DH)
            s = _mm_nt(st["q"][:, hs].astype(BF16), mk_ref[0, :, hs]) * (XA_DH ** -0.5)
            p = jnp.exp(s - jnp.max(s, axis=-1, keepdims=True))
            p = p * (1.0 / jnp.sum(p, axis=-1, keepdims=True))
            o_ref[rows, hs] = _mm(p.astype(BF16), mv_ref[0, :, hs]).astype(o_ref.dtype)

    def out_proj():
        st["a"] = _mm(o_ref[rows, :], wo_ref[...])

    def post():
        y_ref[rows, :] = x_ref[rows, :] + _rms(st["a"], gpost_ref[...])

    return [norm, q_proj, heads, out_proj, post]


def _xattn_prompt_kernel(x_ref, mk_ref, mv_ref, wq_ref, wo_ref, gpre_ref, gpost_ref, y_ref, o_ref):
    half = x_ref.shape[0] // 2
    args = (mk_ref, mv_ref, wq_ref, wo_ref, gpre_ref, gpost_ref, y_ref, o_ref)
    _stagger(_xattn_stages(x_ref, slice(0, half), *args), _xattn_stages(x_ref, slice(half, 2 * half), *args))


def _mlp_stages(x_ref, rows, wup_ref, wdown_ref, gpre_ref, gpost_ref, y_ref):
    st = {}

    def norm():
        st["h"] = _rms(x_ref[rows, :], gpre_ref[...]).astype(BF16)

    def hidden_block(j):
        def run():
            u = jnp.maximum(_mm(st["h"], wup_ref[:, j:j + MLP_BLOCK]), 0.0)
            part = _mm((u * u).astype(BF16), wdown_ref[j:j + MLP_BLOCK, :])
            st["acc"] = part if j == 0 else st["acc"] + part
        return run

    def post():
        y_ref[rows, :] = x_ref[rows, :] + _rms(st["acc"], gpost_ref[...])

    return [norm] + [hidden_block(j) for j in range(0, D_FF, MLP_BLOCK)] + [post]


def _mlp_kernel(x_ref, wup_ref, wdown_ref, gpre_ref, gpost_ref, y_ref):
    half = x_ref.shape[0] // 2
    args = (wup_ref, wdown_ref, gpre_ref, gpost_ref, y_ref)
    _stagger(_mlp_stages(x_ref, slice(0, half), *args), _mlp_stages(x_ref, slice(half, 2 * half), *args))


def _inproj_kernel(x_ref, wmain_ref, wga_ref, w2_ref, b2_ref, gpre_ref, z_ref, la_ref):
    h_bf = _rms(x_ref[...], gpre_ref[...]).astype(BF16)
    _inproj(h_bf, wmain_ref, wga_ref, w2_ref, b2_ref, z_ref, la_ref)


def _mixer_sample_kernel(z_ref, la_ref, cos_ref, sin_ref, dm_ref, sgi_ref, sri_ref, ggla_ref, gret_ref,
                         merged_ref, sgo_ref, sro_ref, og_ref, or_ref, *, nb, ls):
    def gla_put(ci, h, val):
        sgo_ref[ci, h] = val

    def ret_put(ci, h, val):
        sro_ref[ci, h] = val

    _gla_tile(z_ref, la_ref, nb * ls, ls, lambda ci, h: sgi_ref[ci, h], gla_put, ggla_ref, og_ref)
    _ret_tile(z_ref, cos_ref[...], sin_ref[...], nb * ls, ls, lambda ci, h: sri_ref[ci, h], ret_put,
              gret_ref, or_ref, dm_ref)
    _merge(z_ref, og_ref, or_ref, merged_ref)


def _post_mix_sample_kernel(a_ref, res_ref, wmix_ref, gpost_ref, gpre_xa_ref, wq_ref, x1_ref, q_ref):
    x1 = res_ref[...] + _rms(_mm(a_ref[...], wmix_ref[...]), gpost_ref[...])
    x1_ref[...] = x1
    q_ref[...] = _mm(_rms(x1, gpre_xa_ref[...]).astype(BF16), wq_ref[...])


def _post_xa_mlp_sample_kernel(o_ref, x1_ref, wo_ref, gpost_xa_ref, wup_ref, wdown_ref, gpre_ref, gpost_ref,
                               y_ref, x2_ref):
    x2_ref[...] = x1_ref[...] + _rms(_mm(o_ref[...], wo_ref[...]), gpost_xa_ref[...])
    _mlp_kernel(x2_ref, wup_ref, wdown_ref, gpre_ref, gpost_ref, y_ref)


def _xattn_sample_kernel(q_ref, k_ref, v_ref, o_ref, *, nb, ls):
    n_rows = N_HEADS * ls
    n_cols = N_MEM * N_HEADS
    row_head = lax.broadcasted_iota(jnp.int32, (n_rows, n_cols), 0) // ls
    col_head = lax.broadcasted_iota(jnp.int32, (n_rows, n_cols), 1) & (N_HEADS - 1)
    own = row_head == col_head

    def seq_body(e, carry):
        rows = pl.ds(pl.multiple_of(e * ls, ls), ls)
        q = q_ref[rows, :]
        q_hm = jnp.concatenate([q[:, h * XA_DH:(h + 1) * XA_DH] for h in range(N_HEADS)], axis=0)
        k_all = k_ref[e].reshape(n_cols, XA_DH)
        v_all = v_ref[e].reshape(n_cols, XA_DH)
        s = _mm_nt(q_hm.astype(BF16), k_all.astype(BF16)) * (XA_DH ** -0.5)
        s = jnp.where(own, s, -jnp.inf)
        p = jnp.exp(s - jnp.max(s, axis=-1, keepdims=True))
        p = p * (1.0 / jnp.sum(p, axis=-1, keepdims=True))
        o = _mm(p.astype(BF16), v_all.astype(BF16))
        for h in range(N_HEADS):
            o_ref[rows, h * XA_DH:(h + 1) * XA_DH] = o[h * ls:(h + 1) * ls, :].astype(o_ref.dtype)
        return carry

    lax.fori_loop(0, nb, seq_body, 0, unroll=4)


def _rope_tables(pos):
    half = D_K // 2
    inv = ROPE_BASE ** (-jnp.arange(half, dtype=F32) / half)
    ang = pos.astype(F32)[:, None] * inv[None, :]
    cos, sin = jnp.cos(ang), jnp.sin(ang)
    return jnp.concatenate([cos, cos], axis=-1), jnp.concatenate([-sin, sin], axis=-1)


def _decay_masks(n, c):
    t = jnp.arange(n, dtype=jnp.int32)
    keep = ((t[:, None] // c) == (t[None, :] // c)) & (t[:, None] >= t[None, :])
    dist = (t[:, None] - t[None, :]).astype(F32)
    lg = jnp.asarray(LOG_GAMMA, F32)[:, None, None]
    return jnp.where(keep[None], jnp.exp(lg * dist[None]), 0.0)


def _row_spec(tm, n):
    return pl.BlockSpec((tm, n), lambda i: (i, 0))


def _mlp(x, wup, wdown, gpre, gpost, tm):
    t = x.shape[0]
    return pl.pallas_call(
        _mlp_kernel,
        grid=(t // tm,),
        in_specs=[_row_spec(tm, D_MODEL), _const_spec((D_MODEL, D_FF)), _const_spec((D_FF, D_MODEL)),
                  _const_spec((1, D_MODEL)), _const_spec((1, D_MODEL))],
        out_specs=_row_spec(tm, D_MODEL),
        out_shape=jax.ShapeDtypeStruct((t, D_MODEL), F32),
        compiler_params=_params(("arbitrary",)),
        name="mlp",
    )(x, wup, wdown, gpre, gpost)


def kernel(x_prompt, x_sample, state_gla, state_ret, cache_mem_k, cache_mem_v, mem_prompt, w_in, w_gla_a2, b_gla_a, g_gla_head, g_ret_head, w_mix_out, w_xq, w_xk, w_xv, w_xo, g_mem, w_up, w_down, g_pre_mix, g_post_mix, g_pre_xa, g_post_xa, g_pre_ffn, g_post_ffn):
    depth = w_in.shape[0]
    assert depth == 1
    bp, lp, _ = x_prompt.shape
    bs, ls, _ = x_sample.shape
    tl = 256
    tm = 512
    ts_tile = 256
    nb_mix = 8
    nb_xa = 8
    assert lp % (2 * tl) == 0 and tl % CHUNK == 0 and lp % tm == 0
    assert (bs * ls) % ts_tile == 0 and bs % nb_mix == 0 and bs % nb_xa == 0
    assert ls % 8 == 0 and ls <= CHUNK and ls & (ls - 1) == 0 and tl & (tl - 1) == 0
    assert (bp * N_MEM) % tm == 0

    w = w_in[0]
    c_low = 2 * N_QK + 2 * D_MODEL
    assert c_low == N_WA and w.shape[1] == N_MAIN + GATE_RANK
    wt = jnp.transpose(w)
    n_wa_blocks = N_WA // PROJ_BLOCK

    def wt_row(i):
        units = PROJ_BLOCK // GATE_RANK
        return GATE_RANK * jnp.where(i < n_wa_blocks, i * units, i * units + 1)

    wmain, wga = pl.pallas_call(
        _split_w_in_kernel,
        grid=(N_MAIN // PROJ_BLOCK,),
        in_specs=[pl.BlockSpec((pl.Element(PROJ_BLOCK), pl.Element(D_MODEL)), lambda i: (wt_row(i), 0)),
                  pl.BlockSpec((GATE_RANK, D_MODEL), lambda i: (N_WA // GATE_RANK, 0))],
        out_specs=[pl.BlockSpec((D_MODEL, PROJ_BLOCK), lambda i: (0, i)),
                   pl.BlockSpec((D_MODEL, GATE_RANK_PAD), lambda i: (0, 0))],
        out_shape=[jax.ShapeDtypeStruct((D_MODEL, N_MAIN), BF16),
                   jax.ShapeDtypeStruct((D_MODEL, GATE_RANK_PAD), BF16)],
        compiler_params=_params(("arbitrary",)),
        name="split_w_in",
    )(wt, wt)
    w2 = jnp.pad(w_gla_a2[0], ((0, GATE_RANK_PAD - GATE_RANK), (0, 0))).astype(BF16)
    b2 = b_gla_a[0].reshape(1, N_QK)
    ggla = g_gla_head[0].reshape(1, D_MODEL)
    gret = g_ret_head[0].reshape(1, D_MODEL)
    wmix = w_mix_out[0].astype(BF16)
    wxq, wxk, wxv, wxo = (t[0].astype(BF16) for t in (w_xq, w_xk, w_xv, w_xo))
    wup, wdown = w_up[0].astype(BF16), w_down[0].astype(BF16)
    row = lambda g: g[0].reshape(1, D_MODEL)
    gmem, gpre_mix, gpost_mix, gpre_xa, gpost_xa, gpre_ffn, gpost_ffn = (
        row(g) for g in (g_mem, g_pre_mix, g_post_mix, g_pre_xa, g_post_xa, g_pre_ffn, g_post_ffn))
    cos_p, sin_p = _rope_tables(jnp.arange(lp, dtype=jnp.int32))
    cos_s, sin_s = _rope_tables(PAST_LEN + jnp.arange(ls, dtype=jnp.int32))
    rows_mix = nb_mix * ls
    cos_s, sin_s = jnp.tile(cos_s, (nb_mix, 1)), jnp.tile(sin_s, (nb_mix, 1))
    dm_p = _decay_masks(tl, tl)
    dm_s = _decay_masks(rows_mix, ls)

    nmem_rows = bp * N_MEM
    mk, mv, mk_bf, mv_bf = pl.pallas_call(
        _memkv_kernel,
        grid=(nmem_rows // tm,),
        in_specs=[_row_spec(tm, D_MODEL), _const_spec((1, D_MODEL)),
                  _const_spec((D_MODEL, D_MODEL)), _const_spec((D_MODEL, D_MODEL))],
        out_specs=[pl.BlockSpec((tm, N_HEADS, XA_DH), lambda i: (i, 0, 0))] * 2 + [_row_spec(tm, D_MODEL)] * 2,
        out_shape=[jax.ShapeDtypeStruct((nmem_rows, N_HEADS, XA_DH), F32)] * 2
        + [jax.ShapeDtypeStruct((nmem_rows, D_MODEL), BF16)] * 2,
        compiler_params=_params(("arbitrary",)),
        name="memkv",
    )(mem_prompt.reshape(nmem_rows, D_MODEL), gmem, wxk, wxv)

    n_tiles = bp * lp // tl
    steps_per_row = lp // (2 * tl)
    state_spec = pl.BlockSpec((1, N_HEADS, D_K, D_V), lambda s: (s // steps_per_row, 0, 0, 0))
    rope_spec = pl.BlockSpec((2 * tl, D_K), lambda s: (s % steps_per_row, 0))
    x1p, sg_p, sr_p = pl.pallas_call(
        functools.partial(_mixer_prompt_kernel, tl=tl, steps_per_row=steps_per_row),
        grid=(n_tiles // 2,),
        in_specs=[_row_spec(2 * tl, D_MODEL),
                  pl.BlockSpec((tl, D_MODEL), lambda s: (jnp.minimum(2 * s + 2, n_tiles - 1), 0)),
                  rope_spec, rope_spec,
                  _const_spec((N_HEADS, tl, tl)),
                  _const_spec((D_MODEL, N_MAIN)), _const_spec((D_MODEL, GATE_RANK_PAD)),
                  _const_spec((GATE_RANK_PAD, N_QK)), _const_spec((1, N_QK)),
                  _const_spec((1, D_MODEL)), _const_spec((1, D_MODEL)),
                  _const_spec((D_MODEL, D_MODEL)), _const_spec((1, D_MODEL)), _const_spec((1, D_MODEL))],
        out_specs=[_row_spec(2 * tl, D_MODEL), state_spec, state_spec],
        out_shape=[jax.ShapeDtypeStruct((bp * lp, D_MODEL), F32),
                   jax.ShapeDtypeStruct((bp, N_HEADS, D_K, D_V), F32),
                   jax.ShapeDtypeStruct((bp, N_HEADS, D_K, D_V), F32)],
        scratch_shapes=[pltpu.VMEM((2, tl, N_MAIN), F32), pltpu.VMEM((2, tl, N_QK), F32),
                        pltpu.VMEM((tl, D_MODEL), F32), pltpu.VMEM((tl, D_MODEL), F32),
                        pltpu.VMEM((tl, D_MODEL), BF16)],
        compiler_params=_params(("arbitrary",)),
        name="mixer_prompt",
    )(x_prompt.reshape(bp * lp, D_MODEL), x_prompt.reshape(bp * lp, D_MODEL), cos_p, sin_p, dm_p,
      wmain, wga, w2, b2, ggla, gret, wmix, gpre_mix, gpost_mix)
    x1p = x1p.reshape(bp, lp, D_MODEL)

    xa_per_row = lp // tm
    mem_spec = pl.BlockSpec((1, N_MEM, D_MODEL), lambda s: (s // xa_per_row, 0, 0))
    x2p = pl.pallas_call(
        _xattn_prompt_kernel,
        grid=(bp * lp // tm,),
        in_specs=[_row_spec(tm, D_MODEL), mem_spec, mem_spec,
                  _const_spec((D_MODEL, D_MODEL)), _const_spec((D_MODEL, D_MODEL)),
                  _const_spec((1, D_MODEL)), _const_spec((1, D_MODEL))],
        out_specs=_row_spec(tm, D_MODEL),
        out_shape=jax.ShapeDtypeStruct((bp * lp, D_MODEL), F32),
        scratch_shapes=[pltpu.VMEM((tm, D_MODEL), BF16)],
        compiler_params=_params(("arbitrary",)),
        name="xattn_prompt",
    )(x1p.reshape(bp * lp, D_MODEL), mk_bf.reshape(bp, N_MEM, D_MODEL), mv_bf.reshape(bp, N_MEM, D_MODEL),
      wxq, wxo, gpre_xa, gpost_xa)

    yp = _mlp(x2p, wup, wdown, gpre_ffn, gpost_ffn, tm).reshape(bp, lp, D_MODEL)

    ts = bs * ls
    xs = x_sample.reshape(ts, D_MODEL)
    z_s, la_s = pl.pallas_call(
        _inproj_kernel,
        grid=(ts // ts_tile,),
        in_specs=[_row_spec(ts_tile, D_MODEL), _const_spec((D_MODEL, N_MAIN)),
                  _const_spec((D_MODEL, GATE_RANK_PAD)),
                  _const_spec((GATE_RANK_PAD, N_QK)), _const_spec((1, N_QK)), _const_spec((1, D_MODEL))],
        out_specs=[_row_spec(ts_tile, N_MAIN), _row_spec(ts_tile, N_QK)],
        out_shape=[jax.ShapeDtypeStruct((ts, N_MAIN), F32), jax.ShapeDtypeStruct((ts, N_QK), F32)],
        compiler_params=_params(("arbitrary",)),
        name="inproj_sample",
    )(xs, wmain, wga, w2, b2, gpre_mix)

    st_spec = pl.BlockSpec((nb_mix, N_HEADS, D_K, D_V), lambda i: (i, 0, 0, 0))
    merged_s, sg_s, sr_s = pl.pallas_call(
        functools.partial(_mixer_sample_kernel, nb=nb_mix, ls=ls),
        grid=(bs // nb_mix,),
        in_specs=[_row_spec(rows_mix, N_MAIN), _row_spec(rows_mix, N_QK),
                  _const_spec((rows_mix, D_K)), _const_spec((rows_mix, D_K)),
                  _const_spec((N_HEADS, rows_mix, rows_mix)), st_spec, st_spec,
                  _const_spec((1, D_MODEL)), _const_spec((1, D_MODEL))],
        out_specs=[_row_spec(rows_mix, D_MODEL), st_spec, st_spec],
        out_shape=[jax.ShapeDtypeStruct((ts, D_MODEL), BF16),
                   jax.ShapeDtypeStruct((bs, N_HEADS, D_K, D_V), F32),
                   jax.ShapeDtypeStruct((bs, N_HEADS, D_K, D_V), F32)],
        scratch_shapes=[pltpu.VMEM((rows_mix, D_MODEL), F32), pltpu.VMEM((rows_mix, D_MODEL), F32)],
        compiler_params=_params(("arbitrary",)),
        name="mixer_sample",
    )(z_s, la_s, cos_s, sin_s, dm_s, state_gla[0], state_ret[0], ggla, gret)

    x1s, q_s = pl.pallas_call(
        _post_mix_sample_kernel,
        grid=(ts // ts_tile,),
        in_specs=[_row_spec(ts_tile, D_MODEL), _row_spec(ts_tile, D_MODEL), _const_spec((D_MODEL, D_MODEL)),
                  _const_spec((1, D_MODEL)), _const_spec((1, D_MODEL)), _const_spec((D_MODEL, D_MODEL))],
        out_specs=[_row_spec(ts_tile, D_MODEL)] * 2,
        out_shape=[jax.ShapeDtypeStruct((ts, D_MODEL), F32)] * 2,
        compiler_params=_params(("arbitrary",)),
        name="post_mix_sample",
    )(merged_s, xs, wmix, gpost_mix, gpre_xa, wxq)

    rows_xa = nb_xa * ls
    kv_spec = pl.BlockSpec((nb_xa, N_MEM, N_HEADS, XA_DH), lambda i: (i, 0, 0, 0))
    o_s = pl.pallas_call(
        functools.partial(_xattn_sample_kernel, nb=nb_xa, ls=ls),
        grid=(bs // nb_xa,),
        in_specs=[_row_spec(rows_xa, D_MODEL), kv_spec, kv_spec],
        out_specs=_row_spec(rows_xa, D_MODEL),
        out_shape=jax.ShapeDtypeStruct((ts, D_MODEL), BF16),
        compiler_params=_params(("arbitrary",)),
        name="xattn_sample",
    )(q_s, cache_mem_k[0], cache_mem_v[0])

    ys = pl.pallas_call(
        _post_xa_mlp_sample_kernel,
        grid=(ts // ts_tile,),
        in_specs=[_row_spec(ts_tile, D_MODEL), _row_spec(ts_tile, D_MODEL), _const_spec((D_MODEL, D_MODEL)),
                  _const_spec((1, D_MODEL)), _const_spec((D_MODEL, D_FF)), _const_spec((D_FF, D_MODEL)),
                  _const_spec((1, D_MODEL)), _const_spec((1, D_MODEL))],
        out_specs=_row_spec(ts_tile, D_MODEL),
        out_shape=jax.ShapeDtypeStruct((ts, D_MODEL), F32),
        scratch_shapes=[pltpu.VMEM((ts_tile, D_MODEL), F32)],
        compiler_params=_params(("arbitrary",)),
        name="post_xa_mlp_sample",
    )(o_s, x1s, wxo, gpost_xa, wup, wdown, gpre_ffn, gpost_ffn).reshape(bs, ls, D_MODEL)

    hshape = (1, bp, N_MEM, N_HEADS, XA_DH)
    return (yp, ys, sg_p[None], sr_p[None], mk.reshape(hshape), mv.reshape(hshape), sg_s[None], sr_s[None])
```

```python
import functools
import math

import jax
import jax.numpy as jnp
from jax import lax
from jax.experimental import pallas as pl
from jax.experimental.pallas import tpu as pltpu

F32 = jnp.float32
BF16 = jnp.bfloat16

D_MODEL = 1024
N_HEADS = 4
D_K = 128
D_V = 256
N_MEM = 256
XA_DH = 256
D_FF = 4 * D_MODEL
GATE_RANK = 16
GATE_RANK_PAD = 128
GLA_TAU = 16.0
CHUNK = 64
ROPE_BASE = 10000.0
PAST_LEN = 16384
EPS = 1e-6
LOG_GAMMA = tuple(math.log1p(-(2.0 ** (-5.0 - h))) for h in range(N_HEADS))

C_GQ, C_GK, C_GV, C_GR = 0, 512, 1024, 2048
C_RQ, C_RK, C_RV, C_RG = 3072, 3584, 4096, 5120
C_GA, C_GB = 6144, 7168
N_MAIN = 8192
N_WA = 3072
N_WB = N_MAIN - N_WA
N_QK = N_HEADS * D_K
PROJ_BLOCK = 512
MLP_BLOCK = 1024

LANE = 128
VMEM_LIMIT_BYTES = 60 * 1024 * 1024


def _mm(a, b):
    return jnp.dot(a, b, preferred_element_type=F32)


def _mm_nt(a, b):
    return lax.dot_general(a, b, (((1,), (1,)), ((), ())), preferred_element_type=F32)


def _mm_tn(a, b):
    return lax.dot_general(a, b, (((0,), (0,)), ((), ())), preferred_element_type=F32)


def _rms(x, g):
    return x * lax.rsqrt(jnp.mean(x * x, axis=-1, keepdims=True) + EPS) * g


def _sigmoid(x):
    return 1.0 / (1.0 + jnp.exp(-x))


def _const_spec(shape):
    nd = len(shape)
    return pl.BlockSpec(shape, lambda *_: (0,) * nd, pipeline_mode=pl.Buffered(1))


def _params(sem):
    return pltpu.CompilerParams(dimension_semantics=sem, vmem_limit_bytes=VMEM_LIMIT_BYTES)


def _inproj_pieces(h_bf, wmain_ref, wga_ref, w2_ref, b2_ref, z_ref, la_ref):
    def z_block(j):
        def run():
            z_ref[:, j:j + PROJ_BLOCK] = _mm(h_bf, wmain_ref[:, j:j + PROJ_BLOCK])
        return run

    def gate():
        ga = _mm(h_bf, wga_ref[...])
        xg = _mm(ga.astype(BF16), w2_ref[...]) + b2_ref[...]
        log_sig = jnp.minimum(xg, 0.0) - jnp.log1p(jnp.exp(-jnp.abs(xg)))
        la_ref[...] = log_sig * (1.0 / GLA_TAU)

    return [z_block(j) for j in range(0, N_MAIN, PROJ_BLOCK)] + [gate]


def _inproj(h_bf, wmain_ref, wga_ref, w2_ref, b2_ref, z_ref, la_ref):
    for piece in _inproj_pieces(h_bf, wmain_ref, wga_ref, w2_ref, b2_ref, z_ref, la_ref):
        piece()


def _run_next(pieces):
    if pieces:
        pieces.pop(0)()


def _block_causal(n, c):
    shift = c.bit_length() - 1
    row = lax.broadcasted_iota(jnp.int32, (n, n), 0)
    col = lax.broadcasted_iota(jnp.int32, (n, n), 1)
    return ((row >> shift) == (col >> shift)) & (row >= col)


def _gla_tile(z_ref, la_ref, n, c, s_get, s_put, g_ref, o_ref, between=()):
    causal = _block_causal(n, c)
    tri = jnp.where(causal, 1.0, 0.0).astype(BF16)
    la = la_ref[...]
    la_hi = la.astype(BF16)
    la_lo = (la - la_hi.astype(F32)).astype(BF16)
    b = _mm(tri, la_hi) + _mm(tri, la_lo)
    n_chunks = n // c
    b_last = [b[(ci + 1) * c - 1:(ci + 1) * c, :] for ci in range(n_chunks)]
    b_last_full = jnp.concatenate([jnp.broadcast_to(bl, (c, N_QK)) for bl in b_last], axis=0)
    eb = jnp.exp(b)
    enb = jnp.exp(-b)
    ekd = jnp.exp(b_last_full - b)
    for h in range(N_HEADS):
        ks = slice(h * D_K, (h + 1) * D_K)
        q = z_ref[:, C_GQ + h * D_K:C_GQ + (h + 1) * D_K] * (D_K ** -0.5)
        k = z_ref[:, C_GK + h * D_K:C_GK + (h + 1) * D_K]
        v = z_ref[:, C_GV + h * D_V:C_GV + (h + 1) * D_V].astype(BF16)
        qe = (q * eb[:, ks]).astype(BF16)
        ke = (k * enb[:, ks]).astype(BF16)
        kd = (k * ekd[:, ks]).astype(BF16)
        sc = jnp.where(causal, _mm_nt(qe, ke), 0.0).astype(BF16)
        o_intra = _mm(sc, v)
        _run_next(between)
        outs = []
        for ci in range(n_chunks):
            r = slice(ci * c, (ci + 1) * c)
            s_old = s_get(ci, h)
            outs.append(o_intra[r] + _mm(qe[r], s_old.astype(BF16)))
            dcol = jnp.transpose(jnp.broadcast_to(jnp.exp(b_last[ci][:, ks]), (D_K, D_K)))
            s_put(ci, h, jnp.concatenate([dcol, dcol], axis=1) * s_old + _mm_tn(kd[r], v[r]))
        o = jnp.concatenate(outs, axis=0) if n_chunks > 1 else outs[0]
        ms = jnp.mean(o * o, axis=-1, keepdims=True)
        o_ref[:, h * D_V:(h + 1) * D_V] = o * lax.rsqrt(ms + EPS) * g_ref[:, h * D_V:(h + 1) * D_V]
        _run_next(between)


def _ret_tile(z_ref, cos, sin, n, c, s_get, s_put, g_ref, o_ref, dm_ref, between=()):
    tpos = (lax.broadcasted_iota(jnp.int32, (n, 1), 0) & (c - 1)).astype(F32)
    n_chunks = n // c
    for h in range(N_HEADS):
        lg = LOG_GAMMA[h]
        q = z_ref[:, C_RQ + h * D_K:C_RQ + (h + 1) * D_K]
        k = z_ref[:, C_RK + h * D_K:C_RK + (h + 1) * D_K]
        v = z_ref[:, C_RV + h * D_V:C_RV + (h + 1) * D_V].astype(BF16)
        q = q * cos + pltpu.roll(q, D_K // 2, 1) * sin
        k = (k * cos + pltpu.roll(k, D_K // 2, 1) * sin) * (D_K ** -0.5)
        sc = (_mm_nt(q.astype(BF16), k.astype(BF16)) * dm_ref[h]).astype(BF16)
        o_intra = _mm(sc, v)
        _run_next(between)
        qd = (q * jnp.exp(lg * (tpos + 1.0))).astype(BF16)
        kd = (k * jnp.exp(lg * (float(c - 1) - tpos))).astype(BF16)
        outs = []
        for ci in range(n_chunks):
            r = slice(ci * c, (ci + 1) * c)
            s_old = s_get(ci, h)
            outs.append(o_intra[r] + _mm(qd[r], s_old.astype(BF16)))
            s_put(ci, h, math.exp(lg * c) * s_old + _mm_tn(kd[r], v[r]))
        o = jnp.concatenate(outs, axis=0) if n_chunks > 1 else outs[0]
        mu = jnp.mean(o, axis=-1, keepdims=True)
        oc = o - mu
        var = jnp.mean(oc * oc, axis=-1, keepdims=True)
        o_ref[:, h * D_V:(h + 1) * D_V] = oc * lax.rsqrt(var + EPS) * g_ref[:, h * D_V:(h + 1) * D_V]
        _run_next(between)


def _merge(z_ref, og_ref, or_ref, merged_ref):
    for j in range(0, D_MODEL, 256):
        cs = slice(j, j + 256)
        gr = z_ref[:, C_GR + j:C_GR + j + 256]
        rg = z_ref[:, C_RG + j:C_RG + j + 256]
        ga = z_ref[:, C_GA + j:C_GA + j + 256]
        gb = z_ref[:, C_GB + j:C_GB + j + 256]
        o_g = og_ref[:, cs] * (gr * _sigmoid(gr))
        o_r = or_ref[:, cs] * (rg * _sigmoid(rg))
        merged_ref[:, cs] = (_sigmoid(ga) * o_g + _sigmoid(gb) * o_r).astype(merged_ref.dtype)


def _split_w_in_kernel(wt_ref, wtg_ref, w_ref, wga_ref):
    w_ref[...] = jnp.transpose(wt_ref[...]).astype(BF16)
    zeros = jnp.zeros((GATE_RANK_PAD - GATE_RANK, D_MODEL), F32)
    wga_ref[...] = jnp.transpose(jnp.concatenate([wtg_ref[...], zeros], axis=0)).astype(BF16)


def _memkv_kernel(m_ref, g_ref, wk_ref, wv_ref, k_ref, v_ref, kb_ref, vb_ref):
    m = _rms(m_ref[...], g_ref[...]).astype(BF16)
    k = _mm(m, wk_ref[...])
    v = _mm(m, wv_ref[...])
    k_ref[...] = k.reshape(k_ref.shape)
    v_ref[...] = v.reshape(v_ref.shape)
    kb_ref[...] = k.astype(BF16)
    vb_ref[...] = v.astype(BF16)


def _mixer_prompt_kernel(xc_ref, xn_ref, cos_ref, sin_ref, dm_ref, wmain_ref, wga_ref, w2_ref, b2_ref,
                         ggla_ref, gret_ref, wmix_ref, gpre_ref, gpost_ref,
                         y_ref, sg_ref, sr_ref,
                         z_ref, la_ref, og_ref, or_ref, merged_ref, *, tl, steps_per_row):
    step = pl.program_id(0)

    @pl.when(step % steps_per_row == 0)
    def _():
        sg_ref[...] = jnp.zeros_like(sg_ref)
        sr_ref[...] = jnp.zeros_like(sr_ref)

    def project_pieces(x, slot):
        h_bf = _rms(x, gpre_ref[...]).astype(BF16)
        return _inproj_pieces(h_bf, wmain_ref, wga_ref, w2_ref, b2_ref, z_ref.at[slot], la_ref.at[slot])

    @pl.when(step == 0)
    def _():
        for piece in project_pieces(xc_ref[0:tl, :], 0):
            piece()

    n_chunks = tl // CHUNK
    for half in range(2):
        rows = slice(half * tl, (half + 1) * tl)
        pending = project_pieces(xc_ref[tl:2 * tl, :] if half == 0 else xn_ref[...], 1 - half)
        _run_next(pending)
        z_cur, la_cur = z_ref.at[half], la_ref.at[half]

        carried = {}

        def gla_get(ci, h, carried=carried):
            return sg_ref[0, h] if ci == 0 else carried[h]

        def gla_put(ci, h, val, carried=carried):
            carried[h] = val
            if ci == n_chunks - 1:
                sg_ref[0, h] = val

        _gla_tile(z_cur, la_cur, tl, CHUNK, gla_get, gla_put, ggla_ref, og_ref, between=pending)

        def ret_put(ci, h, val):
            sr_ref[0, h] = val

        _ret_tile(z_cur, cos_ref[rows, :], sin_ref[rows, :], tl, tl, lambda ci, h: sr_ref[0, h], ret_put,
                  gret_ref, or_ref, dm_ref, between=pending)
        while pending:
            _run_next(pending)

        _merge(z_cur, og_ref, or_ref, merged_ref)
        m = _mm(merged_ref[...], wmix_ref[...])
        y_ref[rows, :] = xc_ref[rows, :] + _rms(m, gpost_ref[...])


def _stagger(a_stages, b_stages):
    for stage in _staggered(a_stages, b_stages):
        stage()


def _staggered(a_stages, b_stages):
    a, b = list(a_stages), list(b_stages)
    order = [a.pop(0)]
    while a or b:
        order += a[:1] + b[:1]
        a, b = a[1:], b[1:]
    return order


def _xattn_stages(x_ref, rows, mk_ref, mv_ref, wq_ref, wo_ref, gpre_ref, gpost_ref, y_ref, o_ref):
    st = {}

    def norm():
        st["hx"] = _rms(x_ref[rows, :], gpre_ref[...]).astype(BF16)

    def q_proj():
        st["q"] = _mm(st["hx"], wq_ref[...])

    def heads():
        for h in range(N_HEADS):
            hs = slice(h * XA_DH, (h + 1) * XA_DH)
            s = _mm_nt(st["q"][:, hs].astype(BF16), mk_ref[0, :, hs]) * (XA_DH ** -0.5)
            p = jnp.exp(s - jnp.max(s, axis=-1, keepdims=True))
            p = p * (1.0 / jnp.sum(p, axis=-1, keepdims=True))
            o_ref[rows, hs] = _mm(p.astype(BF16), mv_ref[0, :, hs]).astype(o_ref.dtype)

    def out_proj():
        st["a"] = _mm(o_ref[rows, :], wo_ref[...])

    def post():
        y_ref[rows, :] = x_ref[rows, :] + _rms(st["a"], gpost_ref[...])

    return [norm, q_proj, heads, out_proj, post]


def _xattn_prompt_kernel(x_ref, mk_ref, mv_ref, wq_ref, wo_ref, gpre_ref, gpost_ref, y_ref, o_ref):
    half = x_ref.shape[0] // 2
    args = (mk_ref, mv_ref, wq_ref, wo_ref, gpre_ref, gpost_ref, y_ref, o_ref)
    _stagger(_xattn_stages(x_ref, slice(0, half), *args), _xattn_stages(x_ref, slice(half, 2 * half), *args))


def _mlp_stages(x_ref, rows, wup_ref, wdown_ref, gpre_ref, gpost_ref, y_ref):
    st = {}

    def norm():
        st["h"] = _rms(x_ref[rows, :], gpre_ref[...]).astype(BF16)

    def hidden_block(j):
        def run():
            u = jnp.maximum(_mm(st["h"], wup_ref[:, j:j + MLP_BLOCK]), 0.0)
            part = _mm((u * u).astype(BF16), wdown_ref[j:j + MLP_BLOCK, :])
            st["acc"] = part if j == 0 else st["acc"] + part
        return run

    def post():
        y_ref[rows, :] = x_ref[rows, :] + _rms(st["acc"], gpost_ref[...])

    return [norm] + [hidden_block(j) for j in range(0, D_FF, MLP_BLOCK)] + [post]


def _mlp_halves(x_ref, wup_ref, wdown_ref, gpre_ref, gpost_ref, y_ref):
    half = x_ref.shape[0] // 2
    args = (wup_ref, wdown_ref, gpre_ref, gpost_ref, y_ref)
    return _staggered(_mlp_stages(x_ref, slice(0, half), *args), _mlp_stages(x_ref, slice(half, 2 * half), *args))


def _mlp_kernel(x_ref, wup_ref, wdown_ref, gpre_ref, gpost_ref, y_ref):
    for stage in _mlp_halves(x_ref, wup_ref, wdown_ref, gpre_ref, gpost_ref, y_ref):
        stage()


def _inproj_kernel(x_ref, wmain_ref, wga_ref, w2_ref, b2_ref, gpre_ref, z_ref, la_ref):
    h_bf = _rms(x_ref[...], gpre_ref[...]).astype(BF16)
    _inproj(h_bf, wmain_ref, wga_ref, w2_ref, b2_ref, z_ref, la_ref)


def _mixer_sample_body(z_ref, la_ref, cos_ref, sin_ref, dm_ref, sgi_ref, sri_ref, ggla_ref, gret_ref,
                       merged_ref, sgo_ref, sro_ref, og_ref, or_ref, nb, ls, between=()):
    def gla_put(ci, h, val):
        sgo_ref[ci, h] = val

    def ret_put(ci, h, val):
        sro_ref[ci, h] = val

    _gla_tile(z_ref, la_ref, nb * ls, ls, lambda ci, h: sgi_ref[ci, h], gla_put, ggla_ref, og_ref, between=between)
    _ret_tile(z_ref, cos_ref[...], sin_ref[...], nb * ls, ls, lambda ci, h: sri_ref[ci, h], ret_put,
              gret_ref, or_ref, dm_ref, between=between)
    _merge(z_ref, og_ref, or_ref, merged_ref)


def _mlp_mixer_sample_kernel(x_ref, wup_ref, wdown_ref, gpre_ref, gpost_ref,
                             z_ref, la_ref, cos_ref, sin_ref, dm_ref, sgi_ref, sri_ref, ggla_ref, gret_ref,
                             y_ref, merged_ref, sgo_ref, sro_ref, og_ref, or_ref, *, nb, ls):
    pending = _mlp_halves(x_ref, wup_ref, wdown_ref, gpre_ref, gpost_ref, y_ref)
    _mixer_sample_body(z_ref, la_ref, cos_ref, sin_ref, dm_ref, sgi_ref, sri_ref, ggla_ref, gret_ref,
                       merged_ref, sgo_ref, sro_ref, og_ref, or_ref, nb, ls, between=pending)
    while pending:
        _run_next(pending)


def _post_mix_sample_kernel(a_ref, res_ref, wmix_ref, gpost_ref, gpre_xa_ref, wq_ref, x1_ref, q_ref):
    x1 = res_ref[...] + _rms(_mm(a_ref[...], wmix_ref[...]), gpost_ref[...])
    x1_ref[...] = x1
    q_ref[...] = _mm(_rms(x1, gpre_xa_ref[...]).astype(BF16), wq_ref[...])


def _post_xa_mlp_sample_kernel(o_ref, x1_ref, wo_ref, gpost_xa_ref, wup_ref, wdown_ref, gpre_ref, gpost_ref,
                               y_ref, x2_ref):
    x2_ref[...] = x1_ref[...] + _rms(_mm(o_ref[...], wo_ref[...]), gpost_xa_ref[...])
    _mlp_kernel(x2_ref, wup_ref, wdown_ref, gpre_ref, gpost_ref, y_ref)


def _xattn_sample_kernel(q_ref, k_ref, v_ref, o_ref, *, nb, ls):
    n_rows = N_HEADS * ls
    n_cols = N_MEM * N_HEADS
    row_head = lax.broadcasted_iota(jnp.int32, (n_rows, n_cols), 0) // ls
    col_head = lax.broadcasted_iota(jnp.int32, (n_rows, n_cols), 1) & (N_HEADS - 1)
    own = row_head == col_head

    def seq_body(e, carry):
        rows = pl.ds(pl.multiple_of(e * ls, ls), ls)
        q = q_ref[rows, :]
        q_hm = jnp.concatenate([q[:, h * XA_DH:(h + 1) * XA_DH] for h in range(N_HEADS)], axis=0)
        k_all = k_ref[e].reshape(n_cols, XA_DH)
        v_all = v_ref[e].reshape(n_cols, XA_DH)
        s = _mm_nt(q_hm.astype(BF16), k_all.astype(BF16)) * (XA_DH ** -0.5)
        s = jnp.where(own, s, -jnp.inf)
        p = jnp.exp(s - jnp.max(s, axis=-1, keepdims=True))
        p = p * (1.0 / jnp.sum(p, axis=-1, keepdims=True))
        o = _mm(p.astype(BF16), v_all.astype(BF16))
        for h in range(N_HEADS):
            o_ref[rows, h * XA_DH:(h + 1) * XA_DH] = o[h * ls:(h + 1) * ls, :].astype(o_ref.dtype)
        return carry

    lax.fori_loop(0, nb, seq_body, 0, unroll=4)


def _rope_tables(pos):
    half = D_K // 2
    inv = ROPE_BASE ** (-jnp.arange(half, dtype=F32) / half)
    ang = pos.astype(F32)[:, None] * inv[None, :]
    cos, sin = jnp.cos(ang), jnp.sin(ang)
    return jnp.concatenate([cos, cos], axis=-1), jnp.concatenate([-sin, sin], axis=-1)


def _decay_masks(n, c):
    t = jnp.arange(n, dtype=jnp.int32)
    keep = ((t[:, None] // c) == (t[None, :] // c)) & (t[:, None] >= t[None, :])
    dist = (t[:, None] - t[None, :]).astype(F32)
    lg = jnp.asarray(LOG_GAMMA, F32)[:, None, None]
    return jnp.where(keep[None], jnp.exp(lg * dist[None]), 0.0)


def _row_spec(tm, n):
    return pl.BlockSpec((tm, n), lambda i: (i, 0))


def kernel(x_prompt, x_sample, state_gla, state_ret, cache_mem_k, cache_mem_v, mem_prompt, w_in, w_gla_a2, b_gla_a, g_gla_head, g_ret_head, w_mix_out, w_xq, w_xk, w_xv, w_xo, g_mem, w_up, w_down, g_pre_mix, g_post_mix, g_pre_xa, g_post_xa, g_pre_ffn, g_post_ffn):
    depth = w_in.shape[0]
    assert depth == 1
    bp, lp, _ = x_prompt.shape
    bs, ls, _ = x_sample.shape
    tl = 256
    tm = 512
    ts_tile = 256
    nb_xa = 8
    assert lp % (2 * tl) == 0 and tl % CHUNK == 0 and lp % tm == 0
    n_mlp_steps = bp * lp // tm
    assert bs % n_mlp_steps == 0
    nb_mix = bs // n_mlp_steps
    assert (bs * ls) % ts_tile == 0 and bs % nb_xa == 0 and (nb_mix * ls) % 16 == 0
    assert ls % 8 == 0 and ls <= CHUNK and ls & (ls - 1) == 0 and tl & (tl - 1) == 0
    assert (bp * N_MEM) % tm == 0

    w = w_in[0]
    c_low = 2 * N_QK + 2 * D_MODEL
    assert c_low == N_WA and w.shape[1] == N_MAIN + GATE_RANK
    wt = jnp.transpose(w)
    n_wa_blocks = N_WA // PROJ_BLOCK

    def wt_row(i):
        units = PROJ_BLOCK // GATE_RANK
        return GATE_RANK * jnp.where(i < n_wa_blocks, i * units, i * units + 1)

    wmain, wga = pl.pallas_call(
        _split_w_in_kernel,
        grid=(N_MAIN // PROJ_BLOCK,),
        in_specs=[pl.BlockSpec((pl.Element(PROJ_BLOCK), pl.Element(D_MODEL)), lambda i: (wt_row(i), 0)),
                  pl.BlockSpec((GATE_RANK, D_MODEL), lambda i: (N_WA // GATE_RANK, 0))],
        out_specs=[pl.BlockSpec((D_MODEL, PROJ_BLOCK), lambda i: (0, i)),
                   pl.BlockSpec((D_MODEL, GATE_RANK_PAD), lambda i: (0, 0))],
        out_shape=[jax.ShapeDtypeStruct((D_MODEL, N_MAIN), BF16),
                   jax.ShapeDtypeStruct((D_MODEL, GATE_RANK_PAD), BF16)],
        compiler_params=_params(("arbitrary",)),
        name="split_w_in",
    )(wt, wt)
    w2 = jnp.pad(w_gla_a2[0], ((0, GATE_RANK_PAD - GATE_RANK), (0, 0))).astype(BF16)
    b2 = b_gla_a[0].reshape(1, N_QK)
    ggla = g_gla_head[0].reshape(1, D_MODEL)
    gret = g_ret_head[0].reshape(1, D_MODEL)
    wmix = w_mix_out[0].astype(BF16)
    wxq, wxk, wxv, wxo = (t[0].astype(BF16) for t in (w_xq, w_xk, w_xv, w_xo))
    wup, wdown = w_up[0].astype(BF16), w_down[0].astype(BF16)
    row = lambda g: g[0].reshape(1, D_MODEL)
    gmem, gpre_mix, gpost_mix, gpre_xa, gpost_xa, gpre_ffn, gpost_ffn = (
        row(g) for g in (g_mem, g_pre_mix, g_post_mix, g_pre_xa, g_post_xa, g_pre_ffn, g_post_ffn))
    cos_p, sin_p = _rope_tables(jnp.arange(lp, dtype=jnp.int32))
    cos_s, sin_s = _rope_tables(PAST_LEN + jnp.arange(ls, dtype=jnp.int32))
    rows_mix = nb_mix * ls
    cos_s, sin_s = jnp.tile(cos_s, (nb_mix, 1)), jnp.tile(sin_s, (nb_mix, 1))
    dm_p = _decay_masks(tl, tl)
    dm_s = _decay_masks(rows_mix, ls)

    nmem_rows = bp * N_MEM
    mk, mv, mk_bf, mv_bf = pl.pallas_call(
        _memkv_kernel,
        grid=(nmem_rows // tm,),
        in_specs=[_row_spec(tm, D_MODEL), _const_spec((1, D_MODEL)),
                  _const_spec((D_MODEL, D_MODEL)), _const_spec((D_MODEL, D_MODEL))],
        out_specs=[pl.BlockSpec((tm, N_HEADS, XA_DH), lambda i: (i, 0, 0))] * 2 + [_row_spec(tm, D_MODEL)] * 2,
        out_shape=[jax.ShapeDtypeStruct((nmem_rows, N_HEADS, XA_DH), F32)] * 2
        + [jax.ShapeDtypeStruct((nmem_rows, D_MODEL), BF16)] * 2,
        compiler_params=_params(("arbitrary",)),
        name="memkv",
    )(mem_prompt.reshape(nmem_rows, D_MODEL), gmem, wxk, wxv)

    n_tiles = bp * lp // tl
    steps_per_row = lp // (2 * tl)
    state_spec = pl.BlockSpec((1, N_HEADS, D_K, D_V), lambda s: (s // steps_per_row, 0, 0, 0))
    rope_spec = pl.BlockSpec((2 * tl, D_K), lambda s: (s % steps_per_row, 0))
    x1p, sg_p, sr_p = pl.pallas_call(
        functools.partial(_mixer_prompt_kernel, tl=tl, steps_per_row=steps_per_row),
        grid=(n_tiles // 2,),
        in_specs=[_row_spec(2 * tl, D_MODEL),
                  pl.BlockSpec((tl, D_MODEL), lambda s: (jnp.minimum(2 * s + 2, n_tiles - 1), 0)),
                  rope_spec, rope_spec,
                  _const_spec((N_HEADS, tl, tl)),
                  _const_spec((D_MODEL, N_MAIN)), _const_spec((D_MODEL, GATE_RANK_PAD)),
                  _const_spec((GATE_RANK_PAD, N_QK)), _const_spec((1, N_QK)),
                  _const_spec((1, D_MODEL)), _const_spec((1, D_MODEL)),
                  _const_spec((D_MODEL, D_MODEL)), _const_spec((1, D_MODEL)), _const_spec((1, D_MODEL))],
        out_specs=[_row_spec(2 * tl, D_MODEL), state_spec, state_spec],
        out_shape=[jax.ShapeDtypeStruct((bp * lp, D_MODEL), F32),
                   jax.ShapeDtypeStruct((bp, N_HEADS, D_K, D_V), F32),
                   jax.ShapeDtypeStruct((bp, N_HEADS, D_K, D_V), F32)],
        scratch_shapes=[pltpu.VMEM((2, tl, N_MAIN), F32), pltpu.VMEM((2, tl, N_QK), F32),
                        pltpu.VMEM((tl, D_MODEL), F32), pltpu.VMEM((tl, D_MODEL), F32),
                        pltpu.VMEM((tl, D_MODEL), BF16)],
        compiler_params=_params(("arbitrary",)),
        name="mixer_prompt",
    )(x_prompt.reshape(bp * lp, D_MODEL), x_prompt.reshape(bp * lp, D_MODEL), cos_p, sin_p, dm_p,
      wmain, wga, w2, b2, ggla, gret, wmix, gpre_mix, gpost_mix)
    x1p = x1p.reshape(bp, lp, D_MODEL)

    xa_per_row = lp // tm
    mem_spec = pl.BlockSpec((1, N_MEM, D_MODEL), lambda s: (s // xa_per_row, 0, 0))
    x2p = pl.pallas_call(
        _xattn_prompt_kernel,
        grid=(bp * lp // tm,),
        in_specs=[_row_spec(tm, D_MODEL), mem_spec, mem_spec,
                  _const_spec((D_MODEL, D_MODEL)), _const_spec((D_MODEL, D_MODEL)),
                  _const_spec((1, D_MODEL)), _const_spec((1, D_MODEL))],
        out_specs=_row_spec(tm, D_MODEL),
        out_shape=jax.ShapeDtypeStruct((bp * lp, D_MODEL), F32),
        scratch_shapes=[pltpu.VMEM((tm, D_MODEL), BF16)],
        compiler_params=_params(("arbitrary",)),
        name="xattn_prompt",
    )(x1p.reshape(bp * lp, D_MODEL), mk_bf.reshape(bp, N_MEM, D_MODEL), mv_bf.reshape(bp, N_MEM, D_MODEL),
      wxq, wxo, gpre_xa, gpost_xa)

    ts = bs * ls
    xs = x_sample.reshape(ts, D_MODEL)
    z_s, la_s = pl.pallas_call(
        _inproj_kernel,
        grid=(ts // ts_tile,),
        in_specs=[_row_spec(ts_tile, D_MODEL), _const_spec((D_MODEL, N_MAIN)),
                  _const_spec((D_MODEL, GATE_RANK_PAD)),
                  _const_spec((GATE_RANK_PAD, N_QK)), _const_spec((1, N_QK)), _const_spec((1, D_MODEL))],
        out_specs=[_row_spec(ts_tile, N_MAIN), _row_spec(ts_tile, N_QK)],
        out_shape=[jax.ShapeDtypeStruct((ts, N_MAIN), F32), jax.ShapeDtypeStruct((ts, N_QK), F32)],
        compiler_params=_params(("arbitrary",)),
        name="inproj_sample",
    )(xs, wmain, wga, w2, b2, gpre_mix)

    st_spec = pl.BlockSpec((nb_mix, N_HEADS, D_K, D_V), lambda i: (i, 0, 0, 0))
    yp, merged_s, sg_s, sr_s = pl.pallas_call(
        functools.partial(_mlp_mixer_sample_kernel, nb=nb_mix, ls=ls),
        grid=(n_mlp_steps,),
        in_specs=[_row_spec(tm, D_MODEL), _const_spec((D_MODEL, D_FF)), _const_spec((D_FF, D_MODEL)),
                  _const_spec((1, D_MODEL)), _const_spec((1, D_MODEL)),
                  _row_spec(rows_mix, N_MAIN), _row_spec(rows_mix, N_QK),
                  _const_spec((rows_mix, D_K)), _const_spec((rows_mix, D_K)),
                  _const_spec((N_HEADS, rows_mix, rows_mix)), st_spec, st_spec,
                  _const_spec((1, D_MODEL)), _const_spec((1, D_MODEL))],
        out_specs=[_row_spec(tm, D_MODEL), _row_spec(rows_mix, D_MODEL), st_spec, st_spec],
        out_shape=[jax.ShapeDtypeStruct((bp * lp, D_MODEL), F32),
                   jax.ShapeDtypeStruct((ts, D_MODEL), BF16),
                   jax.ShapeDtypeStruct((bs, N_HEADS, D_K, D_V), F32),
                   jax.ShapeDtypeStruct((bs, N_HEADS, D_K, D_V), F32)],
        scratch_shapes=[pltpu.VMEM((rows_mix, D_MODEL), F32), pltpu.VMEM((rows_mix, D_MODEL), F32)],
        compiler_params=_params(("arbitrary",)),
        name="mlp_prompt_mixer_sample",
    )(x2p, wup, wdown, gpre_ffn, gpost_ffn,
      z_s, la_s, cos_s, sin_s, dm_s, state_gla[0], state_ret[0], ggla, gret)
    yp = yp.reshape(bp, lp, D_MODEL)

    x1s, q_s = pl.pallas_call(
        _post_mix_sample_kernel,
        grid=(ts // ts_tile,),
        in_specs=[_row_spec(ts_tile, D_MODEL), _row_spec(ts_tile, D_MODEL), _const_spec((D_MODEL, D_MODEL)),
                  _const_spec((1, D_MODEL)), _const_spec((1, D_MODEL)), _const_spec((D_MODEL, D_MODEL))],
        out_specs=[_row_spec(ts_tile, D_MODEL)] * 2,
        out_shape=[jax.ShapeDtypeStruct((ts, D_MODEL), F32)] * 2,
        compiler_params=_params(("arbitrary",)),
        name="post_mix_sample",
    )(merged_s, xs, wmix, gpost_mix, gpre_xa, wxq)

    rows_xa = nb_xa * ls
    kv_spec = pl.BlockSpec((nb_xa, N_MEM, N_HEADS, XA_DH), lambda i: (i, 0, 0, 0))
    o_s = pl.pallas_call(
        functools.partial(_xattn_sample_kernel, nb=nb_xa, ls=ls),
        grid=(bs // nb_xa,),
        in_specs=[_row_spec(rows_xa, D_MODEL), kv_spec, kv_spec],
        out_specs=_row_spec(rows_xa, D_MODEL),
        out_shape=jax.ShapeDtypeStruct((ts, D_MODEL), BF16),
        compiler_params=_params(("arbitrary",)),
        name="xattn_sample",
    )(q_s, cache_mem_k[0], cache_mem_v[0])

    ys = pl.pallas_call(
        _post_xa_mlp_sample_kernel,
        grid=(ts // ts_tile,),
        in_specs=[_row_spec(ts_tile, D_MODEL), _row_spec(ts_tile, D_MODEL), _const_spec((D_MODEL, D_MODEL)),
                  _const_spec((1, D_MODEL)), _const_spec((D_MODEL, D_FF)), _const_spec((D_FF, D_MODEL)),
                  _const_spec((1, D_MODEL)), _const_spec((1, D_MODEL))],
        out_specs=_row_spec(ts_tile, D_MODEL),
        out_shape=jax.ShapeDtypeStruct((ts, D_MODEL), F32),
        scratch_shapes=[pltpu.VMEM((ts_tile, D_MODEL), F32)],
        compiler_params=_params(("arbitrary",)),
        name="post_xa_mlp_sample",
    )(o_s, x1s, wxo, gpost_xa, wup, wdown, gpre_ffn, gpost_ffn).reshape(bs, ls, D_MODEL)

    hshape = (1, bp, N_MEM, N_HEADS, XA_DH)
    return (yp, ys, sg_p[None], sr_p[None], mk.reshape(hshape), mv.reshape(hshape), sg_s[None], sr_s[None])
```

```python
import functools
import math

import jax
import jax.numpy as jnp
from jax import lax
from jax.experimental import pallas as pl
from jax.experimental.pallas import tpu as pltpu

F32 = jnp.float32
BF16 = jnp.bfloat16

D_MODEL = 1024
N_HEADS = 4
D_K = 128
D_V = 256
N_MEM = 256
XA_DH = 256
D_FF = 4 * D_MODEL
GATE_RANK = 16
GATE_RANK_PAD = 128
GLA_TAU = 16.0
CHUNK = 64
ROPE_BASE = 10000.0
PAST_LEN = 16384
EPS = 1e-6
LOG_GAMMA = tuple(math.log1p(-(2.0 ** (-5.0 - h))) for h in range(N_HEADS))

C_GQ, C_GK, C_GV, C_GR = 0, 512, 1024, 2048
C_RQ, C_RK, C_RV, C_RG = 3072, 3584, 4096, 5120
C_GA, C_GB = 6144, 7168
N_MAIN = 8192
N_WA = 3072
N_WB = N_MAIN - N_WA
N_QK = N_HEADS * D_K
PROJ_BLOCK = 512
MLP_BLOCK = 1024

LANE = 128
VMEM_LIMIT_BYTES = 60 * 1024 * 1024


def _mm(a, b):
    return jnp.dot(a, b, preferred_element_type=F32)


def _mm_nt(a, b):
    return lax.dot_general(a, b, (((1,), (1,)), ((), ())), preferred_element_type=F32)


def _mm_tn(a, b):
    return lax.dot_general(a, b, (((0,), (0,)), ((), ())), preferred_element_type=F32)


def _rms(x, g):
    return x * lax.rsqrt(jnp.mean(x * x, axis=-1, keepdims=True) + EPS) * g


def _sigmoid(x):
    return 1.0 / (1.0 + jnp.exp(-x))


def _const_spec(shape):
    nd = len(shape)
    return pl.BlockSpec(shape, lambda *_: (0,) * nd, pipeline_mode=pl.Buffered(1))


def _params(sem):
    return pltpu.CompilerParams(dimension_semantics=sem, vmem_limit_bytes=VMEM_LIMIT_BYTES)


def _inproj_pieces(h_bf, wmain_ref, wga_ref, w2_ref, b2_ref, z_ref, la_ref):
    def z_block(j):
        def run():
            z_ref[:, j:j + PROJ_BLOCK] = _mm(h_bf, wmain_ref[:, j:j + PROJ_BLOCK])
        return run

    def gate():
        ga = _mm(h_bf, wga_ref[...])
        xg = _mm(ga.astype(BF16), w2_ref[...]) + b2_ref[...]
        log_sig = jnp.minimum(xg, 0.0) - jnp.log1p(jnp.exp(-jnp.abs(xg)))
        la_ref[...] = log_sig * (1.0 / GLA_TAU)

    return [z_block(j) for j in range(0, N_MAIN, PROJ_BLOCK)] + [gate]


def _inproj(h_bf, wmain_ref, wga_ref, w2_ref, b2_ref, z_ref, la_ref):
    for piece in _inproj_pieces(h_bf, wmain_ref, wga_ref, w2_ref, b2_ref, z_ref, la_ref):
        piece()


def _run_next(pieces):
    if pieces:
        pieces.pop(0)()


def _block_causal(n, c):
    shift = c.bit_length() - 1
    row = lax.broadcasted_iota(jnp.int32, (n, n), 0)
    col = lax.broadcasted_iota(jnp.int32, (n, n), 1)
    return ((row >> shift) == (col >> shift)) & (row >= col)


def _gla_tile(z_ref, la_ref, n, c, s_get, s_put, g_ref, o_ref, between=()):
    causal = _block_causal(n, c)
    tri = jnp.where(causal, 1.0, 0.0).astype(BF16)
    la = la_ref[...]
    la_hi = la.astype(BF16)
    la_lo = (la - la_hi.astype(F32)).astype(BF16)
    b = _mm(tri, la_hi) + _mm(tri, la_lo)
    n_chunks = n // c
    b_last = [b[(ci + 1) * c - 1:(ci + 1) * c, :] for ci in range(n_chunks)]
    b_last_full = jnp.concatenate([jnp.broadcast_to(bl, (c, N_QK)) for bl in b_last], axis=0)
    eb = jnp.exp(b)
    enb = jnp.exp(-b)
    ekd = jnp.exp(b_last_full - b)
    for h in range(N_HEADS):
        ks = slice(h * D_K, (h + 1) * D_K)
        q = z_ref[:, C_GQ + h * D_K:C_GQ + (h + 1) * D_K] * (D_K ** -0.5)
        k = z_ref[:, C_GK + h * D_K:C_GK + (h + 1) * D_K]
        v = z_ref[:, C_GV + h * D_V:C_GV + (h + 1) * D_V].astype(BF16)
        qe = (q * eb[:, ks]).astype(BF16)
        ke = (k * enb[:, ks]).astype(BF16)
        kd = (k * ekd[:, ks]).astype(BF16)
        sc = jnp.where(causal, _mm_nt(qe, ke), 0.0).astype(BF16)
        o_intra = _mm(sc, v)
        _run_next(between)
        outs = []
        for ci in range(n_chunks):
            r = slice(ci * c, (ci + 1) * c)
            s_old = s_get(ci, h)
            outs.append(o_intra[r] + _mm(qe[r], s_old.astype(BF16)))
            dcol = jnp.transpose(jnp.broadcast_to(jnp.exp(b_last[ci][:, ks]), (D_K, D_K)))
            s_put(ci, h, jnp.concatenate([dcol, dcol], axis=1) * s_old + _mm_tn(kd[r], v[r]))
        o = jnp.concatenate(outs, axis=0) if n_chunks > 1 else outs[0]
        ms = jnp.mean(o * o, axis=-1, keepdims=True)
        o_ref[:, h * D_V:(h + 1) * D_V] = o * lax.rsqrt(ms + EPS) * g_ref[:, h * D_V:(h + 1) * D_V]
        _run_next(between)


def _ret_tile(z_ref, cos, sin, n, c, s_get, s_put, g_ref, o_ref, dm_ref, between=()):
    tpos = (lax.broadcasted_iota(jnp.int32, (n, 1), 0) & (c - 1)).astype(F32)
    n_chunks = n // c
    for h in range(N_HEADS):
        lg = LOG_GAMMA[h]
        q = z_ref[:, C_RQ + h * D_K:C_RQ + (h + 1) * D_K]
        k = z_ref[:, C_RK + h * D_K:C_RK + (h + 1) * D_K]
        v = z_ref[:, C_RV + h * D_V:C_RV + (h + 1) * D_V].astype(BF16)
        q = q * cos + pltpu.roll(q, D_K // 2, 1) * sin
        k = (k * cos + pltpu.roll(k, D_K // 2, 1) * sin) * (D_K ** -0.5)
        sc = (_mm_nt(q.astype(BF16), k.astype(BF16)) * dm_ref[h]).astype(BF16)
        o_intra = _mm(sc, v)
        _run_next(between)
        qd = (q * jnp.exp(lg * (tpos + 1.0))).astype(BF16)
        kd = (k * jnp.exp(lg * (float(c - 1) - tpos))).astype(BF16)
        outs = []
        for ci in range(n_chunks):
            r = slice(ci * c, (ci + 1) * c)
            s_old = s_get(ci, h)
            outs.append(o_intra[r] + _mm(qd[r], s_old.astype(BF16)))
            s_put(ci, h, math.exp(lg * c) * s_old + _mm_tn(kd[r], v[r]))
        o = jnp.concatenate(outs, axis=0) if n_chunks > 1 else outs[0]
        mu = jnp.mean(o, axis=-1, keepdims=True)
        oc = o - mu
        var = jnp.mean(oc * oc, axis=-1, keepdims=True)
        o_ref[:, h * D_V:(h + 1) * D_V] = oc * lax.rsqrt(var + EPS) * g_ref[:, h * D_V:(h + 1) * D_V]
        _run_next(between)


def _merge(z_ref, og_ref, or_ref, merged_ref):
    for j in range(0, D_MODEL, 256):
        cs = slice(j, j + 256)
        gr = z_ref[:, C_GR + j:C_GR + j + 256]
        rg = z_ref[:, C_RG + j:C_RG + j + 256]
        ga = z_ref[:, C_GA + j:C_GA + j + 256]
        gb = z_ref[:, C_GB + j:C_GB + j + 256]
        o_g = og_ref[:, cs] * (gr * _sigmoid(gr))
        o_r = or_ref[:, cs] * (rg * _sigmoid(rg))
        merged_ref[:, cs] = (_sigmoid(ga) * o_g + _sigmoid(gb) * o_r).astype(merged_ref.dtype)


def _split_w_in_kernel(wt_ref, wtg_ref, w_ref, wga_ref):
    w_ref[...] = jnp.transpose(wt_ref[...]).astype(BF16)
    zeros = jnp.zeros((GATE_RANK_PAD - GATE_RANK, D_MODEL), F32)
    wga_ref[...] = jnp.transpose(jnp.concatenate([wtg_ref[...], zeros], axis=0)).astype(BF16)


def _memkv_kernel(m_ref, g_ref, wk_ref, wv_ref, k_ref, v_ref, kb_ref, vb_ref):
    m = _rms(m_ref[...], g_ref[...]).astype(BF16)
    k = _mm(m, wk_ref[...])
    v = _mm(m, wv_ref[...])
    k_ref[...] = k.reshape(k_ref.shape)
    v_ref[...] = v.reshape(v_ref.shape)
    kb_ref[...] = k.astype(BF16)
    vb_ref[...] = v.astype(BF16)


def _mixer_prompt_kernel(xc_ref, xn_ref, cos_ref, sin_ref, dm_ref, wmain_ref, wga_ref, w2_ref, b2_ref,
                         ggla_ref, gret_ref, wmix_ref, gpre_ref, gpost_ref,
                         y_ref, sg_ref, sr_ref,
                         z_ref, la_ref, og_ref, or_ref, merged_ref, *, tl, steps_per_row):
    step = pl.program_id(0)

    @pl.when(step % steps_per_row == 0)
    def _():
        sg_ref[...] = jnp.zeros_like(sg_ref)
        sr_ref[...] = jnp.zeros_like(sr_ref)

    def project_pieces(x, slot):
        h_bf = _rms(x, gpre_ref[...]).astype(BF16)
        return _inproj_pieces(h_bf, wmain_ref, wga_ref, w2_ref, b2_ref, z_ref.at[slot], la_ref.at[slot])

    @pl.when(step == 0)
    def _():
        for piece in project_pieces(xc_ref[0:tl, :], 0):
            piece()

    n_chunks = tl // CHUNK
    for half in range(2):
        rows = slice(half * tl, (half + 1) * tl)
        pending = project_pieces(xc_ref[tl:2 * tl, :] if half == 0 else xn_ref[...], 1 - half)
        _run_next(pending)
        z_cur, la_cur = z_ref.at[half], la_ref.at[half]

        carried = {}

        def gla_get(ci, h, carried=carried):
            return sg_ref[0, h] if ci == 0 else carried[h]

        def gla_put(ci, h, val, carried=carried):
            carried[h] = val
            if ci == n_chunks - 1:
                sg_ref[0, h] = val

        _gla_tile(z_cur, la_cur, tl, CHUNK, gla_get, gla_put, ggla_ref, og_ref, between=pending)

        def ret_put(ci, h, val):
            sr_ref[0, h] = val

        _ret_tile(z_cur, cos_ref[rows, :], sin_ref[rows, :], tl, tl, lambda ci, h: sr_ref[0, h], ret_put,
                  gret_ref, or_ref, dm_ref, between=pending)
        while pending:
            _run_next(pending)

        _merge(z_cur, og_ref, or_ref, merged_ref)
        m = _mm(merged_ref[...], wmix_ref[...])
        y_ref[rows, :] = xc_ref[rows, :] + _rms(m, gpost_ref[...])


def _staggered(a_stages, b_stages):
    a, b = list(a_stages), list(b_stages)
    order = [a.pop(0)]
    while a or b:
        order += a[:1] + b[:1]
        a, b = a[1:], b[1:]
    return order


def _xattn_stages(x_ref, rows, mk_ref, mv_ref, wq_ref, wo_ref, gpre_ref, gpost_ref, y_ref, o_ref):
    st = {}

    def norm():
        st["hx"] = _rms(x_ref[rows, :], gpre_ref[...]).astype(BF16)

    def q_proj():
        st["q"] = _mm(st["hx"], wq_ref[...])

    def heads():
        for h in range(N_HEADS):
            hs = slice(h * XA_DH, (h + 1) * XA_DH)
            s = _mm_nt(st["q"][:, hs].astype(BF16), mk_ref[0, :, hs]) * (XA_DH ** -0.5)
            p = jnp.exp(s - jnp.max(s, axis=-1, keepdims=True))
            p = p * (1.0 / jnp.sum(p, axis=-1, keepdims=True))
            o_ref[rows, hs] = _mm(p.astype(BF16), mv_ref[0, :, hs]).astype(o_ref.dtype)

    def out_proj():
        st["a"] = _mm(o_ref[rows, :], wo_ref[...])

    def post():
        y_ref[rows, :] = x_ref[rows, :] + _rms(st["a"], gpost_ref[...])

    return [norm, q_proj, heads, out_proj, post]


def _mlp_stages(x_ref, rows, wup_ref, wdown_ref, gpre_ref, gpost_ref, y_ref):
    st = {}

    def norm():
        st["h"] = _rms(x_ref[rows, :], gpre_ref[...]).astype(BF16)

    def hidden_block(j):
        def run():
            u = jnp.maximum(_mm(st["h"], wup_ref[:, j:j + MLP_BLOCK]), 0.0)
            part = _mm((u * u).astype(BF16), wdown_ref[j:j + MLP_BLOCK, :])
            st["acc"] = part if j == 0 else st["acc"] + part
        return run

    def post():
        y_ref[rows, :] = x_ref[rows, :] + _rms(st["acc"], gpost_ref[...])

    return [norm] + [hidden_block(j) for j in range(0, D_FF, MLP_BLOCK)] + [post]


def _mlp_halves(x_ref, wup_ref, wdown_ref, gpre_ref, gpost_ref, y_ref):
    half = x_ref.shape[0] // 2
    args = (wup_ref, wdown_ref, gpre_ref, gpost_ref, y_ref)
    return _staggered(_mlp_stages(x_ref, slice(0, half), *args), _mlp_stages(x_ref, slice(half, 2 * half), *args))


def _mlp_kernel(x_ref, wup_ref, wdown_ref, gpre_ref, gpost_ref, y_ref):
    for stage in _mlp_halves(x_ref, wup_ref, wdown_ref, gpre_ref, gpost_ref, y_ref):
        stage()


def _inproj_kernel(x_ref, wmain_ref, wga_ref, w2_ref, b2_ref, gpre_ref, z_ref, la_ref):
    h_bf = _rms(x_ref[...], gpre_ref[...]).astype(BF16)
    _inproj(h_bf, wmain_ref, wga_ref, w2_ref, b2_ref, z_ref, la_ref)


def _mixer_sample_body(z_ref, la_ref, cos_ref, sin_ref, dm_ref, sgi_ref, sri_ref, ggla_ref, gret_ref,
                       merged_ref, sgo_ref, sro_ref, og_ref, or_ref, nb, ls, between=()):
    def gla_put(ci, h, val):
        sgo_ref[ci, h] = val

    def ret_put(ci, h, val):
        sro_ref[ci, h] = val

    _gla_tile(z_ref, la_ref, nb * ls, ls, lambda ci, h: sgi_ref[ci, h], gla_put, ggla_ref, og_ref, between=between)
    _ret_tile(z_ref, cos_ref[...], sin_ref[...], nb * ls, ls, lambda ci, h: sri_ref[ci, h], ret_put,
              gret_ref, or_ref, dm_ref, between=between)
    _merge(z_ref, og_ref, or_ref, merged_ref)


def _xattn_prompt_mixer_sample_kernel(x_ref, mk_ref, mv_ref, wq_ref, wo_ref, gpre_ref, gpost_ref,
                                      z_ref, la_ref, cos_ref, sin_ref, dm_ref, sgi_ref, sri_ref, ggla_ref, gret_ref,
                                      y_ref, merged_ref, sgo_ref, sro_ref, o_ref, og_ref, or_ref, *, nb, ls):
    half = x_ref.shape[0] // 2
    args = (mk_ref, mv_ref, wq_ref, wo_ref, gpre_ref, gpost_ref, y_ref, o_ref)
    pending = _staggered(_xattn_stages(x_ref, slice(0, half), *args),
                         _xattn_stages(x_ref, slice(half, 2 * half), *args))
    _mixer_sample_body(z_ref, la_ref, cos_ref, sin_ref, dm_ref, sgi_ref, sri_ref, ggla_ref, gret_ref,
                       merged_ref, sgo_ref, sro_ref, og_ref, or_ref, nb, ls, between=pending)
    while pending:
        _run_next(pending)


def _post_mix_sample_kernel(a_ref, res_ref, wmix_ref, gpost_ref, gpre_xa_ref, wq_ref, x1_ref, q_ref):
    x1 = res_ref[...] + _rms(_mm(a_ref[...], wmix_ref[...]), gpost_ref[...])
    x1_ref[...] = x1
    q_ref[...] = _mm(_rms(x1, gpre_xa_ref[...]).astype(BF16), wq_ref[...])


def _post_xa_mlp_sample_kernel(o_ref, x1_ref, wo_ref, gpost_xa_ref, wup_ref, wdown_ref, gpre_ref, gpost_ref,
                               y_ref, x2_ref):
    x2_ref[...] = x1_ref[...] + _rms(_mm(o_ref[...], wo_ref[...]), gpost_xa_ref[...])
    _mlp_kernel(x2_ref, wup_ref, wdown_ref, gpre_ref, gpost_ref, y_ref)


def _xattn_sample_body(q_ref, k_ref, v_ref, o_ref, nb, ls, between=()):
    n_rows = N_HEADS * ls
    n_cols = N_MEM * N_HEADS
    row_head = lax.broadcasted_iota(jnp.int32, (n_rows, n_cols), 0) // ls
    col_head = lax.broadcasted_iota(jnp.int32, (n_rows, n_cols), 1) & (N_HEADS - 1)
    own = row_head == col_head

    for e in range(nb):
        rows = slice(e * ls, (e + 1) * ls)
        q = q_ref[rows, :]
        q_hm = jnp.concatenate([q[:, h * XA_DH:(h + 1) * XA_DH] for h in range(N_HEADS)], axis=0)
        k_all = k_ref[e].reshape(n_cols, XA_DH)
        v_all = v_ref[e].reshape(n_cols, XA_DH)
        s = _mm_nt(q_hm.astype(BF16), k_all.astype(BF16)) * (XA_DH ** -0.5)
        s = jnp.where(own, s, -jnp.inf)
        p = jnp.exp(s - jnp.max(s, axis=-1, keepdims=True))
        p = p * (1.0 / jnp.sum(p, axis=-1, keepdims=True))
        _run_next(between)
        o = _mm(p.astype(BF16), v_all.astype(BF16))
        for h in range(N_HEADS):
            o_ref[rows, h * XA_DH:(h + 1) * XA_DH] = o[h * ls:(h + 1) * ls, :].astype(o_ref.dtype)
        _run_next(between)
        _run_next(between)


def _mlp_xattn_sample_kernel(x_ref, wup_ref, wdown_ref, gpre_ref, gpost_ref, q_ref, k_ref, v_ref,
                             y_ref, o_ref, *, nb, ls):
    pending = _mlp_halves(x_ref, wup_ref, wdown_ref, gpre_ref, gpost_ref, y_ref)
    _xattn_sample_body(q_ref, k_ref, v_ref, o_ref, nb, ls, between=pending)
    while pending:
        _run_next(pending)


def _rope_tables(pos):
    half = D_K // 2
    inv = ROPE_BASE ** (-jnp.arange(half, dtype=F32) / half)
    ang = pos.astype(F32)[:, None] * inv[None, :]
    cos, sin = jnp.cos(ang), jnp.sin(ang)
    return jnp.concatenate([cos, cos], axis=-1), jnp.concatenate([-sin, sin], axis=-1)


def _decay_masks(n, c):
    t = jnp.arange(n, dtype=jnp.int32)
    keep = ((t[:, None] // c) == (t[None, :] // c)) & (t[:, None] >= t[None, :])
    dist = (t[:, None] - t[None, :]).astype(F32)
    lg = jnp.asarray(LOG_GAMMA, F32)[:, None, None]
    return jnp.where(keep[None], jnp.exp(lg * dist[None]), 0.0)


def _row_spec(tm, n):
    return pl.BlockSpec((tm, n), lambda i: (i, 0))


def kernel(x_prompt, x_sample, state_gla, state_ret, cache_mem_k, cache_mem_v, mem_prompt, w_in, w_gla_a2, b_gla_a, g_gla_head, g_ret_head, w_mix_out, w_xq, w_xk, w_xv, w_xo, g_mem, w_up, w_down, g_pre_mix, g_post_mix, g_pre_xa, g_post_xa, g_pre_ffn, g_post_ffn):
    depth = w_in.shape[0]
    assert depth == 1
    bp, lp, _ = x_prompt.shape
    bs, ls, _ = x_sample.shape
    tl = 256
    tm = 512
    ts_tile = 256
    assert lp % (2 * tl) == 0 and tl % CHUNK == 0 and lp % tm == 0
    n_mlp_steps = bp * lp // tm
    assert bs % n_mlp_steps == 0
    nb_mix = bs // n_mlp_steps
    assert (bs * ls) % ts_tile == 0 and (nb_mix * ls) % 16 == 0
    assert ls % 8 == 0 and ls <= CHUNK and ls & (ls - 1) == 0 and tl & (tl - 1) == 0
    assert (bp * N_MEM) % tm == 0

    w = w_in[0]
    c_low = 2 * N_QK + 2 * D_MODEL
    assert c_low == N_WA and w.shape[1] == N_MAIN + GATE_RANK
    wt = jnp.transpose(w)
    n_wa_blocks = N_WA // PROJ_BLOCK

    def wt_row(i):
        units = PROJ_BLOCK // GATE_RANK
        return GATE_RANK * jnp.where(i < n_wa_blocks, i * units, i * units + 1)

    wmain, wga = pl.pallas_call(
        _split_w_in_kernel,
        grid=(N_MAIN // PROJ_BLOCK,),
        in_specs=[pl.BlockSpec((pl.Element(PROJ_BLOCK), pl.Element(D_MODEL)), lambda i: (wt_row(i), 0)),
                  pl.BlockSpec((GATE_RANK, D_MODEL), lambda i: (N_WA // GATE_RANK, 0))],
        out_specs=[pl.BlockSpec((D_MODEL, PROJ_BLOCK), lambda i: (0, i)),
                   pl.BlockSpec((D_MODEL, GATE_RANK_PAD), lambda i: (0, 0))],
        out_shape=[jax.ShapeDtypeStruct((D_MODEL, N_MAIN), BF16),
                   jax.ShapeDtypeStruct((D_MODEL, GATE_RANK_PAD), BF16)],
        compiler_params=_params(("arbitrary",)),
        name="split_w_in",
    )(wt, wt)
    w2 = jnp.pad(w_gla_a2[0], ((0, GATE_RANK_PAD - GATE_RANK), (0, 0))).astype(BF16)
    b2 = b_gla_a[0].reshape(1, N_QK)
    ggla = g_gla_head[0].reshape(1, D_MODEL)
    gret = g_ret_head[0].reshape(1, D_MODEL)
    wmix = w_mix_out[0].astype(BF16)
    wxq, wxk, wxv, wxo = (t[0].astype(BF16) for t in (w_xq, w_xk, w_xv, w_xo))
    wup, wdown = w_up[0].astype(BF16), w_down[0].astype(BF16)
    row = lambda g: g[0].reshape(1, D_MODEL)
    gmem, gpre_mix, gpost_mix, gpre_xa, gpost_xa, gpre_ffn, gpost_ffn = (
        row(g) for g in (g_mem, g_pre_mix, g_post_mix, g_pre_xa, g_post_xa, g_pre_ffn, g_post_ffn))
    cos_p, sin_p = _rope_tables(jnp.arange(lp, dtype=jnp.int32))
    cos_s, sin_s = _rope_tables(PAST_LEN + jnp.arange(ls, dtype=jnp.int32))
    rows_mix = nb_mix * ls
    cos_s, sin_s = jnp.tile(cos_s, (nb_mix, 1)), jnp.tile(sin_s, (nb_mix, 1))
    dm_p = _decay_masks(tl, tl)
    dm_s = _decay_masks(rows_mix, ls)

    nmem_rows = bp * N_MEM
    mk, mv, mk_bf, mv_bf = pl.pallas_call(
        _memkv_kernel,
        grid=(nmem_rows // tm,),
        in_specs=[_row_spec(tm, D_MODEL), _const_spec((1, D_MODEL)),
                  _const_spec((D_MODEL, D_MODEL)), _const_spec((D_MODEL, D_MODEL))],
        out_specs=[pl.BlockSpec((tm, N_HEADS, XA_DH), lambda i: (i, 0, 0))] * 2 + [_row_spec(tm, D_MODEL)] * 2,
        out_shape=[jax.ShapeDtypeStruct((nmem_rows, N_HEADS, XA_DH), F32)] * 2
        + [jax.ShapeDtypeStruct((nmem_rows, D_MODEL), BF16)] * 2,
        compiler_params=_params(("arbitrary",)),
        name="memkv",
    )(mem_prompt.reshape(nmem_rows, D_MODEL), gmem, wxk, wxv)

    n_tiles = bp * lp // tl
    steps_per_row = lp // (2 * tl)
    state_spec = pl.BlockSpec((1, N_HEADS, D_K, D_V), lambda s: (s // steps_per_row, 0, 0, 0))
    rope_spec = pl.BlockSpec((2 * tl, D_K), lambda s: (s % steps_per_row, 0))
    x1p, sg_p, sr_p = pl.pallas_call(
        functools.partial(_mixer_prompt_kernel, tl=tl, steps_per_row=steps_per_row),
        grid=(n_tiles // 2,),
        in_specs=[_row_spec(2 * tl, D_MODEL),
                  pl.BlockSpec((tl, D_MODEL), lambda s: (jnp.minimum(2 * s + 2, n_tiles - 1), 0)),
                  rope_spec, rope_spec,
                  _const_spec((N_HEADS, tl, tl)),
                  _const_spec((D_MODEL, N_MAIN)), _const_spec((D_MODEL, GATE_RANK_PAD)),
                  _const_spec((GATE_RANK_PAD, N_QK)), _const_spec((1, N_QK)),
                  _const_spec((1, D_MODEL)), _const_spec((1, D_MODEL)),
                  _const_spec((D_MODEL, D_MODEL)), _const_spec((1, D_MODEL)), _const_spec((1, D_MODEL))],
        out_specs=[_row_spec(2 * tl, D_MODEL), state_spec, state_spec],
        out_shape=[jax.ShapeDtypeStruct((bp * lp, D_MODEL), F32),
                   jax.ShapeDtypeStruct((bp, N_HEADS, D_K, D_V), F32),
                   jax.ShapeDtypeStruct((bp, N_HEADS, D_K, D_V), F32)],
        scratch_shapes=[pltpu.VMEM((2, tl, N_MAIN), F32), pltpu.VMEM((2, tl, N_QK), F32),
                        pltpu.VMEM((tl, D_MODEL), F32), pltpu.VMEM((tl, D_MODEL), F32),
                        pltpu.VMEM((tl, D_MODEL), BF16)],
        compiler_params=_params(("arbitrary",)),
        name="mixer_prompt",
    )(x_prompt.reshape(bp * lp, D_MODEL), x_prompt.reshape(bp * lp, D_MODEL), cos_p, sin_p, dm_p,
      wmain, wga, w2, b2, ggla, gret, wmix, gpre_mix, gpost_mix)
    x1p = x1p.reshape(bp, lp, D_MODEL)

    ts = bs * ls
    xs = x_sample.reshape(ts, D_MODEL)
    z_s, la_s = pl.pallas_call(
        _inproj_kernel,
        grid=(ts // ts_tile,),
        in_specs=[_row_spec(ts_tile, D_MODEL), _const_spec((D_MODEL, N_MAIN)),
                  _const_spec((D_MODEL, GATE_RANK_PAD)),
                  _const_spec((GATE_RANK_PAD, N_QK)), _const_spec((1, N_QK)), _const_spec((1, D_MODEL))],
        out_specs=[_row_spec(ts_tile, N_MAIN), _row_spec(ts_tile, N_QK)],
        out_shape=[jax.ShapeDtypeStruct((ts, N_MAIN), F32), jax.ShapeDtypeStruct((ts, N_QK), F32)],
        compiler_params=_params(("arbitrary",)),
        name="inproj_sample",
    )(xs, wmain, wga, w2, b2, gpre_mix)

    xa_per_row = lp // tm
    mem_spec = pl.BlockSpec((1, N_MEM, D_MODEL), lambda s: (s // xa_per_row, 0, 0))
    st_spec = pl.BlockSpec((nb_mix, N_HEADS, D_K, D_V), lambda i: (i, 0, 0, 0))
    x2p, merged_s, sg_s, sr_s = pl.pallas_call(
        functools.partial(_xattn_prompt_mixer_sample_kernel, nb=nb_mix, ls=ls),
        grid=(n_mlp_steps,),
        in_specs=[_row_spec(tm, D_MODEL), mem_spec, mem_spec,
                  _const_spec((D_MODEL, D_MODEL)), _const_spec((D_MODEL, D_MODEL)),
                  _const_spec((1, D_MODEL)), _const_spec((1, D_MODEL)),
                  _row_spec(rows_mix, N_MAIN), _row_spec(rows_mix, N_QK),
                  _const_spec((rows_mix, D_K)), _const_spec((rows_mix, D_K)),
                  _const_spec((N_HEADS, rows_mix, rows_mix)), st_spec, st_spec,
                  _const_spec((1, D_MODEL)), _const_spec((1, D_MODEL))],
        out_specs=[_row_spec(tm, D_MODEL), _row_spec(rows_mix, D_MODEL), st_spec, st_spec],
        out_shape=[jax.ShapeDtypeStruct((bp * lp, D_MODEL), F32),
                   jax.ShapeDtypeStruct((ts, D_MODEL), BF16),
                   jax.ShapeDtypeStruct((bs, N_HEADS, D_K, D_V), F32),
                   jax.ShapeDtypeStruct((bs, N_HEADS, D_K, D_V), F32)],
        scratch_shapes=[pltpu.VMEM((tm, D_MODEL), BF16),
                        pltpu.VMEM((rows_mix, D_MODEL), F32), pltpu.VMEM((rows_mix, D_MODEL), F32)],
        compiler_params=_params(("arbitrary",)),
        name="xattn_prompt_mixer_sample",
    )(x1p.reshape(bp * lp, D_MODEL), mk_bf.reshape(bp, N_MEM, D_MODEL), mv_bf.reshape(bp, N_MEM, D_MODEL),
      wxq, wxo, gpre_xa, gpost_xa,
      z_s, la_s, cos_s, sin_s, dm_s, state_gla[0], state_ret[0], ggla, gret)

    x1s, q_s = pl.pallas_call(
        _post_mix_sample_kernel,
        grid=(ts // ts_tile,),
        in_specs=[_row_spec(ts_tile, D_MODEL), _row_spec(ts_tile, D_MODEL), _const_spec((D_MODEL, D_MODEL)),
                  _const_spec((1, D_MODEL)), _const_spec((1, D_MODEL)), _const_spec((D_MODEL, D_MODEL))],
        out_specs=[_row_spec(ts_tile, D_MODEL)] * 2,
        out_shape=[jax.ShapeDtypeStruct((ts, D_MODEL), F32)] * 2,
        compiler_params=_params(("arbitrary",)),
        name="post_mix_sample",
    )(merged_s, xs, wmix, gpost_mix, gpre_xa, wxq)

    kv_spec = pl.BlockSpec((nb_mix, N_MEM, N_HEADS, XA_DH), lambda i: (i, 0, 0, 0))
    yp, o_s = pl.pallas_call(
        functools.partial(_mlp_xattn_sample_kernel, nb=nb_mix, ls=ls),
        grid=(n_mlp_steps,),
        in_specs=[_row_spec(tm, D_MODEL), _const_spec((D_MODEL, D_FF)), _const_spec((D_FF, D_MODEL)),
                  _const_spec((1, D_MODEL)), _const_spec((1, D_MODEL)),
                  _row_spec(rows_mix, D_MODEL), kv_spec, kv_spec],
        out_specs=[_row_spec(tm, D_MODEL), _row_spec(rows_mix, D_MODEL)],
        out_shape=[jax.ShapeDtypeStruct((bp * lp, D_MODEL), F32), jax.ShapeDtypeStruct((ts, D_MODEL), BF16)],
        compiler_params=_params(("arbitrary",)),
        name="mlp_prompt_xattn_sample",
    )(x2p, wup, wdown, gpre_ffn, gpost_ffn, q_s, cache_mem_k[0], cache_mem_v[0])
    yp = yp.reshape(bp, lp, D_MODEL)

    ys = pl.pallas_call(
        _post_xa_mlp_sample_kernel,
        grid=(ts // ts_tile,),
        in_specs=[_row_spec(ts_tile, D_MODEL), _row_spec(ts_tile, D_MODEL), _const_spec((D_MODEL, D_MODEL)),
                  _const_spec((1, D_MODEL)), _const_spec((D_MODEL, D_FF)), _const_spec((D_FF, D_MODEL)),
                  _const_spec((1, D_MODEL)), _const_spec((1, D_MODEL))],
        out_specs=_row_spec(ts_tile, D_MODEL),
        out_shape=jax.ShapeDtypeStruct((ts, D_MODEL), F32),
        scratch_shapes=[pltpu.VMEM((ts_tile, D_MODEL), F32)],
        compiler_params=_params(("arbitrary",)),
        name="post_xa_mlp_sample",
    )(o_s, x1s, wxo, gpost_xa, wup, wdown, gpre_ffn, gpost_ffn).reshape(bs, ls, D_MODEL)

    hshape = (1, bp, N_MEM, N_HEADS, XA_DH)
    return (yp, ys, sg_p[None], sr_p[None], mk.reshape(hshape), mv.reshape(hshape), sg_s[None], sr_s[None])
```

```python
import functools
import math

import jax
import jax.numpy as jnp
from jax import lax
from jax.experimental import pallas as pl
from jax.experimental.pallas import tpu as pltpu

F32 = jnp.float32
BF16 = jnp.bfloat16

D_MODEL = 1024
N_HEADS = 4
D_K = 128
D_V = 256
N_MEM = 256
XA_DH = 256
D_FF = 4 * D_MODEL
GATE_RANK = 16
GATE_RANK_PAD = 128
GLA_TAU = 16.0
CHUNK = 64
ROPE_BASE = 10000.0
PAST_LEN = 16384
EPS = 1e-6
LOG_GAMMA = tuple(math.log1p(-(2.0 ** (-5.0 - h))) for h in range(N_HEADS))

C_GQ, C_GK, C_GV, C_GR = 0, 512, 1024, 2048
C_RQ, C_RK, C_RV, C_RG = 3072, 3584, 4096, 5120
C_GA, C_GB = 6144, 7168
N_MAIN = 8192
N_WA = 3072
N_WB = N_MAIN - N_WA
N_QK = N_HEADS * D_K
PROJ_BLOCK = 512
MLP_BLOCK = 1024

LANE = 128
VMEM_LIMIT_BYTES = 60 * 1024 * 1024


def _mm(a, b):
    return jnp.dot(a, b, preferred_element_type=F32)


def _mm_nt(a, b):
    return lax.dot_general(a, b, (((1,), (1,)), ((), ())), preferred_element_type=F32)


def _mm_tn(a, b):
    return lax.dot_general(a, b, (((0,), (0,)), ((), ())), preferred_element_type=F32)


def _rms(x, g):
    return x * lax.rsqrt(jnp.mean(x * x, axis=-1, keepdims=True) + EPS) * g


def _sigmoid(x):
    return 1.0 / (1.0 + jnp.exp(-x))


def _const_spec(shape):
    nd = len(shape)
    return pl.BlockSpec(shape, lambda *_: (0,) * nd, pipeline_mode=pl.Buffered(1))


def _params(sem):
    return pltpu.CompilerParams(dimension_semantics=sem, vmem_limit_bytes=VMEM_LIMIT_BYTES)


def _inproj_pieces(h_bf, wmain_ref, wga_ref, w2_ref, b2_ref, z_ref, la_ref):
    def z_block(j):
        def run():
            z_ref[:, j:j + PROJ_BLOCK] = _mm(h_bf, wmain_ref[:, j:j + PROJ_BLOCK])
        return run

    def gate():
        ga = _mm(h_bf, wga_ref[...])
        xg = _mm(ga.astype(BF16), w2_ref[...]) + b2_ref[...]
        log_sig = jnp.minimum(xg, 0.0) - jnp.log1p(jnp.exp(-jnp.abs(xg)))
        la_ref[...] = log_sig * (1.0 / GLA_TAU)

    return [z_block(j) for j in range(0, N_MAIN, PROJ_BLOCK)] + [gate]


def _inproj(h_bf, wmain_ref, wga_ref, w2_ref, b2_ref, z_ref, la_ref):
    for piece in _inproj_pieces(h_bf, wmain_ref, wga_ref, w2_ref, b2_ref, z_ref, la_ref):
        piece()


def _run_next(pieces):
    if pieces:
        pieces.pop(0)()


def _block_causal(n, c):
    shift = c.bit_length() - 1
    row = lax.broadcasted_iota(jnp.int32, (n, n), 0)
    col = lax.broadcasted_iota(jnp.int32, (n, n), 1)
    return ((row >> shift) == (col >> shift)) & (row >= col)


def _gla_tile(z_ref, la_ref, n, c, s_get, s_put, g_ref, o_ref, between=()):
    causal = _block_causal(n, c)
    tri = jnp.where(causal, 1.0, 0.0).astype(BF16)
    la = la_ref[...]
    la_hi = la.astype(BF16)
    la_lo = (la - la_hi.astype(F32)).astype(BF16)
    b = _mm(tri, la_hi) + _mm(tri, la_lo)
    n_chunks = n // c
    b_last = [b[(ci + 1) * c - 1:(ci + 1) * c, :] for ci in range(n_chunks)]
    b_last_full = jnp.concatenate([jnp.broadcast_to(bl, (c, N_QK)) for bl in b_last], axis=0)
    eb = jnp.exp(b)
    enb = jnp.exp(-b)
    ekd = jnp.exp(b_last_full - b)
    for h in range(N_HEADS):
        ks = slice(h * D_K, (h + 1) * D_K)
        q = z_ref[:, C_GQ + h * D_K:C_GQ + (h + 1) * D_K] * (D_K ** -0.5)
        k = z_ref[:, C_GK + h * D_K:C_GK + (h + 1) * D_K]
        v = z_ref[:, C_GV + h * D_V:C_GV + (h + 1) * D_V].astype(BF16)
        qe = (q * eb[:, ks]).astype(BF16)
        ke = (k * enb[:, ks]).astype(BF16)
        kd = (k * ekd[:, ks]).astype(BF16)
        sc = jnp.where(causal, _mm_nt(qe, ke), 0.0).astype(BF16)
        o_intra = _mm(sc, v)
        _run_next(between)
        outs = []
        for ci in range(n_chunks):
            r = slice(ci * c, (ci + 1) * c)
            s_old = s_get(ci, h)
            outs.append(o_intra[r] + _mm(qe[r], s_old.astype(BF16)))
            dcol = jnp.transpose(jnp.broadcast_to(jnp.exp(b_last[ci][:, ks]), (D_K, D_K)))
            s_put(ci, h, jnp.concatenate([dcol, dcol], axis=1) * s_old + _mm_tn(kd[r], v[r]))
        o = jnp.concatenate(outs, axis=0) if n_chunks > 1 else outs[0]
        ms = jnp.mean(o * o, axis=-1, keepdims=True)
        o_ref[:, h * D_V:(h + 1) * D_V] = o * lax.rsqrt(ms + EPS) * g_ref[:, h * D_V:(h + 1) * D_V]
        _run_next(between)


def _ret_tile(z_ref, cos, sin, n, c, s_get, s_put, g_ref, o_ref, dm_ref, between=()):
    tpos = (lax.broadcasted_iota(jnp.int32, (n, 1), 0) & (c - 1)).astype(F32)
    n_chunks = n // c
    for h in range(N_HEADS):
        lg = LOG_GAMMA[h]
        q = z_ref[:, C_RQ + h * D_K:C_RQ + (h + 1) * D_K]
        k = z_ref[:, C_RK + h * D_K:C_RK + (h + 1) * D_K]
        v = z_ref[:, C_RV + h * D_V:C_RV + (h + 1) * D_V].astype(BF16)
        q = q * cos + pltpu.roll(q, D_K // 2, 1) * sin
        k = (k * cos + pltpu.roll(k, D_K // 2, 1) * sin) * (D_K ** -0.5)
        sc = (_mm_nt(q.astype(BF16), k.astype(BF16)) * dm_ref[h]).astype(BF16)
        o_intra = _mm(sc, v)
        _run_next(between)
        qd = (q * jnp.exp(lg * (tpos + 1.0))).astype(BF16)
        kd = (k * jnp.exp(lg * (float(c - 1) - tpos))).astype(BF16)
        outs = []
        for ci in range(n_chunks):
            r = slice(ci * c, (ci + 1) * c)
            s_old = s_get(ci, h)
            outs.append(o_intra[r] + _mm(qd[r], s_old.astype(BF16)))
            s_put(ci, h, math.exp(lg * c) * s_old + _mm_tn(kd[r], v[r]))
        o = jnp.concatenate(outs, axis=0) if n_chunks > 1 else outs[0]
        mu = jnp.mean(o, axis=-1, keepdims=True)
        oc = o - mu
        var = jnp.mean(oc * oc, axis=-1, keepdims=True)
        o_ref[:, h * D_V:(h + 1) * D_V] = oc * lax.rsqrt(var + EPS) * g_ref[:, h * D_V:(h + 1) * D_V]
        _run_next(between)


def _merge(z_ref, og_ref, or_ref, merged_ref):
    for j in range(0, D_MODEL, 256):
        cs = slice(j, j + 256)
        gr = z_ref[:, C_GR + j:C_GR + j + 256]
        rg = z_ref[:, C_RG + j:C_RG + j + 256]
        ga = z_ref[:, C_GA + j:C_GA + j + 256]
        gb = z_ref[:, C_GB + j:C_GB + j + 256]
        o_g = og_ref[:, cs] * (gr * _sigmoid(gr))
        o_r = or_ref[:, cs] * (rg * _sigmoid(rg))
        merged_ref[:, cs] = (_sigmoid(ga) * o_g + _sigmoid(gb) * o_r).astype(merged_ref.dtype)


def _split_w_in_kernel(wt_ref, wtg_ref, w_ref, wga_ref):
    w_ref[...] = jnp.transpose(wt_ref[...]).astype(BF16)
    zeros = jnp.zeros((GATE_RANK_PAD - GATE_RANK, D_MODEL), F32)
    wga_ref[...] = jnp.transpose(jnp.concatenate([wtg_ref[...], zeros], axis=0)).astype(BF16)


def _memkv_kernel(m_ref, g_ref, wk_ref, wv_ref, k_ref, v_ref, kb_ref, vb_ref):
    m = _rms(m_ref[...], g_ref[...]).astype(BF16)
    k = _mm(m, wk_ref[...])
    v = _mm(m, wv_ref[...])
    k_ref[...] = k.reshape(k_ref.shape)
    v_ref[...] = v.reshape(v_ref.shape)
    kb_ref[...] = k.astype(BF16)
    vb_ref[...] = v.astype(BF16)


def _mixer_prompt_kernel(xc_ref, xn_ref, cos_ref, sin_ref, dm_ref, wmain_ref, wga_ref, w2_ref, b2_ref,
                         ggla_ref, gret_ref, wmix_ref, gpre_ref, gpost_ref,
                         y_ref, sg_ref, sr_ref,
                         z_ref, la_ref, og_ref, or_ref, merged_ref, *, tl, steps_per_row):
    step = pl.program_id(0)

    @pl.when(step % steps_per_row == 0)
    def _():
        sg_ref[...] = jnp.zeros_like(sg_ref)
        sr_ref[...] = jnp.zeros_like(sr_ref)

    def project_pieces(x, slot):
        h_bf = _rms(x, gpre_ref[...]).astype(BF16)
        return _inproj_pieces(h_bf, wmain_ref, wga_ref, w2_ref, b2_ref, z_ref.at[slot], la_ref.at[slot])

    @pl.when(step == 0)
    def _():
        for piece in project_pieces(xc_ref[0:tl, :], 0):
            piece()

    n_chunks = tl // CHUNK
    for half in range(2):
        rows = slice(half * tl, (half + 1) * tl)
        pending = project_pieces(xc_ref[tl:2 * tl, :] if half == 0 else xn_ref[...], 1 - half)
        _run_next(pending)
        z_cur, la_cur = z_ref.at[half], la_ref.at[half]

        carried = {}

        def gla_get(ci, h, carried=carried):
            return sg_ref[0, h] if ci == 0 else carried[h]

        def gla_put(ci, h, val, carried=carried):
            carried[h] = val
            if ci == n_chunks - 1:
                sg_ref[0, h] = val

        _gla_tile(z_cur, la_cur, tl, CHUNK, gla_get, gla_put, ggla_ref, og_ref, between=pending)

        def ret_put(ci, h, val):
            sr_ref[0, h] = val

        _ret_tile(z_cur, cos_ref[rows, :], sin_ref[rows, :], tl, tl, lambda ci, h: sr_ref[0, h], ret_put,
                  gret_ref, or_ref, dm_ref, between=pending)
        while pending:
            _run_next(pending)

        _merge(z_cur, og_ref, or_ref, merged_ref)
        m = _mm(merged_ref[...], wmix_ref[...])
        y_ref[rows, :] = xc_ref[rows, :] + _rms(m, gpost_ref[...])


def _staggered(a_stages, b_stages):
    a, b = list(a_stages), list(b_stages)
    order = [a.pop(0)]
    while a or b:
        order += a[:1] + b[:1]
        a, b = a[1:], b[1:]
    return order


def _xattn_stages(x_ref, rows, mk_ref, mv_ref, wq_ref, wo_ref, gpre_ref, gpost_ref, y_ref, o_ref):
    st = {}

    def norm():
        st["hx"] = _rms(x_ref[rows, :], gpre_ref[...]).astype(BF16)

    def q_proj():
        st["q"] = _mm(st["hx"], wq_ref[...])

    def heads():
        for h in range(N_HEADS):
            hs = slice(h * XA_DH, (h + 1) * XA_DH)
            s = _mm_nt(st["q"][:, hs].astype(BF16), mk_ref[0, :, hs]) * (XA_DH ** -0.5)
            p = jnp.exp(s - jnp.max(s, axis=-1, keepdims=True))
            p = p * (1.0 / jnp.sum(p, axis=-1, keepdims=True))
            o_ref[rows, hs] = _mm(p.astype(BF16), mv_ref[0, :, hs]).astype(o_ref.dtype)

    def out_proj():
        st["a"] = _mm(o_ref[rows, :], wo_ref[...])

    def post():
        y_ref[rows, :] = x_ref[rows, :] + _rms(st["a"], gpost_ref[...])

    return [norm, q_proj, heads, out_proj, post]


def _mlp_stages(x_ref, rows, wup_ref, wdown_ref, gpre_ref, gpost_ref, y_ref):
    st = {}

    def norm():
        st["h"] = _rms(x_ref[rows, :], gpre_ref[...]).astype(BF16)

    def hidden_block(j):
        def run():
            u = jnp.maximum(_mm(st["h"], wup_ref[:, j:j + MLP_BLOCK]), 0.0)
            part = _mm((u * u).astype(BF16), wdown_ref[j:j + MLP_BLOCK, :])
            st["acc"] = part if j == 0 else st["acc"] + part
        return run

    def post():
        y_ref[rows, :] = x_ref[rows, :] + _rms(st["acc"], gpost_ref[...])

    return [norm] + [hidden_block(j) for j in range(0, D_FF, MLP_BLOCK)] + [post]


def _mlp_halves(x_ref, wup_ref, wdown_ref, gpre_ref, gpost_ref, y_ref):
    half = x_ref.shape[0] // 2
    args = (wup_ref, wdown_ref, gpre_ref, gpost_ref, y_ref)
    return _staggered(_mlp_stages(x_ref, slice(0, half), *args), _mlp_stages(x_ref, slice(half, 2 * half), *args))


def _mlp_kernel(x_ref, wup_ref, wdown_ref, gpre_ref, gpost_ref, y_ref):
    for stage in _mlp_halves(x_ref, wup_ref, wdown_ref, gpre_ref, gpost_ref, y_ref):
        stage()


def _inproj_kernel(x_ref, wmain_ref, wga_ref, w2_ref, b2_ref, gpre_ref, z_ref, la_ref):
    h_bf = _rms(x_ref[...], gpre_ref[...]).astype(BF16)
    _inproj(h_bf, wmain_ref, wga_ref, w2_ref, b2_ref, z_ref, la_ref)


def _mixer_sample_body(z_ref, la_ref, cos_ref, sin_ref, dm_ref, sgi_ref, sri_ref, ggla_ref, gret_ref,
                       merged_ref, sgo_ref, sro_ref, og_ref, or_ref, nb, ls, between=()):
    def gla_put(ci, h, val):
        sgo_ref[ci, h] = val

    def ret_put(ci, h, val):
        sro_ref[ci, h] = val

    _gla_tile(z_ref, la_ref, nb * ls, ls, lambda ci, h: sgi_ref[ci, h], gla_put, ggla_ref, og_ref, between=between)
    _ret_tile(z_ref, cos_ref[...], sin_ref[...], nb * ls, ls, lambda ci, h: sri_ref[ci, h], ret_put,
              gret_ref, or_ref, dm_ref, between=between)
    _merge(z_ref, og_ref, or_ref, merged_ref)


def _xattn_prompt_mixer_sample_kernel(x_ref, mk_ref, mv_ref, wq_ref, wo_ref, gpre_ref, gpost_ref,
                                      z_ref, la_ref, cos_ref, sin_ref, dm_ref, sgi_ref, sri_ref, ggla_ref, gret_ref,
                                      wup_f_ref, wdown_f_ref,
                                      y_ref, merged_ref, sgo_ref, sro_ref, wup_b_ref, wdown_b_ref,
                                      o_ref, og_ref, or_ref, *, nb, ls):
    wup_b_ref[...] = wup_f_ref[...].astype(BF16)
    wdown_b_ref[...] = wdown_f_ref[...].astype(BF16)
    half = x_ref.shape[0] // 2
    args = (mk_ref, mv_ref, wq_ref, wo_ref, gpre_ref, gpost_ref, y_ref, o_ref)
    pending = _staggered(_xattn_stages(x_ref, slice(0, half), *args),
                         _xattn_stages(x_ref, slice(half, 2 * half), *args))
    _mixer_sample_body(z_ref, la_ref, cos_ref, sin_ref, dm_ref, sgi_ref, sri_ref, ggla_ref, gret_ref,
                       merged_ref, sgo_ref, sro_ref, og_ref, or_ref, nb, ls, between=pending)
    while pending:
        _run_next(pending)


def _post_mix_sample_kernel(a_ref, res_ref, wmix_ref, gpost_ref, gpre_xa_ref, wq_ref, x1_ref, q_ref):
    x1 = res_ref[...] + _rms(_mm(a_ref[...], wmix_ref[...]), gpost_ref[...])
    x1_ref[...] = x1
    q_ref[...] = _mm(_rms(x1, gpre_xa_ref[...]).astype(BF16), wq_ref[...])


def _post_xa_mlp_sample_kernel(o_ref, x1_ref, wo_ref, gpost_xa_ref, wup_ref, wdown_ref, gpre_ref, gpost_ref,
                               y_ref, x2_ref):
    x2_ref[...] = x1_ref[...] + _rms(_mm(o_ref[...], wo_ref[...]), gpost_xa_ref[...])
    _mlp_kernel(x2_ref, wup_ref, wdown_ref, gpre_ref, gpost_ref, y_ref)


def _xattn_sample_body(q_ref, k_ref, v_ref, o_ref, nb, ls, between=()):
    n_rows = N_HEADS * ls
    n_cols = N_MEM * N_HEADS
    row_head = lax.broadcasted_iota(jnp.int32, (n_rows, n_cols), 0) // ls
    col_head = lax.broadcasted_iota(jnp.int32, (n_rows, n_cols), 1) & (N_HEADS - 1)
    own = row_head == col_head

    for e in range(nb):
        rows = slice(e * ls, (e + 1) * ls)
        q = q_ref[rows, :]
        q_hm = jnp.concatenate([q[:, h * XA_DH:(h + 1) * XA_DH] for h in range(N_HEADS)], axis=0)
        k_all = k_ref[e].reshape(n_cols, XA_DH)
        v_all = v_ref[e].reshape(n_cols, XA_DH)
        s = _mm_nt(q_hm.astype(BF16), k_all.astype(BF16)) * (XA_DH ** -0.5)
        s = jnp.where(own, s, -jnp.inf)
        p = jnp.exp(s - jnp.max(s, axis=-1, keepdims=True))
        p = p * (1.0 / jnp.sum(p, axis=-1, keepdims=True))
        _run_next(between)
        o = _mm(p.astype(BF16), v_all.astype(BF16))
        for h in range(N_HEADS):
            o_ref[rows, h * XA_DH:(h + 1) * XA_DH] = o[h * ls:(h + 1) * ls, :].astype(o_ref.dtype)
        _run_next(between)
        _run_next(between)


def _mlp_xattn_sample_kernel(x_ref, wup_ref, wdown_ref, gpre_ref, gpost_ref, q_ref, k_ref, v_ref,
                             y_ref, o_ref, *, nb, ls):
    pending = _mlp_halves(x_ref, wup_ref, wdown_ref, gpre_ref, gpost_ref, y_ref)
    _xattn_sample_body(q_ref, k_ref, v_ref, o_ref, nb, ls, between=pending)
    while pending:
        _run_next(pending)


def _rope_tables(pos):
    half = D_K // 2
    inv = ROPE_BASE ** (-jnp.arange(half, dtype=F32) / half)
    ang = pos.astype(F32)[:, None] * inv[None, :]
    cos, sin = jnp.cos(ang), jnp.sin(ang)
    return jnp.concatenate([cos, cos], axis=-1), jnp.concatenate([-sin, sin], axis=-1)


def _decay_masks(n, c):
    t = jnp.arange(n, dtype=jnp.int32)
    keep = ((t[:, None] // c) == (t[None, :] // c)) & (t[:, None] >= t[None, :])
    dist = (t[:, None] - t[None, :]).astype(F32)
    lg = jnp.asarray(LOG_GAMMA, F32)[:, None, None]
    return jnp.where(keep[None], jnp.exp(lg * dist[None]), 0.0)


def _row_spec(tm, n):
    return pl.BlockSpec((tm, n), lambda i: (i, 0))


def kernel(x_prompt, x_sample, state_gla, state_ret, cache_mem_k, cache_mem_v, mem_prompt, w_in, w_gla_a2, b_gla_a, g_gla_head, g_ret_head, w_mix_out, w_xq, w_xk, w_xv, w_xo, g_mem, w_up, w_down, g_pre_mix, g_post_mix, g_pre_xa, g_post_xa, g_pre_ffn, g_post_ffn):
    depth = w_in.shape[0]
    assert depth == 1
    bp, lp, _ = x_prompt.shape
    bs, ls, _ = x_sample.shape
    tl = 256
    tm = 512
    ts_tile = 256
    assert lp % (2 * tl) == 0 and tl % CHUNK == 0 and lp % tm == 0
    n_mlp_steps = bp * lp // tm
    assert bs % n_mlp_steps == 0
    nb_mix = bs // n_mlp_steps
    assert (bs * ls) % ts_tile == 0 and (nb_mix * ls) % 16 == 0
    assert ls % 8 == 0 and ls <= CHUNK and ls & (ls - 1) == 0 and tl & (tl - 1) == 0
    assert (bp * N_MEM) % tm == 0

    w = w_in[0]
    c_low = 2 * N_QK + 2 * D_MODEL
    assert c_low == N_WA and w.shape[1] == N_MAIN + GATE_RANK
    wt = jnp.transpose(w)
    n_wa_blocks = N_WA // PROJ_BLOCK

    def wt_row(i):
        units = PROJ_BLOCK // GATE_RANK
        return GATE_RANK * jnp.where(i < n_wa_blocks, i * units, i * units + 1)

    wmain, wga = pl.pallas_call(
        _split_w_in_kernel,
        grid=(N_MAIN // PROJ_BLOCK,),
        in_specs=[pl.BlockSpec((pl.Element(PROJ_BLOCK), pl.Element(D_MODEL)), lambda i: (wt_row(i), 0)),
                  pl.BlockSpec((GATE_RANK, D_MODEL), lambda i: (N_WA // GATE_RANK, 0))],
        out_specs=[pl.BlockSpec((D_MODEL, PROJ_BLOCK), lambda i: (0, i)),
                   pl.BlockSpec((D_MODEL, GATE_RANK_PAD), lambda i: (0, 0))],
        out_shape=[jax.ShapeDtypeStruct((D_MODEL, N_MAIN), BF16),
                   jax.ShapeDtypeStruct((D_MODEL, GATE_RANK_PAD), BF16)],
        compiler_params=_params(("arbitrary",)),
        name="split_w_in",
    )(wt, wt)
    w2 = jnp.pad(w_gla_a2[0], ((0, GATE_RANK_PAD - GATE_RANK), (0, 0))).astype(BF16)
    b2 = b_gla_a[0].reshape(1, N_QK)
    ggla = g_gla_head[0].reshape(1, D_MODEL)
    gret = g_ret_head[0].reshape(1, D_MODEL)
    wmix = w_mix_out[0].astype(BF16)
    wxq, wxk, wxv, wxo = (t[0].astype(BF16) for t in (w_xq, w_xk, w_xv, w_xo))
    row = lambda g: g[0].reshape(1, D_MODEL)
    gmem, gpre_mix, gpost_mix, gpre_xa, gpost_xa, gpre_ffn, gpost_ffn = (
        row(g) for g in (g_mem, g_pre_mix, g_post_mix, g_pre_xa, g_post_xa, g_pre_ffn, g_post_ffn))
    cos_p, sin_p = _rope_tables(jnp.arange(lp, dtype=jnp.int32))
    cos_s, sin_s = _rope_tables(PAST_LEN + jnp.arange(ls, dtype=jnp.int32))
    rows_mix = nb_mix * ls
    cos_s, sin_s = jnp.tile(cos_s, (nb_mix, 1)), jnp.tile(sin_s, (nb_mix, 1))
    dm_p = _decay_masks(tl, tl)
    dm_s = _decay_masks(rows_mix, ls)

    nmem_rows = bp * N_MEM
    mk, mv, mk_bf, mv_bf = pl.pallas_call(
        _memkv_kernel,
        grid=(nmem_rows // tm,),
        in_specs=[_row_spec(tm, D_MODEL), _const_spec((1, D_MODEL)),
                  _const_spec((D_MODEL, D_MODEL)), _const_spec((D_MODEL, D_MODEL))],
        out_specs=[pl.BlockSpec((tm, N_HEADS, XA_DH), lambda i: (i, 0, 0))] * 2 + [_row_spec(tm, D_MODEL)] * 2,
        out_shape=[jax.ShapeDtypeStruct((nmem_rows, N_HEADS, XA_DH), F32)] * 2
        + [jax.ShapeDtypeStruct((nmem_rows, D_MODEL), BF16)] * 2,
        compiler_params=_params(("arbitrary",)),
        name="memkv",
    )(mem_prompt.reshape(nmem_rows, D_MODEL), gmem, wxk, wxv)

    n_tiles = bp * lp // tl
    steps_per_row = lp // (2 * tl)
    state_spec = pl.BlockSpec((1, N_HEADS, D_K, D_V), lambda s: (s // steps_per_row, 0, 0, 0))
    rope_spec = pl.BlockSpec((2 * tl, D_K), lambda s: (s % steps_per_row, 0))
    x1p, sg_p, sr_p = pl.pallas_call(
        functools.partial(_mixer_prompt_kernel, tl=tl, steps_per_row=steps_per_row),
        grid=(n_tiles // 2,),
        in_specs=[_row_spec(2 * tl, D_MODEL),
                  pl.BlockSpec((tl, D_MODEL), lambda s: (jnp.minimum(2 * s + 2, n_tiles - 1), 0)),
                  rope_spec, rope_spec,
                  _const_spec((N_HEADS, tl, tl)),
                  _const_spec((D_MODEL, N_MAIN)), _const_spec((D_MODEL, GATE_RANK_PAD)),
                  _const_spec((GATE_RANK_PAD, N_QK)), _const_spec((1, N_QK)),
                  _const_spec((1, D_MODEL)), _const_spec((1, D_MODEL)),
                  _const_spec((D_MODEL, D_MODEL)), _const_spec((1, D_MODEL)), _const_spec((1, D_MODEL))],
        out_specs=[_row_spec(2 * tl, D_MODEL), state_spec, state_spec],
        out_shape=[jax.ShapeDtypeStruct((bp * lp, D_MODEL), F32),
                   jax.ShapeDtypeStruct((bp, N_HEADS, D_K, D_V), F32),
                   jax.ShapeDtypeStruct((bp, N_HEADS, D_K, D_V), F32)],
        scratch_shapes=[pltpu.VMEM((2, tl, N_MAIN), F32), pltpu.VMEM((2, tl, N_QK), F32),
                        pltpu.VMEM((tl, D_MODEL), F32), pltpu.VMEM((tl, D_MODEL), F32),
                        pltpu.VMEM((tl, D_MODEL), BF16)],
        compiler_params=_params(("arbitrary",)),
        name="mixer_prompt",
    )(x_prompt.reshape(bp * lp, D_MODEL), x_prompt.reshape(bp * lp, D_MODEL), cos_p, sin_p, dm_p,
      wmain, wga, w2, b2, ggla, gret, wmix, gpre_mix, gpost_mix)
    x1p = x1p.reshape(bp, lp, D_MODEL)

    ts = bs * ls
    xs = x_sample.reshape(ts, D_MODEL)
    z_s, la_s = pl.pallas_call(
        _inproj_kernel,
        grid=(ts // ts_tile,),
        in_specs=[_row_spec(ts_tile, D_MODEL), _const_spec((D_MODEL, N_MAIN)),
                  _const_spec((D_MODEL, GATE_RANK_PAD)),
                  _const_spec((GATE_RANK_PAD, N_QK)), _const_spec((1, N_QK)), _const_spec((1, D_MODEL))],
        out_specs=[_row_spec(ts_tile, N_MAIN), _row_spec(ts_tile, N_QK)],
        out_shape=[jax.ShapeDtypeStruct((ts, N_MAIN), F32), jax.ShapeDtypeStruct((ts, N_QK), F32)],
        compiler_params=_params(("arbitrary",)),
        name="inproj_sample",
    )(xs, wmain, wga, w2, b2, gpre_mix)

    xa_per_row = lp // tm
    mem_spec = pl.BlockSpec((1, N_MEM, D_MODEL), lambda s: (s // xa_per_row, 0, 0))
    st_spec = pl.BlockSpec((nb_mix, N_HEADS, D_K, D_V), lambda i: (i, 0, 0, 0))
    assert D_MODEL % (16 * n_mlp_steps) == 0
    wup_spec = _row_spec(D_MODEL // n_mlp_steps, D_FF)
    wdown_spec = _row_spec(D_FF // n_mlp_steps, D_MODEL)
    x2p, merged_s, sg_s, sr_s, wup, wdown = pl.pallas_call(
        functools.partial(_xattn_prompt_mixer_sample_kernel, nb=nb_mix, ls=ls),
        grid=(n_mlp_steps,),
        in_specs=[_row_spec(tm, D_MODEL), mem_spec, mem_spec,
                  _const_spec((D_MODEL, D_MODEL)), _const_spec((D_MODEL, D_MODEL)),
                  _const_spec((1, D_MODEL)), _const_spec((1, D_MODEL)),
                  _row_spec(rows_mix, N_MAIN), _row_spec(rows_mix, N_QK),
                  _const_spec((rows_mix, D_K)), _const_spec((rows_mix, D_K)),
                  _const_spec((N_HEADS, rows_mix, rows_mix)), st_spec, st_spec,
                  _const_spec((1, D_MODEL)), _const_spec((1, D_MODEL)),
                  wup_spec, wdown_spec],
        out_specs=[_row_spec(tm, D_MODEL), _row_spec(rows_mix, D_MODEL), st_spec, st_spec, wup_spec, wdown_spec],
        out_shape=[jax.ShapeDtypeStruct((bp * lp, D_MODEL), F32),
                   jax.ShapeDtypeStruct((ts, D_MODEL), BF16),
                   jax.ShapeDtypeStruct((bs, N_HEADS, D_K, D_V), F32),
                   jax.ShapeDtypeStruct((bs, N_HEADS, D_K, D_V), F32),
                   jax.ShapeDtypeStruct((D_MODEL, D_FF), BF16),
                   jax.ShapeDtypeStruct((D_FF, D_MODEL), BF16)],
        scratch_shapes=[pltpu.VMEM((tm, D_MODEL), BF16),
                        pltpu.VMEM((rows_mix, D_MODEL), F32), pltpu.VMEM((rows_mix, D_MODEL), F32)],
        compiler_params=_params(("arbitrary",)),
        name="xattn_prompt_mixer_sample",
    )(x1p.reshape(bp * lp, D_MODEL), mk_bf.reshape(bp, N_MEM, D_MODEL), mv_bf.reshape(bp, N_MEM, D_MODEL),
      wxq, wxo, gpre_xa, gpost_xa,
      z_s, la_s, cos_s, sin_s, dm_s, state_gla[0], state_ret[0], ggla, gret, w_up[0], w_down[0])

    x1s, q_s = pl.pallas_call(
        _post_mix_sample_kernel,
        grid=(ts // ts_tile,),
        in_specs=[_row_spec(ts_tile, D_MODEL), _row_spec(ts_tile, D_MODEL), _const_spec((D_MODEL, D_MODEL)),
                  _const_spec((1, D_MODEL)), _const_spec((1, D_MODEL)), _const_spec((D_MODEL, D_MODEL))],
        out_specs=[_row_spec(ts_tile, D_MODEL)] * 2,
        out_shape=[jax.ShapeDtypeStruct((ts, D_MODEL), F32)] * 2,
        compiler_params=_params(("arbitrary",)),
        name="post_mix_sample",
    )(merged_s, xs, wmix, gpost_mix, gpre_xa, wxq)

    kv_spec = pl.BlockSpec((nb_mix, N_MEM, N_HEADS, XA_DH), lambda i: (i, 0, 0, 0))
    yp, o_s = pl.pallas_call(
        functools.partial(_mlp_xattn_sample_kernel, nb=nb_mix, ls=ls),
        grid=(n_mlp_steps,),
        in_specs=[_row_spec(tm, D_MODEL), _const_spec((D_MODEL, D_FF)), _const_spec((D_FF, D_MODEL)),
                  _const_spec((1, D_MODEL)), _const_spec((1, D_MODEL)),
                  _row_spec(rows_mix, D_MODEL), kv_spec, kv_spec],
        out_specs=[_row_spec(tm, D_MODEL), _row_spec(rows_mix, D_MODEL)],
        out_shape=[jax.ShapeDtypeStruct((bp * lp, D_MODEL), F32), jax.ShapeDtypeStruct((ts, D_MODEL), BF16)],
        compiler_params=_params(("arbitrary",)),
        name="mlp_prompt_xattn_sample",
    )(x2p, wup, wdown, gpre_ffn, gpost_ffn, q_s, cache_mem_k[0], cache_mem_v[0])
    yp = yp.reshape(bp, lp, D_MODEL)

    ys = pl.pallas_call(
        _post_xa_mlp_sample_kernel,
        grid=(ts // ts_tile,),
        in_specs=[_row_spec(ts_tile, D_MODEL), _row_spec(ts_tile, D_MODEL), _const_spec((D_MODEL, D_MODEL)),
                  _const_spec((1, D_MODEL)), _const_spec((D_MODEL, D_FF)), _const_spec((D_FF, D_MODEL)),
                  _const_spec((1, D_MODEL)), _const_spec((1, D_MODEL))],
        out_specs=_row_spec(ts_tile, D_MODEL),
        out_shape=jax.ShapeDtypeStruct((ts, D_MODEL), F32),
        scratch_shapes=[pltpu.VMEM((ts_tile, D_MODEL), F32)],
        compiler_params=_params(("arbitrary",)),
        name="post_xa_mlp_sample",
    )(o_s, x1s, wxo, gpost_xa, wup, wdown, gpre_ffn, gpost_ffn).reshape(bs, ls, D_MODEL)

    hshape = (1, bp, N_MEM, N_HEADS, XA_DH)
    return (yp, ys, sg_p[None], sr_p[None], mk.reshape(hshape), mv.reshape(hshape), sg_s[None], sr_s[None])
```

```python
import functools
import math

import jax
import jax.numpy as jnp
from jax import lax
from jax.experimental import pallas as pl
from jax.experimental.pallas import tpu as pltpu

F32 = jnp.float32
BF16 = jnp.bfloat16

D_MODEL = 1024
N_HEADS = 4
D_K = 128
D_V = 256
N_MEM = 256
XA_DH = 256
D_FF = 4 * D_MODEL
GATE_RANK = 16
GATE_RANK_PAD = 128
GLA_TAU = 16.0
CHUNK = 64
ROPE_BASE = 10000.0
PAST_LEN = 16384
EPS = 1e-6
LOG_GAMMA = tuple(math.log1p(-(2.0 ** (-5.0 - h))) for h in range(N_HEADS))

C_GQ, C_GK, C_GV, C_GR = 0, 512, 1024, 2048
C_RQ, C_RK, C_RV, C_RG = 3072, 3584, 4096, 5120
C_GA, C_GB = 6144, 7168
N_MAIN = 8192
N_WA = 3072
N_WB = N_MAIN - N_WA
N_QK = N_HEADS * D_K
PROJ_BLOCK = 512
MLP_BLOCK = 1024
SPLIT_BLOCK = 1024

LANE = 128
VMEM_LIMIT_BYTES = 60 * 1024 * 1024


def _mm(a, b):
    return jnp.dot(a, b, preferred_element_type=F32)


def _mm_nt(a, b):
    return lax.dot_general(a, b, (((1,), (1,)), ((), ())), preferred_element_type=F32)


def _mm_tn(a, b):
    return lax.dot_general(a, b, (((0,), (0,)), ((), ())), preferred_element_type=F32)


def _rms(x, g):
    return x * lax.rsqrt(jnp.mean(x * x, axis=-1, keepdims=True) + EPS) * g


def _sigmoid(x):
    return 1.0 / (1.0 + jnp.exp(-x))


def _const_spec(shape):
    nd = len(shape)
    return pl.BlockSpec(shape, lambda *_: (0,) * nd, pipeline_mode=pl.Buffered(1))


def _params(sem):
    return pltpu.CompilerParams(dimension_semantics=sem, vmem_limit_bytes=VMEM_LIMIT_BYTES)


def _inproj_pieces(h_bf, wmain_ref, wga_ref, w2_ref, b2_ref, z_ref, la_ref):
    def z_block(j):
        def run():
            z_ref[:, j:j + PROJ_BLOCK] = _mm(h_bf, wmain_ref[:, j:j + PROJ_BLOCK])
        return run

    def gate():
        ga = _mm(h_bf, wga_ref[...])
        xg = _mm(ga.astype(BF16), w2_ref[...]) + b2_ref[...]
        log_sig = jnp.minimum(xg, 0.0) - jnp.log1p(jnp.exp(-jnp.abs(xg)))
        la_ref[...] = log_sig * (1.0 / GLA_TAU)

    return [z_block(j) for j in range(0, N_MAIN, PROJ_BLOCK)] + [gate]


def _inproj(h_bf, wmain_ref, wga_ref, w2_ref, b2_ref, z_ref, la_ref):
    for piece in _inproj_pieces(h_bf, wmain_ref, wga_ref, w2_ref, b2_ref, z_ref, la_ref):
        piece()


def _run_next(pieces):
    if pieces:
        pieces.pop(0)()


def _block_causal(n, c):
    shift = c.bit_length() - 1
    row = lax.broadcasted_iota(jnp.int32, (n, n), 0)
    col = lax.broadcasted_iota(jnp.int32, (n, n), 1)
    return ((row >> shift) == (col >> shift)) & (row >= col)


def _gla_tile(z_ref, la_ref, n, c, s_get, s_put, g_ref, o_ref, between=()):
    causal = _block_causal(n, c)
    tri = jnp.where(causal, 1.0, 0.0).astype(BF16)
    la = la_ref[...]
    la_hi = la.astype(BF16)
    la_lo = (la - la_hi.astype(F32)).astype(BF16)
    b = _mm(tri, la_hi) + _mm(tri, la_lo)
    n_chunks = n // c
    b_last = [b[(ci + 1) * c - 1:(ci + 1) * c, :] for ci in range(n_chunks)]
    for h in range(N_HEADS):
        ks = slice(h * D_K, (h + 1) * D_K)
        q = z_ref[:, C_GQ + h * D_K:C_GQ + (h + 1) * D_K] * (D_K ** -0.5)
        k = z_ref[:, C_GK + h * D_K:C_GK + (h + 1) * D_K]
        v = z_ref[:, C_GV + h * D_V:C_GV + (h + 1) * D_V].astype(BF16)
        b_h = b[:, ks]
        b_last_h = jnp.concatenate([jnp.broadcast_to(bl[:, ks], (c, D_K)) for bl in b_last], axis=0)
        qe = (q * jnp.exp(b_h)).astype(BF16)
        ke = (k * jnp.exp(-b_h)).astype(BF16)
        kd = (k * jnp.exp(b_last_h - b_h)).astype(BF16)
        sc = jnp.where(causal, _mm_nt(qe, ke), 0.0).astype(BF16)
        o_intra = _mm(sc, v)
        _run_next(between)
        outs = []
        for ci in range(n_chunks):
            r = slice(ci * c, (ci + 1) * c)
            s_old = s_get(ci, h)
            outs.append(o_intra[r] + _mm(qe[r], s_old.astype(BF16)))
            dcol = jnp.transpose(jnp.broadcast_to(jnp.exp(b_last[ci][:, ks]), (D_K, D_K)))
            s_put(ci, h, jnp.concatenate([dcol, dcol], axis=1) * s_old + _mm_tn(kd[r], v[r]))
        o = jnp.concatenate(outs, axis=0) if n_chunks > 1 else outs[0]
        ms = jnp.mean(o * o, axis=-1, keepdims=True)
        o_ref[:, h * D_V:(h + 1) * D_V] = o * lax.rsqrt(ms + EPS) * g_ref[:, h * D_V:(h + 1) * D_V]
        _run_next(between)


def _ret_tile(z_ref, cos, sin, n, c, s_get, s_put, g_ref, o_ref, dm_ref, between=()):
    tpos = (lax.broadcasted_iota(jnp.int32, (n, 1), 0) & (c - 1)).astype(F32)
    n_chunks = n // c
    for h in range(N_HEADS):
        lg = LOG_GAMMA[h]
        q = z_ref[:, C_RQ + h * D_K:C_RQ + (h + 1) * D_K]
        k = z_ref[:, C_RK + h * D_K:C_RK + (h + 1) * D_K]
        v = z_ref[:, C_RV + h * D_V:C_RV + (h + 1) * D_V].astype(BF16)
        q = q * cos + pltpu.roll(q, D_K // 2, 1) * sin
        k = (k * cos + pltpu.roll(k, D_K // 2, 1) * sin) * (D_K ** -0.5)
        sc = (_mm_nt(q.astype(BF16), k.astype(BF16)) * dm_ref[h]).astype(BF16)
        o_intra = _mm(sc, v)
        _run_next(between)
        qd = (q * jnp.exp(lg * (tpos + 1.0))).astype(BF16)
        kd = (k * jnp.exp(lg * (float(c - 1) - tpos))).astype(BF16)
        outs = []
        for ci in range(n_chunks):
            r = slice(ci * c, (ci + 1) * c)
            s_old = s_get(ci, h)
            outs.append(o_intra[r] + _mm(qd[r], s_old.astype(BF16)))
            s_put(ci, h, math.exp(lg * c) * s_old + _mm_tn(kd[r], v[r]))
        o = jnp.concatenate(outs, axis=0) if n_chunks > 1 else outs[0]
        mu = jnp.mean(o, axis=-1, keepdims=True)
        oc = o - mu
        var = jnp.mean(oc * oc, axis=-1, keepdims=True)
        o_ref[:, h * D_V:(h + 1) * D_V] = oc * lax.rsqrt(var + EPS) * g_ref[:, h * D_V:(h + 1) * D_V]
        _run_next(between)


def _merge(z_ref, og_ref, or_ref, merged_ref):
    for j in range(0, D_MODEL, 256):
        cs = slice(j, j + 256)
        gr = z_ref[:, C_GR + j:C_GR + j + 256]
        rg = z_ref[:, C_RG + j:C_RG + j + 256]
        ga = z_ref[:, C_GA + j:C_GA + j + 256]
        gb = z_ref[:, C_GB + j:C_GB + j + 256]
        o_g = og_ref[:, cs] * (gr * _sigmoid(gr))
        o_r = or_ref[:, cs] * (rg * _sigmoid(rg))
        merged_ref[:, cs] = (_sigmoid(ga) * o_g + _sigmoid(gb) * o_r).astype(merged_ref.dtype)


def _split_w_in_kernel(wt_ref, wtg_ref, w_ref, wga_ref):
    w_ref[...] = jnp.transpose(wt_ref[...]).astype(BF16)
    zeros = jnp.zeros((GATE_RANK_PAD - GATE_RANK, D_MODEL), F32)
    wga_ref[...] = jnp.transpose(jnp.concatenate([wtg_ref[...], zeros], axis=0)).astype(BF16)


def _memkv_kernel(m_ref, g_ref, wk_ref, wv_ref, k_ref, v_ref, kb_ref, vb_ref):
    m = _rms(m_ref[...], g_ref[...]).astype(BF16)
    k = _mm(m, wk_ref[...])
    v = _mm(m, wv_ref[...])
    k_ref[...] = k.reshape(k_ref.shape)
    v_ref[...] = v.reshape(v_ref.shape)
    kb_ref[...] = k.astype(BF16)
    vb_ref[...] = v.astype(BF16)


def _mixer_prompt_kernel(xc_ref, xn_ref, cos_ref, sin_ref, dm_ref, wmain_ref, wga_ref, w2_ref, b2_ref,
                         ggla_ref, gret_ref, wmix_ref, gpre_ref, gpost_ref,
                         y_ref, sg_ref, sr_ref,
                         z_ref, la_ref, og_ref, or_ref, merged_ref, *, tl, steps_per_row):
    step = pl.program_id(0)

    @pl.when(step % steps_per_row == 0)
    def _():
        sg_ref[...] = jnp.zeros_like(sg_ref)
        sr_ref[...] = jnp.zeros_like(sr_ref)

    def project_pieces(x, slot):
        h_bf = _rms(x, gpre_ref[...]).astype(BF16)
        return _inproj_pieces(h_bf, wmain_ref, wga_ref, w2_ref, b2_ref, z_ref.at[slot], la_ref.at[slot])

    @pl.when(step == 0)
    def _():
        for piece in project_pieces(xc_ref[0:tl, :], 0):
            piece()

    n_chunks = tl // CHUNK
    for half in range(2):
        rows = slice(half * tl, (half + 1) * tl)
        pending = project_pieces(xc_ref[tl:2 * tl, :] if half == 0 else xn_ref[...], 1 - half)
        _run_next(pending)
        z_cur, la_cur = z_ref.at[half], la_ref.at[half]

        carried = {}

        def gla_get(ci, h, carried=carried):
            return sg_ref[0, h] if ci == 0 else carried[h]

        def gla_put(ci, h, val, carried=carried):
            carried[h] = val
            if ci == n_chunks - 1:
                sg_ref[0, h] = val

        _gla_tile(z_cur, la_cur, tl, CHUNK, gla_get, gla_put, ggla_ref, og_ref, between=pending)

        def ret_put(ci, h, val):
            sr_ref[0, h] = val

        _ret_tile(z_cur, cos_ref[rows, :], sin_ref[rows, :], tl, tl, lambda ci, h: sr_ref[0, h], ret_put,
                  gret_ref, or_ref, dm_ref, between=pending)
        while pending:
            _run_next(pending)

        _merge(z_cur, og_ref, or_ref, merged_ref)
        m = _mm(merged_ref[...], wmix_ref[...])
        y_ref[rows, :] = xc_ref[rows, :] + _rms(m, gpost_ref[...])


def _staggered(a_stages, b_stages):
    a, b = list(a_stages), list(b_stages)
    order = [a.pop(0)]
    while a or b:
        order += a[:1] + b[:1]
        a, b = a[1:], b[1:]
    return order


def _xattn_stages(x_ref, rows, mk_ref, mv_ref, wq_ref, wo_ref, gpre_ref, gpost_ref, y_ref, o_ref):
    st = {}

    def norm():
        st["hx"] = _rms(x_ref[rows, :], gpre_ref[...]).astype(BF16)

    def q_proj():
        st["q"] = _mm(st["hx"], wq_ref[...])

    def heads():
        for h in range(N_HEADS):
            hs = slice(h * XA_DH, (h + 1) * XA_DH)
            s = _mm_nt(st["q"][:, hs].astype(BF16), mk_ref[0, :, hs]) * (XA_DH ** -0.5)
            p = jnp.exp(s - jnp.max(s, axis=-1, keepdims=True))
            p = p * (1.0 / jnp.sum(p, axis=-1, keepdims=True))
            o_ref[rows, hs] = _mm(p.astype(BF16), mv_ref[0, :, hs]).astype(o_ref.dtype)

    def out_proj():
        st["a"] = _mm(o_ref[rows, :], wo_ref[...])

    def post():
        y_ref[rows, :] = x_ref[rows, :] + _rms(st["a"], gpost_ref[...])

    return [norm, q_proj, heads, out_proj, post]


def _mlp_stages(x_ref, rows, wup_ref, wdown_ref, gpre_ref, gpost_ref, y_ref):
    st = {}

    def norm():
        st["h"] = _rms(x_ref[rows, :], gpre_ref[...]).astype(BF16)

    def hidden_block(j):
        def run():
            u = jnp.maximum(_mm(st["h"], wup_ref[:, j:j + MLP_BLOCK]), 0.0)
            part = _mm((u * u).astype(BF16), wdown_ref[j:j + MLP_BLOCK, :])
            st["acc"] = part if j == 0 else st["acc"] + part
        return run

    def post():
        y_ref[rows, :] = x_ref[rows, :] + _rms(st["acc"], gpost_ref[...])

    return [norm] + [hidden_block(j) for j in range(0, D_FF, MLP_BLOCK)] + [post]


def _mlp_halves(x_ref, wup_ref, wdown_ref, gpre_ref, gpost_ref, y_ref):
    half = x_ref.shape[0] // 2
    args = (wup_ref, wdown_ref, gpre_ref, gpost_ref, y_ref)
    return _staggered(_mlp_stages(x_ref, slice(0, half), *args), _mlp_stages(x_ref, slice(half, 2 * half), *args))


def _mlp_kernel(x_ref, wup_ref, wdown_ref, gpre_ref, gpost_ref, y_ref):
    for stage in _mlp_halves(x_ref, wup_ref, wdown_ref, gpre_ref, gpost_ref, y_ref):
        stage()


def _inproj_kernel(x_ref, wmain_ref, wga_ref, w2_ref, b2_ref, gpre_ref, z_ref, la_ref):
    h_bf = _rms(x_ref[...], gpre_ref[...]).astype(BF16)
    _inproj(h_bf, wmain_ref, wga_ref, w2_ref, b2_ref, z_ref, la_ref)


def _mixer_sample_body(z_ref, la_ref, cos_ref, sin_ref, dm_ref, sgi_ref, sri_ref, ggla_ref, gret_ref,
                       merged_ref, sgo_ref, sro_ref, og_ref, or_ref, nb, ls, between=()):
    def gla_put(ci, h, val):
        sgo_ref[ci, h] = val

    def ret_put(ci, h, val):
        sro_ref[ci, h] = val

    _gla_tile(z_ref, la_ref, nb * ls, ls, lambda ci, h: sgi_ref[ci, h], gla_put, ggla_ref, og_ref, between=between)
    _ret_tile(z_ref, cos_ref[...], sin_ref[...], nb * ls, ls, lambda ci, h: sri_ref[ci, h], ret_put,
              gret_ref, or_ref, dm_ref, between=between)
    _merge(z_ref, og_ref, or_ref, merged_ref)


def _xattn_prompt_mixer_sample_kernel(x_ref, mk_ref, mv_ref, wq_ref, wo_ref, gpre_ref, gpost_ref,
                                      z_ref, la_ref, cos_ref, sin_ref, dm_ref, sgi_ref, sri_ref, ggla_ref, gret_ref,
                                      wup_f_ref, wdown_f_ref,
                                      y_ref, merged_ref, sgo_ref, sro_ref, wup_b_ref, wdown_b_ref,
                                      o_ref, og_ref, or_ref, *, nb, ls):
    wup_b_ref[...] = wup_f_ref[...].astype(BF16)
    wdown_b_ref[...] = wdown_f_ref[...].astype(BF16)
    half = x_ref.shape[0] // 2
    args = (mk_ref, mv_ref, wq_ref, wo_ref, gpre_ref, gpost_ref, y_ref, o_ref)
    pending = _staggered(_xattn_stages(x_ref, slice(0, half), *args),
                         _xattn_stages(x_ref, slice(half, 2 * half), *args))
    _mixer_sample_body(z_ref, la_ref, cos_ref, sin_ref, dm_ref, sgi_ref, sri_ref, ggla_ref, gret_ref,
                       merged_ref, sgo_ref, sro_ref, og_ref, or_ref, nb, ls, between=pending)
    while pending:
        _run_next(pending)


def _post_mix_sample_kernel(a_ref, res_ref, wmix_ref, gpost_ref, gpre_xa_ref, wq_ref, x1_ref, q_ref):
    x1 = res_ref[...] + _rms(_mm(a_ref[...], wmix_ref[...]), gpost_ref[...])
    x1_ref[...] = x1
    q_ref[...] = _mm(_rms(x1, gpre_xa_ref[...]).astype(BF16), wq_ref[...])


def _post_xa_mlp_sample_kernel(o_ref, x1_ref, wo_ref, gpost_xa_ref, wup_ref, wdown_ref, gpre_ref, gpost_ref,
                               y_ref, x2_ref):
    x2_ref[...] = x1_ref[...] + _rms(_mm(o_ref[...], wo_ref[...]), gpost_xa_ref[...])
    _mlp_kernel(x2_ref, wup_ref, wdown_ref, gpre_ref, gpost_ref, y_ref)


def _xattn_sample_body(q_ref, k_ref, v_ref, o_ref, nb, ls, between=()):
    n_rows = N_HEADS * ls
    n_cols = N_MEM * N_HEADS
    row_head = lax.broadcasted_iota(jnp.int32, (n_rows, n_cols), 0) // ls
    col_head = lax.broadcasted_iota(jnp.int32, (n_rows, n_cols), 1) & (N_HEADS - 1)
    own = row_head == col_head

    for e in range(nb):
        rows = slice(e * ls, (e + 1) * ls)
        q = q_ref[rows, :]
        q_hm = jnp.concatenate([q[:, h * XA_DH:(h + 1) * XA_DH] for h in range(N_HEADS)], axis=0)
        k_all = k_ref[e].reshape(n_cols, XA_DH)
        v_all = v_ref[e].reshape(n_cols, XA_DH)
        s = _mm_nt(q_hm.astype(BF16), k_all.astype(BF16)) * (XA_DH ** -0.5)
        s = jnp.where(own, s, -jnp.inf)
        p = jnp.exp(s - jnp.max(s, axis=-1, keepdims=True))
        p = p * (1.0 / jnp.sum(p, axis=-1, keepdims=True))
        _run_next(between)
        o = _mm(p.astype(BF16), v_all.astype(BF16))
        for h in range(N_HEADS):
            o_ref[rows, h * XA_DH:(h + 1) * XA_DH] = o[h * ls:(h + 1) * ls, :].astype(o_ref.dtype)
        _run_next(between)
        _run_next(between)


def _mlp_xattn_sample_kernel(x_ref, wup_ref, wdown_ref, gpre_ref, gpost_ref, q_ref, k_ref, v_ref,
                             y_ref, o_ref, *, nb, ls):
    pending = _mlp_halves(x_ref, wup_ref, wdown_ref, gpre_ref, gpost_ref, y_ref)
    _xattn_sample_body(q_ref, k_ref, v_ref, o_ref, nb, ls, between=pending)
    while pending:
        _run_next(pending)


def _rope_tables(pos):
    half = D_K // 2
    inv = ROPE_BASE ** (-jnp.arange(half, dtype=F32) / half)
    ang = pos.astype(F32)[:, None] * inv[None, :]
    cos, sin = jnp.cos(ang), jnp.sin(ang)
    return jnp.concatenate([cos, cos], axis=-1), jnp.concatenate([-sin, sin], axis=-1)


def _decay_masks(n, c):
    t = jnp.arange(n, dtype=jnp.int32)
    keep = ((t[:, None] // c) == (t[None, :] // c)) & (t[:, None] >= t[None, :])
    dist = (t[:, None] - t[None, :]).astype(F32)
    lg = jnp.asarray(LOG_GAMMA, F32)[:, None, None]
    return jnp.where(keep[None], jnp.exp(lg * dist[None]), 0.0)


def _row_spec(tm, n):
    return pl.BlockSpec((tm, n), lambda i: (i, 0))


def kernel(x_prompt, x_sample, state_gla, state_ret, cache_mem_k, cache_mem_v, mem_prompt, w_in, w_gla_a2, b_gla_a, g_gla_head, g_ret_head, w_mix_out, w_xq, w_xk, w_xv, w_xo, g_mem, w_up, w_down, g_pre_mix, g_post_mix, g_pre_xa, g_post_xa, g_pre_ffn, g_post_ffn):
    depth = w_in.shape[0]
    assert depth == 1
    bp, lp, _ = x_prompt.shape
    bs, ls, _ = x_sample.shape
    tl = 256
    tm = 512
    ts_tile = 256
    assert lp % (2 * tl) == 0 and tl % CHUNK == 0 and lp % tm == 0
    n_mlp_steps = bp * lp // tm
    assert bs % n_mlp_steps == 0
    nb_mix = bs // n_mlp_steps
    assert (bs * ls) % ts_tile == 0 and (nb_mix * ls) % 16 == 0
    assert ls % 8 == 0 and ls <= CHUNK and ls & (ls - 1) == 0 and tl & (tl - 1) == 0
    assert (bp * N_MEM) % tm == 0

    w = w_in[0]
    c_low = 2 * N_QK + 2 * D_MODEL
    assert c_low == N_WA and w.shape[1] == N_MAIN + GATE_RANK
    wt = jnp.transpose(w)
    n_wa_blocks = N_WA // SPLIT_BLOCK

    def wt_row(i):
        units = SPLIT_BLOCK // GATE_RANK
        return GATE_RANK * jnp.where(i < n_wa_blocks, i * units, i * units + 1)

    wmain, wga = pl.pallas_call(
        _split_w_in_kernel,
        grid=(N_MAIN // SPLIT_BLOCK,),
        in_specs=[pl.BlockSpec((pl.Element(SPLIT_BLOCK), pl.Element(D_MODEL)), lambda i: (wt_row(i), 0)),
                  pl.BlockSpec((GATE_RANK, D_MODEL), lambda i: (N_WA // GATE_RANK, 0))],
        out_specs=[pl.BlockSpec((D_MODEL, SPLIT_BLOCK), lambda i: (0, i)),
                   pl.BlockSpec((D_MODEL, GATE_RANK_PAD), lambda i: (0, 0))],
        out_shape=[jax.ShapeDtypeStruct((D_MODEL, N_MAIN), BF16),
                   jax.ShapeDtypeStruct((D_MODEL, GATE_RANK_PAD), BF16)],
        compiler_params=_params(("arbitrary",)),
        name="split_w_in",
    )(wt, wt)
    w2 = jnp.pad(w_gla_a2[0], ((0, GATE_RANK_PAD - GATE_RANK), (0, 0))).astype(BF16)
    b2 = b_gla_a[0].reshape(1, N_QK)
    ggla = g_gla_head[0].reshape(1, D_MODEL)
    gret = g_ret_head[0].reshape(1, D_MODEL)
    wmix = w_mix_out[0].astype(BF16)
    wxq, wxk, wxv, wxo = (t[0].astype(BF16) for t in (w_xq, w_xk, w_xv, w_xo))
    row = lambda g: g[0].reshape(1, D_MODEL)
    gmem, gpre_mix, gpost_mix, gpre_xa, gpost_xa, gpre_ffn, gpost_ffn = (
        row(g) for g in (g_mem, g_pre_mix, g_post_mix, g_pre_xa, g_post_xa, g_pre_ffn, g_post_ffn))
    cos_p, sin_p = _rope_tables(jnp.arange(lp, dtype=jnp.int32))
    cos_s, sin_s = _rope_tables(PAST_LEN + jnp.arange(ls, dtype=jnp.int32))
    rows_mix = nb_mix * ls
    cos_s, sin_s = jnp.tile(cos_s, (nb_mix, 1)), jnp.tile(sin_s, (nb_mix, 1))
    dm_p = _decay_masks(tl, tl)
    dm_s = _decay_masks(rows_mix, ls)

    nmem_rows = bp * N_MEM
    mk, mv, mk_bf, mv_bf = pl.pallas_call(
        _memkv_kernel,
        grid=(nmem_rows // tm,),
        in_specs=[_row_spec(tm, D_MODEL), _const_spec((1, D_MODEL)),
                  _const_spec((D_MODEL, D_MODEL)), _const_spec((D_MODEL, D_MODEL))],
        out_specs=[pl.BlockSpec((tm, N_HEADS, XA_DH), lambda i: (i, 0, 0))] * 2 + [_row_spec(tm, D_MODEL)] * 2,
        out_shape=[jax.ShapeDtypeStruct((nmem_rows, N_HEADS, XA_DH), F32)] * 2
        + [jax.ShapeDtypeStruct((nmem_rows, D_MODEL), BF16)] * 2,
        compiler_params=_params(("arbitrary",)),
        name="memkv",
    )(mem_prompt.reshape(nmem_rows, D_MODEL), gmem, wxk, wxv)

    n_tiles = bp * lp // tl
    steps_per_row = lp // (2 * tl)
    state_spec = pl.BlockSpec((1, N_HEADS, D_K, D_V), lambda s: (s // steps_per_row, 0, 0, 0))
    rope_spec = pl.BlockSpec((2 * tl, D_K), lambda s: (s % steps_per_row, 0))
    x1p, sg_p, sr_p = pl.pallas_call(
        functools.partial(_mixer_prompt_kernel, tl=tl, steps_per_row=steps_per_row),
        grid=(n_tiles // 2,),
        in_specs=[_row_spec(2 * tl, D_MODEL),
                  pl.BlockSpec((tl, D_MODEL), lambda s: (jnp.minimum(2 * s + 2, n_tiles - 1), 0)),
                  rope_spec, rope_spec,
                  _const_spec((N_HEADS, tl, tl)),
                  _const_spec((D_MODEL, N_MAIN)), _const_spec((D_MODEL, GATE_RANK_PAD)),
                  _const_spec((GATE_RANK_PAD, N_QK)), _const_spec((1, N_QK)),
                  _const_spec((1, D_MODEL)), _const_spec((1, D_MODEL)),
                  _const_spec((D_MODEL, D_MODEL)), _const_spec((1, D_MODEL)), _const_spec((1, D_MODEL))],
        out_specs=[_row_spec(2 * tl, D_MODEL), state_spec, state_spec],
        out_shape=[jax.ShapeDtypeStruct((bp * lp, D_MODEL), F32),
                   jax.ShapeDtypeStruct((bp, N_HEADS, D_K, D_V), F32),
                   jax.ShapeDtypeStruct((bp, N_HEADS, D_K, D_V), F32)],
        scratch_shapes=[pltpu.VMEM((2, tl, N_MAIN), F32), pltpu.VMEM((2, tl, N_QK), F32),
                        pltpu.VMEM((tl, D_MODEL), F32), pltpu.VMEM((tl, D_MODEL), F32),
                        pltpu.VMEM((tl, D_MODEL), BF16)],
        compiler_params=_params(("arbitrary",)),
        name="mixer_prompt",
    )(x_prompt.reshape(bp * lp, D_MODEL), x_prompt.reshape(bp * lp, D_MODEL), cos_p, sin_p, dm_p,
      wmain, wga, w2, b2, ggla, gret, wmix, gpre_mix, gpost_mix)
    x1p = x1p.reshape(bp, lp, D_MODEL)

    ts = bs * ls
    xs = x_sample.reshape(ts, D_MODEL)
    z_s, la_s = pl.pallas_call(
        _inproj_kernel,
        grid=(ts // ts_tile,),
        in_specs=[_row_spec(ts_tile, D_MODEL), _const_spec((D_MODEL, N_MAIN)),
                  _const_spec((D_MODEL, GATE_RANK_PAD)),
                  _const_spec((GATE_RANK_PAD, N_QK)), _const_spec((1, N_QK)), _const_spec((1, D_MODEL))],
        out_specs=[_row_spec(ts_tile, N_MAIN), _row_spec(ts_tile, N_QK)],
        out_shape=[jax.ShapeDtypeStruct((ts, N_MAIN), F32), jax.ShapeDtypeStruct((ts, N_QK), F32)],
        compiler_params=_params(("arbitrary",)),
        name="inproj_sample",
    )(xs, wmain, wga, w2, b2, gpre_mix)

    xa_per_row = lp // tm
    mem_spec = pl.BlockSpec((1, N_MEM, D_MODEL), lambda s: (s // xa_per_row, 0, 0))
    st_spec = pl.BlockSpec((nb_mix, N_HEADS, D_K, D_V), lambda i: (i, 0, 0, 0))
    assert D_MODEL % (16 * n_mlp_steps) == 0
    wup_spec = _row_spec(D_MODEL // n_mlp_steps, D_FF)
    wdown_spec = _row_spec(D_FF // n_mlp_steps, D_MODEL)
    x2p, merged_s, sg_s, sr_s, wup, wdown = pl.pallas_call(
        functools.partial(_xattn_prompt_mixer_sample_kernel, nb=nb_mix, ls=ls),
        grid=(n_mlp_steps,),
        in_specs=[_row_spec(tm, D_MODEL), mem_spec, mem_spec,
                  _const_spec((D_MODEL, D_MODEL)), _const_spec((D_MODEL, D_MODEL)),
                  _const_spec((1, D_MODEL)), _const_spec((1, D_MODEL)),
                  _row_spec(rows_mix, N_MAIN), _row_spec(rows_mix, N_QK),
                  _const_spec((rows_mix, D_K)), _const_spec((rows_mix, D_K)),
                  _const_spec((N_HEADS, rows_mix, rows_mix)), st_spec, st_spec,
                  _const_spec((1, D_MODEL)), _const_spec((1, D_MODEL)),
                  wup_spec, wdown_spec],
        out_specs=[_row_spec(tm, D_MODEL), _row_spec(rows_mix, D_MODEL), st_spec, st_spec, wup_spec, wdown_spec],
        out_shape=[jax.ShapeDtypeStruct((bp * lp, D_MODEL), F32),
                   jax.ShapeDtypeStruct((ts, D_MODEL), BF16),
                   jax.ShapeDtypeStruct((bs, N_HEADS, D_K, D_V), F32),
                   jax.ShapeDtypeStruct((bs, N_HEADS, D_K, D_V), F32),
                   jax.ShapeDtypeStruct((D_MODEL, D_FF), BF16),
                   jax.ShapeDtypeStruct((D_FF, D_MODEL), BF16)],
        scratch_shapes=[pltpu.VMEM((tm, D_MODEL), BF16),
                        pltpu.VMEM((rows_mix, D_MODEL), F32), pltpu.VMEM((rows_mix, D_MODEL), F32)],
        compiler_params=_params(("arbitrary",)),
        name="xattn_prompt_mixer_sample",
    )(x1p.reshape(bp * lp, D_MODEL), mk_bf.reshape(bp, N_MEM, D_MODEL), mv_bf.reshape(bp, N_MEM, D_MODEL),
      wxq, wxo, gpre_xa, gpost_xa,
      z_s, la_s, cos_s, sin_s, dm_s, state_gla[0], state_ret[0], ggla, gret, w_up[0], w_down[0])

    x1s, q_s = pl.pallas_call(
        _post_mix_sample_kernel,
        grid=(ts // ts_tile,),
        in_specs=[_row_spec(ts_tile, D_MODEL), _row_spec(ts_tile, D_MODEL), _const_spec((D_MODEL, D_MODEL)),
                  _const_spec((1, D_MODEL)), _const_spec((1, D_MODEL)), _const_spec((D_MODEL, D_MODEL))],
        out_specs=[_row_spec(ts_tile, D_MODEL)] * 2,
        out_shape=[jax.ShapeDtypeStruct((ts, D_MODEL), F32)] * 2,
        compiler_params=_params(("arbitrary",)),
        name="post_mix_sample",
    )(merged_s, xs, wmix, gpost_mix, gpre_xa, wxq)

    kv_spec = pl.BlockSpec((nb_mix, N_MEM, N_HEADS, XA_DH), lambda i: (i, 0, 0, 0))
    yp, o_s = pl.pallas_call(
        functools.partial(_mlp_xattn_sample_kernel, nb=nb_mix, ls=ls),
        grid=(n_mlp_steps,),
        in_specs=[_row_spec(tm, D_MODEL), _const_spec((D_MODEL, D_FF)), _const_spec((D_FF, D_MODEL)),
                  _const_spec((1, D_MODEL)), _const_spec((1, D_MODEL)),
                  _row_spec(rows_mix, D_MODEL), kv_spec, kv_spec],
        out_specs=[_row_spec(tm, D_MODEL), _row_spec(rows_mix, D_MODEL)],
        out_shape=[jax.ShapeDtypeStruct((bp * lp, D_MODEL), F32), jax.ShapeDtypeStruct((ts, D_MODEL), BF16)],
        compiler_params=_params(("arbitrary",)),
        name="mlp_prompt_xattn_sample",
    )(x2p, wup, wdown, gpre_ffn, gpost_ffn, q_s, cache_mem_k[0], cache_mem_v[0])
    yp = yp.reshape(bp, lp, D_MODEL)

    ys = pl.pallas_call(
        _post_xa_mlp_sample_kernel,
        grid=(ts // ts_tile,),
        in_specs=[_row_spec(ts_tile, D_MODEL), _row_spec(ts_tile, D_MODEL), _const_spec((D_MODEL, D_MODEL)),
                  _const_spec((1, D_MODEL)), _const_spec((D_MODEL, D_FF)), _const_spec((D_FF, D_MODEL)),
                  _const_spec((1, D_MODEL)), _const_spec((1, D_MODEL))],
        out_specs=_row_spec(ts_tile, D_MODEL),
        out_shape=jax.ShapeDtypeStruct((ts, D_MODEL), F32),
        scratch_shapes=[pltpu.VMEM((ts_tile, D_MODEL), F32)],
        compiler_params=_params(("arbitrary",)),
        name="post_xa_mlp_sample",
    )(o_s, x1s, wxo, gpost_xa, wup, wdown, gpre_ffn, gpost_ffn).reshape(bs, ls, D_MODEL)

    hshape = (1, bp, N_MEM, N_HEADS, XA_DH)
    return (yp, ys, sg_p[None], sr_p[None], mk.reshape(hshape), mv.reshape(hshape), sg_s[None], sr_s[None])
```

```python
import functools
import math

import jax
import jax.numpy as jnp
from jax import lax
from jax.experimental import pallas as pl
from jax.experimental.pallas import tpu as pltpu

F32 = jnp.float32
BF16 = jnp.bfloat16

D_MODEL = 1024
N_HEADS = 4
D_K = 128
D_V = 256
N_MEM = 256
XA_DH = 256
D_FF = 4 * D_MODEL
GATE_RANK = 16
GATE_RANK_PAD = 128
GLA_TAU = 16.0
CHUNK = 64
ROPE_BASE = 10000.0
PAST_LEN = 16384
EPS = 1e-6
LOG_GAMMA = tuple(math.log1p(-(2.0 ** (-5.0 - h))) for h in range(N_HEADS))

C_GQ, C_GK, C_GV, C_GR = 0, 512, 1024, 2048
C_RQ, C_RK, C_RV, C_RG = 3072, 3584, 4096, 5120
C_GA, C_GB = 6144, 7168
N_MAIN = 8192
N_WA = 3072
N_WB = N_MAIN - N_WA
N_QK = N_HEADS * D_K
PROJ_BLOCK = 512
MLP_BLOCK = 1024
SPLIT_BLOCK = 1024

LANE = 128
VMEM_LIMIT_BYTES = 60 * 1024 * 1024


def _mm(a, b):
    return jnp.dot(a, b, preferred_element_type=F32)


def _mm_nt(a, b):
    return lax.dot_general(a, b, (((1,), (1,)), ((), ())), preferred_element_type=F32)


def _mm_tn(a, b):
    return lax.dot_general(a, b, (((0,), (0,)), ((), ())), preferred_element_type=F32)


def _rms(x, g):
    return x * lax.rsqrt(jnp.mean(x * x, axis=-1, keepdims=True) + EPS) * g


def _sigmoid(x):
    return 1.0 / (1.0 + jnp.exp(-x))


def _const_spec(shape):
    nd = len(shape)
    return pl.BlockSpec(shape, lambda *_: (0,) * nd, pipeline_mode=pl.Buffered(1))


def _params(sem):
    return pltpu.CompilerParams(dimension_semantics=sem, vmem_limit_bytes=VMEM_LIMIT_BYTES)


def _inproj_pieces(h_bf, wmain_ref, wga_ref, w2_ref, b2_ref, z_ref, la_ref):
    def z_block(j):
        def run():
            z_ref[:, j:j + PROJ_BLOCK] = _mm(h_bf, wmain_ref[:, j:j + PROJ_BLOCK])
        return run

    def gate():
        ga = _mm(h_bf, wga_ref[...])
        xg = _mm(ga.astype(BF16), w2_ref[...]) + b2_ref[...]
        log_sig = jnp.minimum(xg, 0.0) - jnp.log1p(jnp.exp(-jnp.abs(xg)))
        la_ref[...] = log_sig * (1.0 / GLA_TAU)

    return [z_block(j) for j in range(0, N_MAIN, PROJ_BLOCK)] + [gate]


def _inproj(h_bf, wmain_ref, wga_ref, w2_ref, b2_ref, z_ref, la_ref):
    for piece in _inproj_pieces(h_bf, wmain_ref, wga_ref, w2_ref, b2_ref, z_ref, la_ref):
        piece()


def _run_next(pieces):
    if pieces:
        pieces.pop(0)()


def _block_causal(n, c):
    shift = c.bit_length() - 1
    row = lax.broadcasted_iota(jnp.int32, (n, n), 0)
    col = lax.broadcasted_iota(jnp.int32, (n, n), 1)
    return ((row >> shift) == (col >> shift)) & (row >= col)


def _gla_tile(z_ref, la_ref, n, c, s_get, s_put, g_ref, o_ref, between=()):
    causal = _block_causal(n, c)
    tri = jnp.where(causal, 1.0, 0.0).astype(BF16)
    la = la_ref[...]
    la_hi = la.astype(BF16)
    la_lo = (la - la_hi.astype(F32)).astype(BF16)
    b = _mm(tri, la_hi) + _mm(tri, la_lo)
    n_chunks = n // c
    b_last = [b[(ci + 1) * c - 1:(ci + 1) * c, :] for ci in range(n_chunks)]
    for h in range(N_HEADS):
        ks = slice(h * D_K, (h + 1) * D_K)
        q = z_ref[:, C_GQ + h * D_K:C_GQ + (h + 1) * D_K] * (D_K ** -0.5)
        k = z_ref[:, C_GK + h * D_K:C_GK + (h + 1) * D_K]
        v = z_ref[:, C_GV + h * D_V:C_GV + (h + 1) * D_V].astype(BF16)
        b_h = b[:, ks]
        b_last_h = jnp.concatenate([jnp.broadcast_to(bl[:, ks], (c, D_K)) for bl in b_last], axis=0)
        qe = (q * jnp.exp(b_h)).astype(BF16)
        ke = (k * jnp.exp(-b_h)).astype(BF16)
        kd = (k * jnp.exp(b_last_h - b_h)).astype(BF16)
        sc = jnp.where(causal, _mm_nt(qe, ke), 0.0).astype(BF16)
        o_intra = _mm(sc, v)
        _run_next(between)
        outs = []
        for ci in range(n_chunks):
            r = slice(ci * c, (ci + 1) * c)
            s_old = s_get(ci, h)
            outs.append(o_intra[r] + _mm(qe[r], s_old.astype(BF16)))
            dcol = jnp.transpose(jnp.broadcast_to(jnp.exp(b_last[ci][:, ks]), (D_K, D_K)))
            s_put(ci, h, jnp.concatenate([dcol, dcol], axis=1) * s_old + _mm_tn(kd[r], v[r]))
        o = jnp.concatenate(outs, axis=0) if n_chunks > 1 else outs[0]
        ms = jnp.mean(o * o, axis=-1, keepdims=True)
        o_ref[:, h * D_V:(h + 1) * D_V] = o * lax.rsqrt(ms + EPS) * g_ref[:, h * D_V:(h + 1) * D_V]
        _run_next(between)


def _ret_tile(z_ref, cos, sin, n, c, s_get, s_put, g_ref, o_ref, dm_ref, between=()):
    tpos = (lax.broadcasted_iota(jnp.int32, (n, 1), 0) & (c - 1)).astype(F32)
    n_chunks = n // c
    for h in range(N_HEADS):
        lg = LOG_GAMMA[h]
        q = z_ref[:, C_RQ + h * D_K:C_RQ + (h + 1) * D_K]
        k = z_ref[:, C_RK + h * D_K:C_RK + (h + 1) * D_K]
        v = z_ref[:, C_RV + h * D_V:C_RV + (h + 1) * D_V].astype(BF16)
        q = q * cos + pltpu.roll(q, D_K // 2, 1) * sin
        k = (k * cos + pltpu.roll(k, D_K // 2, 1) * sin) * (D_K ** -0.5)
        sc = (_mm_nt(q.astype(BF16), k.astype(BF16)) * dm_ref[h]).astype(BF16)
        o_intra = _mm(sc, v)
        _run_next(between)
        qd = (q * jnp.exp(lg * (tpos + 1.0))).astype(BF16)
        kd = (k * jnp.exp(lg * (float(c - 1) - tpos))).astype(BF16)
        outs = []
        for ci in range(n_chunks):
            r = slice(ci * c, (ci + 1) * c)
            s_old = s_get(ci, h)
            outs.append(o_intra[r] + _mm(qd[r], s_old.astype(BF16)))
            s_put(ci, h, math.exp(lg * c) * s_old + _mm_tn(kd[r], v[r]))
        o = jnp.concatenate(outs, axis=0) if n_chunks > 1 else outs[0]
        mu = jnp.mean(o, axis=-1, keepdims=True)
        oc = o - mu
        var = jnp.mean(oc * oc, axis=-1, keepdims=True)
        o_ref[:, h * D_V:(h + 1) * D_V] = oc * lax.rsqrt(var + EPS) * g_ref[:, h * D_V:(h + 1) * D_V]
        _run_next(between)


def _merge(z_ref, og_ref, or_ref, merged_ref):
    for j in range(0, D_MODEL, 256):
        cs = slice(j, j + 256)
        gr = z_ref[:, C_GR + j:C_GR + j + 256]
        rg = z_ref[:, C_RG + j:C_RG + j + 256]
        ga = z_ref[:, C_GA + j:C_GA + j + 256]
        gb = z_ref[:, C_GB + j:C_GB + j + 256]
        o_g = og_ref[:, cs] * (gr * _sigmoid(gr))
        o_r = or_ref[:, cs] * (rg * _sigmoid(rg))
        merged_ref[:, cs] = (_sigmoid(ga) * o_g + _sigmoid(gb) * o_r).astype(merged_ref.dtype)


def _split_w_in_kernel(wt_ref, wtg_ref, wmix_f_ref, w_ref, wga_ref, wmix_b_ref):
    wmix_b_ref[...] = wmix_f_ref[...].astype(BF16)
    w_ref[...] = jnp.transpose(wt_ref[...]).astype(BF16)
    zeros = jnp.zeros((GATE_RANK_PAD - GATE_RANK, D_MODEL), F32)
    wga_ref[...] = jnp.transpose(jnp.concatenate([wtg_ref[...], zeros], axis=0)).astype(BF16)


def _memkv_kernel(m_ref, g_ref, wk_ref, wv_ref, k_ref, v_ref, kb_ref, vb_ref):
    m = _rms(m_ref[...], g_ref[...]).astype(BF16)
    k = _mm(m, wk_ref[...])
    v = _mm(m, wv_ref[...])
    k_ref[...] = k.reshape(k_ref.shape)
    v_ref[...] = v.reshape(v_ref.shape)
    kb_ref[...] = k.astype(BF16)
    vb_ref[...] = v.astype(BF16)


def _mixer_prompt_kernel(xc_ref, xn_ref, cos_ref, sin_ref, dm_ref, wmain_ref, wga_ref, w2_ref, b2_ref,
                         ggla_ref, gret_ref, wmix_ref, gpre_ref, gpost_ref,
                         y_ref, sg_ref, sr_ref,
                         z_ref, la_ref, og_ref, or_ref, merged_ref, *, tl, steps_per_row):
    step = pl.program_id(0)

    @pl.when(step % steps_per_row == 0)
    def _():
        sg_ref[...] = jnp.zeros_like(sg_ref)
        sr_ref[...] = jnp.zeros_like(sr_ref)

    def project_pieces(x, slot):
        h_bf = _rms(x, gpre_ref[...]).astype(BF16)
        return _inproj_pieces(h_bf, wmain_ref, wga_ref, w2_ref, b2_ref, z_ref.at[slot], la_ref.at[slot])

    @pl.when(step == 0)
    def _():
        for piece in project_pieces(xc_ref[0:tl, :], 0):
            piece()

    n_chunks = tl // CHUNK
    for half in range(2):
        rows = slice(half * tl, (half + 1) * tl)
        pending = project_pieces(xc_ref[tl:2 * tl, :] if half == 0 else xn_ref[...], 1 - half)
        _run_next(pending)
        z_cur, la_cur = z_ref.at[half], la_ref.at[half]

        carried = {}

        def gla_get(ci, h, carried=carried):
            return sg_ref[0, h] if ci == 0 else carried[h]

        def gla_put(ci, h, val, carried=carried):
            carried[h] = val
            if ci == n_chunks - 1:
                sg_ref[0, h] = val

        _gla_tile(z_cur, la_cur, tl, CHUNK, gla_get, gla_put, ggla_ref, og_ref, between=pending)

        def ret_put(ci, h, val):
            sr_ref[0, h] = val

        _ret_tile(z_cur, cos_ref[rows, :], sin_ref[rows, :], tl, tl, lambda ci, h: sr_ref[0, h], ret_put,
                  gret_ref, or_ref, dm_ref, between=pending)
        while pending:
            _run_next(pending)

        _merge(z_cur, og_ref, or_ref, merged_ref)
        m = _mm(merged_ref[...], wmix_ref[...])
        y_ref[rows, :] = xc_ref[rows, :] + _rms(m, gpost_ref[...])


def _staggered(a_stages, b_stages):
    a, b = list(a_stages), list(b_stages)
    order = [a.pop(0)]
    while a or b:
        order += a[:1] + b[:1]
        a, b = a[1:], b[1:]
    return order


def _xattn_stages(x_ref, rows, mk_ref, mv_ref, wq_ref, wo_ref, gpre_ref, gpost_ref, y_ref, o_ref):
    st = {}

    def norm():
        st["hx"] = _rms(x_ref[rows, :], gpre_ref[...]).astype(BF16)

    def q_proj():
        st["q"] = _mm(st["hx"], wq_ref[...])

    def heads():
        for h in range(N_HEADS):
            hs = slice(h * XA_DH, (h + 1) * XA_DH)
            s = _mm_nt(st["q"][:, hs].astype(BF16), mk_ref[0, :, hs]) * (XA_DH ** -0.5)
            p = jnp.exp(s - jnp.max(s, axis=-1, keepdims=True))
            p = p * (1.0 / jnp.sum(p, axis=-1, keepdims=True))
            o_ref[rows, hs] = _mm(p.astype(BF16), mv_ref[0, :, hs]).astype(o_ref.dtype)

    def out_proj():
        st["a"] = _mm(o_ref[rows, :], wo_ref[...])

    def post():
        y_ref[rows, :] = x_ref[rows, :] + _rms(st["a"], gpost_ref[...])

    return [norm, q_proj, heads, out_proj, post]


def _mlp_stages(x_ref, rows, wup_ref, wdown_ref, gpre_ref, gpost_ref, y_ref):
    st = {}

    def norm():
        st["h"] = _rms(x_ref[rows, :], gpre_ref[...]).astype(BF16)

    def hidden_block(j):
        def run():
            u = jnp.maximum(_mm(st["h"], wup_ref[:, j:j + MLP_BLOCK]), 0.0)
            part = _mm((u * u).astype(BF16), wdown_ref[j:j + MLP_BLOCK, :])
            st["acc"] = part if j == 0 else st["acc"] + part
        return run

    def post():
        y_ref[rows, :] = x_ref[rows, :] + _rms(st["acc"], gpost_ref[...])

    return [norm] + [hidden_block(j) for j in range(0, D_FF, MLP_BLOCK)] + [post]


def _mlp_halves(x_ref, wup_ref, wdown_ref, gpre_ref, gpost_ref, y_ref):
    half = x_ref.shape[0] // 2
    args = (wup_ref, wdown_ref, gpre_ref, gpost_ref, y_ref)
    return _staggered(_mlp_stages(x_ref, slice(0, half), *args), _mlp_stages(x_ref, slice(half, 2 * half), *args))


def _mlp_kernel(x_ref, wup_ref, wdown_ref, gpre_ref, gpost_ref, y_ref):
    for stage in _mlp_halves(x_ref, wup_ref, wdown_ref, gpre_ref, gpost_ref, y_ref):
        stage()


def _inproj_kernel(x_ref, wmain_ref, wga_ref, w2_ref, b2_ref, gpre_ref, wq_f, wk_f, wv_f, wo_f,
                   z_ref, la_ref, wq_b, wk_b, wv_b, wo_b):
    for src, dst in ((wq_f, wq_b), (wk_f, wk_b), (wv_f, wv_b), (wo_f, wo_b)):
        dst[...] = src[...].astype(BF16)
    h_bf = _rms(x_ref[...], gpre_ref[...]).astype(BF16)
    _inproj(h_bf, wmain_ref, wga_ref, w2_ref, b2_ref, z_ref, la_ref)


def _mixer_sample_body(z_ref, la_ref, cos_ref, sin_ref, dm_ref, sgi_ref, sri_ref, ggla_ref, gret_ref,
                       merged_ref, sgo_ref, sro_ref, og_ref, or_ref, nb, ls, between=()):
    def gla_put(ci, h, val):
        sgo_ref[ci, h] = val

    def ret_put(ci, h, val):
        sro_ref[ci, h] = val

    _gla_tile(z_ref, la_ref, nb * ls, ls, lambda ci, h: sgi_ref[ci, h], gla_put, ggla_ref, og_ref, between=between)
    _ret_tile(z_ref, cos_ref[...], sin_ref[...], nb * ls, ls, lambda ci, h: sri_ref[ci, h], ret_put,
              gret_ref, or_ref, dm_ref, between=between)
    _merge(z_ref, og_ref, or_ref, merged_ref)


def _xattn_prompt_mixer_sample_kernel(x_ref, mk_ref, mv_ref, wq_ref, wo_ref, gpre_ref, gpost_ref,
                                      z_ref, la_ref, cos_ref, sin_ref, dm_ref, sgi_ref, sri_ref, ggla_ref, gret_ref,
                                      wup_f_ref, wdown_f_ref,
                                      y_ref, merged_ref, sgo_ref, sro_ref, wup_b_ref, wdown_b_ref,
                                      o_ref, og_ref, or_ref, *, nb, ls):
    wup_b_ref[...] = wup_f_ref[...].astype(BF16)
    wdown_b_ref[...] = wdown_f_ref[...].astype(BF16)
    half = x_ref.shape[0] // 2
    args = (mk_ref, mv_ref, wq_ref, wo_ref, gpre_ref, gpost_ref, y_ref, o_ref)
    pending = _staggered(_xattn_stages(x_ref, slice(0, half), *args),
                         _xattn_stages(x_ref, slice(half, 2 * half), *args))
    _mixer_sample_body(z_ref, la_ref, cos_ref, sin_ref, dm_ref, sgi_ref, sri_ref, ggla_ref, gret_ref,
                       merged_ref, sgo_ref, sro_ref, og_ref, or_ref, nb, ls, between=pending)
    while pending:
        _run_next(pending)


def _post_mix_sample_kernel(a_ref, res_ref, wmix_ref, gpost_ref, gpre_xa_ref, wq_ref, x1_ref, q_ref):
    x1 = res_ref[...] + _rms(_mm(a_ref[...], wmix_ref[...]), gpost_ref[...])
    x1_ref[...] = x1
    q_ref[...] = _mm(_rms(x1, gpre_xa_ref[...]).astype(BF16), wq_ref[...])


def _post_xa_mlp_sample_kernel(o_ref, x1_ref, wo_ref, gpost_xa_ref, wup_ref, wdown_ref, gpre_ref, gpost_ref,
                               y_ref, x2_ref):
    x2_ref[...] = x1_ref[...] + _rms(_mm(o_ref[...], wo_ref[...]), gpost_xa_ref[...])
    _mlp_kernel(x2_ref, wup_ref, wdown_ref, gpre_ref, gpost_ref, y_ref)


def _xattn_sample_body(q_ref, k_ref, v_ref, o_ref, nb, ls, between=()):
    n_rows = N_HEADS * ls
    n_cols = N_MEM * N_HEADS
    row_head = lax.broadcasted_iota(jnp.int32, (n_rows, n_cols), 0) // ls
    col_head = lax.broadcasted_iota(jnp.int32, (n_rows, n_cols), 1) & (N_HEADS - 1)
    own = row_head == col_head

    for e in range(nb):
        rows = slice(e * ls, (e + 1) * ls)
        q = q_ref[rows, :]
        q_hm = jnp.concatenate([q[:, h * XA_DH:(h + 1) * XA_DH] for h in range(N_HEADS)], axis=0)
        k_all = k_ref[e].reshape(n_cols, XA_DH)
        v_all = v_ref[e].reshape(n_cols, XA_DH)
        s = _mm_nt(q_hm.astype(BF16), k_all.astype(BF16)) * (XA_DH ** -0.5)
        s = jnp.where(own, s, -jnp.inf)
        p = jnp.exp(s - jnp.max(s, axis=-1, keepdims=True))
        p = p * (1.0 / jnp.sum(p, axis=-1, keepdims=True))
        _run_next(between)
        o = _mm(p.astype(BF16), v_all.astype(BF16))
        for h in range(N_HEADS):
            o_ref[rows, h * XA_DH:(h + 1) * XA_DH] = o[h * ls:(h + 1) * ls, :].astype(o_ref.dtype)
        _run_next(between)
        _run_next(between)


def _mlp_xattn_sample_kernel(x_ref, wup_ref, wdown_ref, gpre_ref, gpost_ref, q_ref, k_ref, v_ref,
                             y_ref, o_ref, *, nb, ls):
    pending = _mlp_halves(x_ref, wup_ref, wdown_ref, gpre_ref, gpost_ref, y_ref)
    _xattn_sample_body(q_ref, k_ref, v_ref, o_ref, nb, ls, between=pending)
    while pending:
        _run_next(pending)


def _rope_tables(pos):
    half = D_K // 2
    inv = ROPE_BASE ** (-jnp.arange(half, dtype=F32) / half)
    ang = pos.astype(F32)[:, None] * inv[None, :]
    cos, sin = jnp.cos(ang), jnp.sin(ang)
    return jnp.concatenate([cos, cos], axis=-1), jnp.concatenate([-sin, sin], axis=-1)


def _decay_masks(n, c):
    t = jnp.arange(n, dtype=jnp.int32)
    keep = ((t[:, None] // c) == (t[None, :] // c)) & (t[:, None] >= t[None, :])
    dist = (t[:, None] - t[None, :]).astype(F32)
    lg = jnp.asarray(LOG_GAMMA, F32)[:, None, None]
    return jnp.where(keep[None], jnp.exp(lg * dist[None]), 0.0)


def _row_spec(tm, n):
    return pl.BlockSpec((tm, n), lambda i: (i, 0))


def kernel(x_prompt, x_sample, state_gla, state_ret, cache_mem_k, cache_mem_v, mem_prompt, w_in, w_gla_a2, b_gla_a, g_gla_head, g_ret_head, w_mix_out, w_xq, w_xk, w_xv, w_xo, g_mem, w_up, w_down, g_pre_mix, g_post_mix, g_pre_xa, g_post_xa, g_pre_ffn, g_post_ffn):
    depth = w_in.shape[0]
    assert depth == 1
    bp, lp, _ = x_prompt.shape
    bs, ls, _ = x_sample.shape
    tl = 256
    tm = 512
    ts_tile = 256
    assert lp % (2 * tl) == 0 and tl % CHUNK == 0 and lp % tm == 0
    n_mlp_steps = bp * lp // tm
    assert bs % n_mlp_steps == 0
    nb_mix = bs // n_mlp_steps
    assert (bs * ls) % ts_tile == 0 and (nb_mix * ls) % 16 == 0
    assert ls % 8 == 0 and ls <= CHUNK and ls & (ls - 1) == 0 and tl & (tl - 1) == 0
    assert (bp * N_MEM) % tm == 0

    w = w_in[0]
    c_low = 2 * N_QK + 2 * D_MODEL
    assert c_low == N_WA and w.shape[1] == N_MAIN + GATE_RANK
    wt = jnp.transpose(w)
    n_wa_blocks = N_WA // SPLIT_BLOCK

    def wt_row(i):
        units = SPLIT_BLOCK // GATE_RANK
        return GATE_RANK * jnp.where(i < n_wa_blocks, i * units, i * units + 1)

    n_split = N_MAIN // SPLIT_BLOCK
    wmix_spec = _row_spec(D_MODEL // n_split, D_MODEL)
    wmain, wga, wmix = pl.pallas_call(
        _split_w_in_kernel,
        grid=(n_split,),
        in_specs=[pl.BlockSpec((pl.Element(SPLIT_BLOCK), pl.Element(D_MODEL)), lambda i: (wt_row(i), 0)),
                  pl.BlockSpec((GATE_RANK, D_MODEL), lambda i: (N_WA // GATE_RANK, 0)), wmix_spec],
        out_specs=[pl.BlockSpec((D_MODEL, SPLIT_BLOCK), lambda i: (0, i)),
                   pl.BlockSpec((D_MODEL, GATE_RANK_PAD), lambda i: (0, 0)), wmix_spec],
        out_shape=[jax.ShapeDtypeStruct((D_MODEL, N_MAIN), BF16),
                   jax.ShapeDtypeStruct((D_MODEL, GATE_RANK_PAD), BF16),
                   jax.ShapeDtypeStruct((D_MODEL, D_MODEL), BF16)],
        compiler_params=_params(("arbitrary",)),
        name="split_w_in",
    )(wt, wt, w_mix_out[0])
    w2 = jnp.pad(w_gla_a2[0], ((0, GATE_RANK_PAD - GATE_RANK), (0, 0))).astype(BF16)
    b2 = b_gla_a[0].reshape(1, N_QK)
    ggla = g_gla_head[0].reshape(1, D_MODEL)
    gret = g_ret_head[0].reshape(1, D_MODEL)
    row = lambda g: g[0].reshape(1, D_MODEL)
    gmem, gpre_mix, gpost_mix, gpre_xa, gpost_xa, gpre_ffn, gpost_ffn = (
        row(g) for g in (g_mem, g_pre_mix, g_post_mix, g_pre_xa, g_post_xa, g_pre_ffn, g_post_ffn))
    cos_p, sin_p = _rope_tables(jnp.arange(lp, dtype=jnp.int32))
    cos_s, sin_s = _rope_tables(PAST_LEN + jnp.arange(ls, dtype=jnp.int32))
    rows_mix = nb_mix * ls
    cos_s, sin_s = jnp.tile(cos_s, (nb_mix, 1)), jnp.tile(sin_s, (nb_mix, 1))
    dm_p = _decay_masks(tl, tl)
    dm_s = _decay_masks(rows_mix, ls)

    ts = bs * ls
    xs = x_sample.reshape(ts, D_MODEL)
    n_inproj = ts // ts_tile
    assert D_MODEL % (16 * n_inproj) == 0
    wx_spec = _row_spec(D_MODEL // n_inproj, D_MODEL)
    z_s, la_s, wxq, wxk, wxv, wxo = pl.pallas_call(
        _inproj_kernel,
        grid=(n_inproj,),
        in_specs=[_row_spec(ts_tile, D_MODEL), _const_spec((D_MODEL, N_MAIN)),
                  _const_spec((D_MODEL, GATE_RANK_PAD)),
                  _const_spec((GATE_RANK_PAD, N_QK)), _const_spec((1, N_QK)), _const_spec((1, D_MODEL))]
        + [wx_spec] * 4,
        out_specs=[_row_spec(ts_tile, N_MAIN), _row_spec(ts_tile, N_QK)] + [wx_spec] * 4,
        out_shape=[jax.ShapeDtypeStruct((ts, N_MAIN), F32), jax.ShapeDtypeStruct((ts, N_QK), F32)]
        + [jax.ShapeDtypeStruct((D_MODEL, D_MODEL), BF16)] * 4,
        compiler_params=_params(("arbitrary",)),
        name="inproj_sample",
    )(xs, wmain, wga, w2, b2, gpre_mix, w_xq[0], w_xk[0], w_xv[0], w_xo[0])

    nmem_rows = bp * N_MEM
    mk, mv, mk_bf, mv_bf = pl.pallas_call(
        _memkv_kernel,
        grid=(nmem_rows // tm,),
        in_specs=[_row_spec(tm, D_MODEL), _const_spec((1, D_MODEL)),
                  _const_spec((D_MODEL, D_MODEL)), _const_spec((D_MODEL, D_MODEL))],
        out_specs=[pl.BlockSpec((tm, N_HEADS, XA_DH), lambda i: (i, 0, 0))] * 2 + [_row_spec(tm, D_MODEL)] * 2,
        out_shape=[jax.ShapeDtypeStruct((nmem_rows, N_HEADS, XA_DH), F32)] * 2
        + [jax.ShapeDtypeStruct((nmem_rows, D_MODEL), BF16)] * 2,
        compiler_params=_params(("arbitrary",)),
        name="memkv",
    )(mem_prompt.reshape(nmem_rows, D_MODEL), gmem, wxk, wxv)

    n_tiles = bp * lp // tl
    steps_per_row = lp // (2 * tl)
    state_spec = pl.BlockSpec((1, N_HEADS, D_K, D_V), lambda s: (s // steps_per_row, 0, 0, 0))
    rope_spec = pl.BlockSpec((2 * tl, D_K), lambda s: (s % steps_per_row, 0))
    x1p, sg_p, sr_p = pl.pallas_call(
        functools.partial(_mixer_prompt_kernel, tl=tl, steps_per_row=steps_per_row),
        grid=(n_tiles // 2,),
        in_specs=[_row_spec(2 * tl, D_MODEL),
                  pl.BlockSpec((tl, D_MODEL), lambda s: (jnp.minimum(2 * s + 2, n_tiles - 1), 0)),
                  rope_spec, rope_spec,
                  _const_spec((N_HEADS, tl, tl)),
                  _const_spec((D_MODEL, N_MAIN)), _const_spec((D_MODEL, GATE_RANK_PAD)),
                  _const_spec((GATE_RANK_PAD, N_QK)), _const_spec((1, N_QK)),
                  _const_spec((1, D_MODEL)), _const_spec((1, D_MODEL)),
                  _const_spec((D_MODEL, D_MODEL)), _const_spec((1, D_MODEL)), _const_spec((1, D_MODEL))],
        out_specs=[_row_spec(2 * tl, D_MODEL), state_spec, state_spec],
        out_shape=[jax.ShapeDtypeStruct((bp * lp, D_MODEL), F32),
                   jax.ShapeDtypeStruct((bp, N_HEADS, D_K, D_V), F32),
                   jax.ShapeDtypeStruct((bp, N_HEADS, D_K, D_V), F32)],
        scratch_shapes=[pltpu.VMEM((2, tl, N_MAIN), F32), pltpu.VMEM((2, tl, N_QK), F32),
                        pltpu.VMEM((tl, D_MODEL), F32), pltpu.VMEM((tl, D_MODEL), F32),
                        pltpu.VMEM((tl, D_MODEL), BF16)],
        compiler_params=_params(("arbitrary",)),
        name="mixer_prompt",
    )(x_prompt.reshape(bp * lp, D_MODEL), x_prompt.reshape(bp * lp, D_MODEL), cos_p, sin_p, dm_p,
      wmain, wga, w2, b2, ggla, gret, wmix, gpre_mix, gpost_mix)
    x1p = x1p.reshape(bp, lp, D_MODEL)

    xa_per_row = lp // tm
    mem_spec = pl.BlockSpec((1, N_MEM, D_MODEL), lambda s: (s // xa_per_row, 0, 0))
    st_spec = pl.BlockSpec((nb_mix, N_HEADS, D_K, D_V), lambda i: (i, 0, 0, 0))
    assert D_MODEL % (16 * n_mlp_steps) == 0
    wup_spec = _row_spec(D_MODEL // n_mlp_steps, D_FF)
    wdown_spec = _row_spec(D_FF // n_mlp_steps, D_MODEL)
    x2p, merged_s, sg_s, sr_s, wup, wdown = pl.pallas_call(
        functools.partial(_xattn_prompt_mixer_sample_kernel, nb=nb_mix, ls=ls),
        grid=(n_mlp_steps,),
        in_specs=[_row_spec(tm, D_MODEL), mem_spec, mem_spec,
                  _const_spec((D_MODEL, D_MODEL)), _const_spec((D_MODEL, D_MODEL)),
                  _const_spec((1, D_MODEL)), _const_spec((1, D_MODEL)),
                  _row_spec(rows_mix, N_MAIN), _row_spec(rows_mix, N_QK),
                  _const_spec((rows_mix, D_K)), _const_spec((rows_mix, D_K)),
                  _const_spec((N_HEADS, rows_mix, rows_mix)), st_spec, st_spec,
                  _const_spec((1, D_MODEL)), _const_spec((1, D_MODEL)),
                  wup_spec, wdown_spec],
        out_specs=[_row_spec(tm, D_MODEL), _row_spec(rows_mix, D_MODEL), st_spec, st_spec, wup_spec, wdown_spec],
        out_shape=[jax.ShapeDtypeStruct((bp * lp, D_MODEL), F32),
                   jax.ShapeDtypeStruct((ts, D_MODEL), BF16),
                   jax.ShapeDtypeStruct((bs, N_HEADS, D_K, D_V), F32),
                   jax.ShapeDtypeStruct((bs, N_HEADS, D_K, D_V), F32),
                   jax.ShapeDtypeStruct((D_MODEL, D_FF), BF16),
                   jax.ShapeDtypeStruct((D_FF, D_MODEL), BF16)],
        scratch_shapes=[pltpu.VMEM((tm, D_MODEL), BF16),
                        pltpu.VMEM((rows_mix, D_MODEL), F32), pltpu.VMEM((rows_mix, D_MODEL), F32)],
        compiler_params=_params(("arbitrary",)),
        name="xattn_prompt_mixer_sample",
    )(x1p.reshape(bp * lp, D_MODEL), mk_bf.reshape(bp, N_MEM, D_MODEL), mv_bf.reshape(bp, N_MEM, D_MODEL),
      wxq, wxo, gpre_xa, gpost_xa,
      z_s, la_s, cos_s, sin_s, dm_s, state_gla[0], state_ret[0], ggla, gret, w_up[0], w_down[0])

    x1s, q_s = pl.pallas_call(
        _post_mix_sample_kernel,
        grid=(ts // ts_tile,),
        in_specs=[_row_spec(ts_tile, D_MODEL), _row_spec(ts_tile, D_MODEL), _const_spec((D_MODEL, D_MODEL)),
                  _const_spec((1, D_MODEL)), _const_spec((1, D_MODEL)), _const_spec((D_MODEL, D_MODEL))],
        out_specs=[_row_spec(ts_tile, D_MODEL)] * 2,
        out_shape=[jax.ShapeDtypeStruct((ts, D_MODEL), F32)] * 2,
        compiler_params=_params(("arbitrary",)),
        name="post_mix_sample",
    )(merged_s, xs, wmix, gpost_mix, gpre_xa, wxq)

    kv_spec = pl.BlockSpec((nb_mix, N_MEM, N_HEADS, XA_DH), lambda i: (i, 0, 0, 0))
    yp, o_s = pl.pallas_call(
        functools.partial(_mlp_xattn_sample_kernel, nb=nb_mix, ls=ls),
        grid=(n_mlp_steps,),
        in_specs=[_row_spec(tm, D_MODEL), _const_spec((D_MODEL, D_FF)), _const_spec((D_FF, D_MODEL)),
                  _const_spec((1, D_MODEL)), _const_spec((1, D_MODEL)),
                  _row_spec(rows_mix, D_MODEL), kv_spec, kv_spec],
        out_specs=[_row_spec(tm, D_MODEL), _row_spec(rows_mix, D_MODEL)],
        out_shape=[jax.ShapeDtypeStruct((bp * lp, D_MODEL), F32), jax.ShapeDtypeStruct((ts, D_MODEL), BF16)],
        compiler_params=_params(("arbitrary",)),
        name="mlp_prompt_xattn_sample",
    )(x2p, wup, wdown, gpre_ffn, gpost_ffn, q_s, cache_mem_k[0], cache_mem_v[0])
    yp = yp.reshape(bp, lp, D_MODEL)

    ys = pl.pallas_call(
        _post_xa_mlp_sample_kernel,
        grid=(ts // ts_tile,),
        in_specs=[_row_spec(ts_tile, D_MODEL), _row_spec(ts_tile, D_MODEL), _const_spec((D_MODEL, D_MODEL)),
                  _const_spec((1, D_MODEL)), _const_spec((D_MODEL, D_FF)), _const_spec((D_FF, D_MODEL)),
                  _const_spec((1, D_MODEL)), _const_spec((1, D_MODEL))],
        out_specs=_row_spec(ts_tile, D_MODEL),
        out_shape=jax.ShapeDtypeStruct((ts, D_MODEL), F32),
        scratch_shapes=[pltpu.VMEM((ts_tile, D_MODEL), F32)],
        compiler_params=_params(("arbitrary",)),
        name="post_xa_mlp_sample",
    )(o_s, x1s, wxo, gpost_xa, wup, wdown, gpre_ffn, gpost_ffn).reshape(bs, ls, D_MODEL)

    hshape = (1, bp, N_MEM, N_HEADS, XA_DH)
    return (yp, ys, sg_p[None], sr_p[None], mk.reshape(hshape), mv.reshape(hshape), sg_s[None], sr_s[None])
```

```python
import functools
import math

import jax
import jax.numpy as jnp
from jax import lax
from jax.experimental import pallas as pl
from jax.experimental.pallas import tpu as pltpu

F32 = jnp.float32
BF16 = jnp.bfloat16

D_MODEL = 1024
N_HEADS = 4
D_K = 128
D_V = 256
N_MEM = 256
XA_DH = 256
D_FF = 4 * D_MODEL
GATE_RANK = 16
GATE_RANK_PAD = 128
GLA_TAU = 16.0
CHUNK = 64
ROPE_BASE = 10000.0
PAST_LEN = 16384
EPS = 1e-6
LOG_GAMMA = tuple(math.log1p(-(2.0 ** (-5.0 - h))) for h in range(N_HEADS))

C_GQ, C_GK, C_GV, C_GR = 0, 512, 1024, 2048
C_RQ, C_RK, C_RV, C_RG = 3072, 3584, 4096, 5120
C_GA, C_GB = 6144, 7168
N_MAIN = 8192
N_WA = 3072
N_WB = N_MAIN - N_WA
N_QK = N_HEADS * D_K
PROJ_BLOCK = 512
MLP_BLOCK = 1024
SPLIT_BLOCK = 1024

LANE = 128
VMEM_LIMIT_BYTES = 60 * 1024 * 1024


def _mm(a, b):
    return jnp.dot(a, b, preferred_element_type=F32)


def _mm_nt(a, b):
    return lax.dot_general(a, b, (((1,), (1,)), ((), ())), preferred_element_type=F32)


def _mm_tn(a, b):
    return lax.dot_general(a, b, (((0,), (0,)), ((), ())), preferred_element_type=F32)


def _rms(x, g):
    return x * lax.rsqrt(jnp.mean(x * x, axis=-1, keepdims=True) + EPS) * g


def _sigmoid(x):
    return 1.0 / (1.0 + jnp.exp(-x))


def _const_spec(shape):
    nd = len(shape)
    return pl.BlockSpec(shape, lambda *_: (0,) * nd, pipeline_mode=pl.Buffered(1))


def _params(sem):
    return pltpu.CompilerParams(dimension_semantics=sem, vmem_limit_bytes=VMEM_LIMIT_BYTES)


def _inproj_pieces(h_bf, wmain_ref, wga_ref, w2_ref, b2_ref, z_ref, la_ref):
    def z_block(j):
        def run():
            z_ref[:, j:j + PROJ_BLOCK] = _mm(h_bf, wmain_ref[:, j:j + PROJ_BLOCK])
        return run

    def gate():
        ga = _mm(h_bf, wga_ref[...])
        xg = _mm(ga.astype(BF16), w2_ref[...]) + b2_ref[...]
        log_sig = jnp.minimum(xg, 0.0) - jnp.log1p(jnp.exp(-jnp.abs(xg)))
        la_ref[...] = log_sig * (1.0 / GLA_TAU)

    return [z_block(j) for j in range(0, N_MAIN, PROJ_BLOCK)] + [gate]


def _inproj(h_bf, wmain_ref, wga_ref, w2_ref, b2_ref, z_ref, la_ref):
    for piece in _inproj_pieces(h_bf, wmain_ref, wga_ref, w2_ref, b2_ref, z_ref, la_ref):
        piece()


def _run_next(pieces):
    if pieces:
        pieces.pop(0)()


def _block_causal(n, c):
    shift = c.bit_length() - 1
    row = lax.broadcasted_iota(jnp.int32, (n, n), 0)
    col = lax.broadcasted_iota(jnp.int32, (n, n), 1)
    return ((row >> shift) == (col >> shift)) & (row >= col)


def _gla_tile(z_ref, la_ref, n, c, s_get, s_put, g_ref, o_ref, between=()):
    causal = _block_causal(n, c)
    tri = jnp.where(causal, 1.0, 0.0).astype(BF16)
    la = la_ref[...]
    la_hi = la.astype(BF16)
    la_lo = (la - la_hi.astype(F32)).astype(BF16)
    b = _mm(tri, la_hi) + _mm(tri, la_lo)
    n_chunks = n // c
    b_last = [b[(ci + 1) * c - 1:(ci + 1) * c, :] for ci in range(n_chunks)]
    for h in range(N_HEADS):
        ks = slice(h * D_K, (h + 1) * D_K)
        q = z_ref[:, C_GQ + h * D_K:C_GQ + (h + 1) * D_K] * (D_K ** -0.5)
        k = z_ref[:, C_GK + h * D_K:C_GK + (h + 1) * D_K]
        v = z_ref[:, C_GV + h * D_V:C_GV + (h + 1) * D_V].astype(BF16)
        b_h = b[:, ks]
        b_last_h = jnp.concatenate([jnp.broadcast_to(bl[:, ks], (c, D_K)) for bl in b_last], axis=0)
        qe = (q * jnp.exp(b_h)).astype(BF16)
        ke = (k * jnp.exp(-b_h)).astype(BF16)
        kd = (k * jnp.exp(b_last_h - b_h)).astype(BF16)
        sc = jnp.where(causal, _mm_nt(qe, ke), 0.0).astype(BF16)
        o_intra = _mm(sc, v)
        _run_next(between)
        outs = []
        for ci in range(n_chunks):
            r = slice(ci * c, (ci + 1) * c)
            s_old = s_get(ci, h)
            outs.append(o_intra[r] + _mm(qe[r], s_old.astype(BF16)))
            dcol = jnp.transpose(jnp.broadcast_to(jnp.exp(b_last[ci][:, ks]), (D_K, D_K)))
            s_put(ci, h, jnp.concatenate([dcol, dcol], axis=1) * s_old + _mm_tn(kd[r], v[r]))
        o = jnp.concatenate(outs, axis=0) if n_chunks > 1 else outs[0]
        ms = jnp.mean(o * o, axis=-1, keepdims=True)
        o_ref[:, h * D_V:(h + 1) * D_V] = o * lax.rsqrt(ms + EPS) * g_ref[:, h * D_V:(h + 1) * D_V]
        _run_next(between)


def _ret_tile(z_ref, cos, sin, n, c, s_get, s_put, g_ref, o_ref, dm_ref, between=()):
    tpos = (lax.broadcasted_iota(jnp.int32, (n, 1), 0) & (c - 1)).astype(F32)
    n_chunks = n // c
    for h in range(N_HEADS):
        lg = LOG_GAMMA[h]
        q = z_ref[:, C_RQ + h * D_K:C_RQ + (h + 1) * D_K]
        k = z_ref[:, C_RK + h * D_K:C_RK + (h + 1) * D_K]
        v = z_ref[:, C_RV + h * D_V:C_RV + (h + 1) * D_V].astype(BF16)
        q = q * cos + pltpu.roll(q, D_K // 2, 1) * sin
        k = (k * cos + pltpu.roll(k, D_K // 2, 1) * sin) * (D_K ** -0.5)
        sc = (_mm_nt(q.astype(BF16), k.astype(BF16)) * dm_ref[h]).astype(BF16)
        o_intra = _mm(sc, v)
        _run_next(between)
        qd = (q * jnp.exp(lg * (tpos + 1.0))).astype(BF16)
        kd = (k * jnp.exp(lg * (float(c - 1) - tpos))).astype(BF16)
        outs = []
        for ci in range(n_chunks):
            r = slice(ci * c, (ci + 1) * c)
            s_old = s_get(ci, h)
            outs.append(o_intra[r] + _mm(qd[r], s_old.astype(BF16)))
            s_put(ci, h, math.exp(lg * c) * s_old + _mm_tn(kd[r], v[r]))
        o = jnp.concatenate(outs, axis=0) if n_chunks > 1 else outs[0]
        mu = jnp.mean(o, axis=-1, keepdims=True)
        oc = o - mu
        var = jnp.mean(oc * oc, axis=-1, keepdims=True)
        o_ref[:, h * D_V:(h + 1) * D_V] = oc * lax.rsqrt(var + EPS) * g_ref[:, h * D_V:(h + 1) * D_V]
        _run_next(between)


def _merge(z_ref, og_ref, or_ref, merged_ref):
    for j in range(0, D_MODEL, 256):
        cs = slice(j, j + 256)
        gr = z_ref[:, C_GR + j:C_GR + j + 256]
        rg = z_ref[:, C_RG + j:C_RG + j + 256]
        ga = z_ref[:, C_GA + j:C_GA + j + 256]
        gb = z_ref[:, C_GB + j:C_GB + j + 256]
        o_g = og_ref[:, cs] * (gr * _sigmoid(gr))
        o_r = or_ref[:, cs] * (rg * _sigmoid(rg))
        merged_ref[:, cs] = (_sigmoid(ga) * o_g + _sigmoid(gb) * o_r).astype(merged_ref.dtype)


def _split_w_in_kernel(wt_ref, wtg_ref, w_ref, wga_ref):
    w_ref[...] = jnp.transpose(wt_ref[...]).astype(BF16)
    zeros = jnp.zeros((GATE_RANK_PAD - GATE_RANK, D_MODEL), F32)
    wga_ref[...] = jnp.transpose(jnp.concatenate([wtg_ref[...], zeros], axis=0)).astype(BF16)


def _mixer_prompt_kernel(xc_ref, xn_ref, cos_ref, sin_ref, dm_ref, wmain_ref, wga_ref, w2_ref, b2_ref,
                         ggla_ref, gret_ref, wmix_ref, gpre_ref, gpost_ref,
                         y_ref, sg_ref, sr_ref,
                         z_ref, la_ref, og_ref, or_ref, merged_ref, *, tl, steps_per_row):
    step = pl.program_id(0)

    @pl.when(step % steps_per_row == 0)
    def _():
        sg_ref[...] = jnp.zeros_like(sg_ref)
        sr_ref[...] = jnp.zeros_like(sr_ref)

    def project_pieces(x, slot):
        h_bf = _rms(x, gpre_ref[...]).astype(BF16)
        return _inproj_pieces(h_bf, wmain_ref, wga_ref, w2_ref, b2_ref, z_ref.at[slot], la_ref.at[slot])

    @pl.when(step == 0)
    def _():
        for piece in project_pieces(xc_ref[0:tl, :], 0):
            piece()

    n_chunks = tl // CHUNK
    for half in range(2):
        rows = slice(half * tl, (half + 1) * tl)
        pending = project_pieces(xc_ref[tl:2 * tl, :] if half == 0 else xn_ref[...], 1 - half)
        _run_next(pending)
        z_cur, la_cur = z_ref.at[half], la_ref.at[half]

        carried = {}

        def gla_get(ci, h, carried=carried):
            return sg_ref[0, h] if ci == 0 else carried[h]

        def gla_put(ci, h, val, carried=carried):
            carried[h] = val
            if ci == n_chunks - 1:
                sg_ref[0, h] = val

        _gla_tile(z_cur, la_cur, tl, CHUNK, gla_get, gla_put, ggla_ref, og_ref, between=pending)

        def ret_put(ci, h, val):
            sr_ref[0, h] = val

        _ret_tile(z_cur, cos_ref[rows, :], sin_ref[rows, :], tl, tl, lambda ci, h: sr_ref[0, h], ret_put,
                  gret_ref, or_ref, dm_ref, between=pending)
        while pending:
            _run_next(pending)

        _merge(z_cur, og_ref, or_ref, merged_ref)
        m = _mm(merged_ref[...], wmix_ref[...])
        y_ref[rows, :] = xc_ref[rows, :] + _rms(m, gpost_ref[...])


def _staggered(a_stages, b_stages):
    a, b = list(a_stages), list(b_stages)
    order = [a.pop(0)]
    while a or b:
        order += a[:1] + b[:1]
        a, b = a[1:], b[1:]
    return order


def _xattn_stages(x_ref, rows, mk_ref, mv_ref, wq_ref, wo_ref, gpre_ref, gpost_ref, y_ref, o_ref):
    st = {}

    def norm():
        st["hx"] = _rms(x_ref[rows, :], gpre_ref[...]).astype(BF16)

    def q_proj():
        st["q"] = _mm(st["hx"], wq_ref[...])

    def heads():
        for h in range(N_HEADS):
            hs = slice(h * XA_DH, (h + 1) * XA_DH)
            s = _mm_nt(st["q"][:, hs].astype(BF16), mk_ref[0, :, hs]) * (XA_DH ** -0.5)
            p = jnp.exp(s - jnp.max(s, axis=-1, keepdims=True))
            p = p * (1.0 / jnp.sum(p, axis=-1, keepdims=True))
            o_ref[rows, hs] = _mm(p.astype(BF16), mv_ref[0, :, hs]).astype(o_ref.dtype)

    def out_proj():
        st["a"] = _mm(o_ref[rows, :], wo_ref[...])

    def post():
        y_ref[rows, :] = x_ref[rows, :] + _rms(st["a"], gpost_ref[...])

    return [norm, q_proj, heads, out_proj, post]


def _mlp_stages(x_ref, rows, wup_ref, wdown_ref, gpre_ref, gpost_ref, y_ref):
    st = {}

    def norm():
        st["h"] = _rms(x_ref[rows, :], gpre_ref[...]).astype(BF16)

    def hidden_block(j):
        def run():
            u = jnp.maximum(_mm(st["h"], wup_ref[:, j:j + MLP_BLOCK]), 0.0)
            part = _mm((u * u).astype(BF16), wdown_ref[j:j + MLP_BLOCK, :])
            st["acc"] = part if j == 0 else st["acc"] + part
        return run

    def post():
        y_ref[rows, :] = x_ref[rows, :] + _rms(st["acc"], gpost_ref[...])

    return [norm] + [hidden_block(j) for j in range(0, D_FF, MLP_BLOCK)] + [post]


def _mlp_halves(x_ref, wup_ref, wdown_ref, gpre_ref, gpost_ref, y_ref):
    half = x_ref.shape[0] // 2
    args = (wup_ref, wdown_ref, gpre_ref, gpost_ref, y_ref)
    return _staggered(_mlp_stages(x_ref, slice(0, half), *args), _mlp_stages(x_ref, slice(half, 2 * half), *args))


def _mlp_kernel(x_ref, wup_ref, wdown_ref, gpre_ref, gpost_ref, y_ref):
    for stage in _mlp_halves(x_ref, wup_ref, wdown_ref, gpre_ref, gpost_ref, y_ref):
        stage()


def _inproj_kernel(x_ref, wmain_ref, wga_ref, w2_ref, b2_ref, gpre_ref, z_ref, la_ref):
    h_bf = _rms(x_ref[...], gpre_ref[...]).astype(BF16)
    _inproj(h_bf, wmain_ref, wga_ref, w2_ref, b2_ref, z_ref, la_ref)


def _mixer_sample_body(z_ref, la_ref, cos_ref, sin_ref, dm_ref, sgi_ref, sri_ref, ggla_ref, gret_ref,
                       merged_ref, sgo_ref, sro_ref, og_ref, or_ref, nb, ls, between=()):
    def gla_put(ci, h, val):
        sgo_ref[ci, h] = val

    def ret_put(ci, h, val):
        sro_ref[ci, h] = val

    _gla_tile(z_ref, la_ref, nb * ls, ls, lambda ci, h: sgi_ref[ci, h], gla_put, ggla_ref, og_ref, between=between)
    _ret_tile(z_ref, cos_ref[...], sin_ref[...], nb * ls, ls, lambda ci, h: sri_ref[ci, h], ret_put,
              gret_ref, or_ref, dm_ref, between=between)
    _merge(z_ref, og_ref, or_ref, merged_ref)


def _xattn_prompt_mixer_sample_kernel(x_ref, mem_ref, gmem_ref, wk_ref, wv_ref, wq_ref, wo_ref, gpre_ref, gpost_ref,
                                      z_ref, la_ref, cos_ref, sin_ref, dm_ref, sgi_ref, sri_ref, ggla_ref, gret_ref,
                                      wup_f_ref, wdown_f_ref,
                                      y_ref, mk_ref, mv_ref, merged_ref, sgo_ref, sro_ref, wup_b_ref, wdown_b_ref,
                                      mkb_ref, mvb_ref, o_ref, og_ref, or_ref, *, nb, ls, steps_per_row):
    @pl.when(pl.program_id(0) % steps_per_row == 0)
    def _():
        m = _rms(mem_ref[0], gmem_ref[...]).astype(BF16)
        k = _mm(m, wk_ref[...])
        v = _mm(m, wv_ref[...])
        mk_ref[...] = k.reshape(mk_ref.shape)
        mv_ref[...] = v.reshape(mv_ref.shape)
        mkb_ref[0] = k.astype(BF16)
        mvb_ref[0] = v.astype(BF16)

    wup_b_ref[...] = wup_f_ref[...].astype(BF16)
    wdown_b_ref[...] = wdown_f_ref[...].astype(BF16)
    half = x_ref.shape[0] // 2
    args = (mkb_ref, mvb_ref, wq_ref, wo_ref, gpre_ref, gpost_ref, y_ref, o_ref)
    pending = _staggered(_xattn_stages(x_ref, slice(0, half), *args),
                         _xattn_stages(x_ref, slice(half, 2 * half), *args))
    _mixer_sample_body(z_ref, la_ref, cos_ref, sin_ref, dm_ref, sgi_ref, sri_ref, ggla_ref, gret_ref,
                       merged_ref, sgo_ref, sro_ref, og_ref, or_ref, nb, ls, between=pending)
    while pending:
        _run_next(pending)


def _post_mix_sample_kernel(a_ref, res_ref, wmix_ref, gpost_ref, gpre_xa_ref, wq_ref, x1_ref, q_ref):
    x1 = res_ref[...] + _rms(_mm(a_ref[...], wmix_ref[...]), gpost_ref[...])
    x1_ref[...] = x1
    q_ref[...] = _mm(_rms(x1, gpre_xa_ref[...]).astype(BF16), wq_ref[...])


def _post_xa_mlp_sample_kernel(o_ref, x1_ref, wo_ref, gpost_xa_ref, wup_ref, wdown_ref, gpre_ref, gpost_ref,
                               y_ref, x2_ref):
    x2_ref[...] = x1_ref[...] + _rms(_mm(o_ref[...], wo_ref[...]), gpost_xa_ref[...])
    _mlp_kernel(x2_ref, wup_ref, wdown_ref, gpre_ref, gpost_ref, y_ref)


def _xattn_sample_body(q_ref, k_ref, v_ref, o_ref, nb, ls, between=()):
    n_rows = N_HEADS * ls
    n_cols = N_MEM * N_HEADS
    row_head = lax.broadcasted_iota(jnp.int32, (n_rows, n_cols), 0) // ls
    col_head = lax.broadcasted_iota(jnp.int32, (n_rows, n_cols), 1) & (N_HEADS - 1)
    own = row_head == col_head

    for e in range(nb):
        rows = slice(e * ls, (e + 1) * ls)
        q = q_ref[rows, :]
        q_hm = jnp.concatenate([q[:, h * XA_DH:(h + 1) * XA_DH] for h in range(N_HEADS)], axis=0)
        k_all = k_ref[e].reshape(n_cols, XA_DH)
        v_all = v_ref[e].reshape(n_cols, XA_DH)
        s = _mm_nt(q_hm.astype(BF16), k_all.astype(BF16)) * (XA_DH ** -0.5)
        s = jnp.where(own, s, -jnp.inf)
        p = jnp.exp(s - jnp.max(s, axis=-1, keepdims=True))
        p = p * (1.0 / jnp.sum(p, axis=-1, keepdims=True))
        _run_next(between)
        o = _mm(p.astype(BF16), v_all.astype(BF16))
        for h in range(N_HEADS):
            o_ref[rows, h * XA_DH:(h + 1) * XA_DH] = o[h * ls:(h + 1) * ls, :].astype(o_ref.dtype)
        _run_next(between)
        _run_next(between)


def _mlp_xattn_sample_kernel(x_ref, wup_ref, wdown_ref, gpre_ref, gpost_ref, q_ref, k_ref, v_ref,
                             y_ref, o_ref, *, nb, ls):
    pending = _mlp_halves(x_ref, wup_ref, wdown_ref, gpre_ref, gpost_ref, y_ref)
    _xattn_sample_body(q_ref, k_ref, v_ref, o_ref, nb, ls, between=pending)
    while pending:
        _run_next(pending)


def _rope_tables(pos):
    half = D_K // 2
    inv = ROPE_BASE ** (-jnp.arange(half, dtype=F32) / half)
    ang = pos.astype(F32)[:, None] * inv[None, :]
    cos, sin = jnp.cos(ang), jnp.sin(ang)
    return jnp.concatenate([cos, cos], axis=-1), jnp.concatenate([-sin, sin], axis=-1)


def _decay_masks(n, c):
    t = jnp.arange(n, dtype=jnp.int32)
    keep = ((t[:, None] // c) == (t[None, :] // c)) & (t[:, None] >= t[None, :])
    dist = (t[:, None] - t[None, :]).astype(F32)
    lg = jnp.asarray(LOG_GAMMA, F32)[:, None, None]
    return jnp.where(keep[None], jnp.exp(lg * dist[None]), 0.0)


def _row_spec(tm, n):
    return pl.BlockSpec((tm, n), lambda i: (i, 0))


def kernel(x_prompt, x_sample, state_gla, state_ret, cache_mem_k, cache_mem_v, mem_prompt, w_in, w_gla_a2, b_gla_a, g_gla_head, g_ret_head, w_mix_out, w_xq, w_xk, w_xv, w_xo, g_mem, w_up, w_down, g_pre_mix, g_post_mix, g_pre_xa, g_post_xa, g_pre_ffn, g_post_ffn):
    depth = w_in.shape[0]
    assert depth == 1
    bp, lp, _ = x_prompt.shape
    bs, ls, _ = x_sample.shape
    tl = 256
    tm = 512
    ts_tile = 256
    assert lp % (2 * tl) == 0 and tl % CHUNK == 0 and lp % tm == 0
    n_mlp_steps = bp * lp // tm
    assert bs % n_mlp_steps == 0
    nb_mix = bs // n_mlp_steps
    assert (bs * ls) % ts_tile == 0 and (nb_mix * ls) % 16 == 0
    assert ls % 8 == 0 and ls <= CHUNK and ls & (ls - 1) == 0 and tl & (tl - 1) == 0
    assert (bp * N_MEM) % tm == 0

    w = w_in[0]
    c_low = 2 * N_QK + 2 * D_MODEL
    assert c_low == N_WA and w.shape[1] == N_MAIN + GATE_RANK
    wt = jnp.transpose(w)
    n_wa_blocks = N_WA // SPLIT_BLOCK

    def wt_row(i):
        units = SPLIT_BLOCK // GATE_RANK
        return GATE_RANK * jnp.where(i < n_wa_blocks, i * units, i * units + 1)

    wmain, wga = pl.pallas_call(
        _split_w_in_kernel,
        grid=(N_MAIN // SPLIT_BLOCK,),
        in_specs=[pl.BlockSpec((pl.Element(SPLIT_BLOCK), pl.Element(D_MODEL)), lambda i: (wt_row(i), 0)),
                  pl.BlockSpec((GATE_RANK, D_MODEL), lambda i: (N_WA // GATE_RANK, 0))],
        out_specs=[pl.BlockSpec((D_MODEL, SPLIT_BLOCK), lambda i: (0, i)),
                   pl.BlockSpec((D_MODEL, GATE_RANK_PAD), lambda i: (0, 0))],
        out_shape=[jax.ShapeDtypeStruct((D_MODEL, N_MAIN), BF16),
                   jax.ShapeDtypeStruct((D_MODEL, GATE_RANK_PAD), BF16)],
        compiler_params=_params(("arbitrary",)),
        name="split_w_in",
    )(wt, wt)
    w2 = jnp.pad(w_gla_a2[0], ((0, GATE_RANK_PAD - GATE_RANK), (0, 0))).astype(BF16)
    b2 = b_gla_a[0].reshape(1, N_QK)
    ggla = g_gla_head[0].reshape(1, D_MODEL)
    gret = g_ret_head[0].reshape(1, D_MODEL)
    wmix = w_mix_out[0].astype(BF16)
    wxq, wxk, wxv, wxo = (t[0].astype(BF16) for t in (w_xq, w_xk, w_xv, w_xo))
    row = lambda g: g[0].reshape(1, D_MODEL)
    gmem, gpre_mix, gpost_mix, gpre_xa, gpost_xa, gpre_ffn, gpost_ffn = (
        row(g) for g in (g_mem, g_pre_mix, g_post_mix, g_pre_xa, g_post_xa, g_pre_ffn, g_post_ffn))
    cos_p, sin_p = _rope_tables(jnp.arange(lp, dtype=jnp.int32))
    cos_s, sin_s = _rope_tables(PAST_LEN + jnp.arange(ls, dtype=jnp.int32))
    rows_mix = nb_mix * ls
    cos_s, sin_s = jnp.tile(cos_s, (nb_mix, 1)), jnp.tile(sin_s, (nb_mix, 1))
    dm_p = _decay_masks(tl, tl)
    dm_s = _decay_masks(rows_mix, ls)

    n_tiles = bp * lp // tl
    steps_per_row = lp // (2 * tl)
    state_spec = pl.BlockSpec((1, N_HEADS, D_K, D_V), lambda s: (s // steps_per_row, 0, 0, 0))
    rope_spec = pl.BlockSpec((2 * tl, D_K), lambda s: (s % steps_per_row, 0))
    x1p, sg_p, sr_p = pl.pallas_call(
        functools.partial(_mixer_prompt_kernel, tl=tl, steps_per_row=steps_per_row),
        grid=(n_tiles // 2,),
        in_specs=[_row_spec(2 * tl, D_MODEL),
                  pl.BlockSpec((tl, D_MODEL), lambda s: (jnp.minimum(2 * s + 2, n_tiles - 1), 0)),
                  rope_spec, rope_spec,
                  _const_spec((N_HEADS, tl, tl)),
                  _const_spec((D_MODEL, N_MAIN)), _const_spec((D_MODEL, GATE_RANK_PAD)),
                  _const_spec((GATE_RANK_PAD, N_QK)), _const_spec((1, N_QK)),
                  _const_spec((1, D_MODEL)), _const_spec((1, D_MODEL)),
                  _const_spec((D_MODEL, D_MODEL)), _const_spec((1, D_MODEL)), _const_spec((1, D_MODEL))],
        out_specs=[_row_spec(2 * tl, D_MODEL), state_spec, state_spec],
        out_shape=[jax.ShapeDtypeStruct((bp * lp, D_MODEL), F32),
                   jax.ShapeDtypeStruct((bp, N_HEADS, D_K, D_V), F32),
                   jax.ShapeDtypeStruct((bp, N_HEADS, D_K, D_V), F32)],
        scratch_shapes=[pltpu.VMEM((2, tl, N_MAIN), F32), pltpu.VMEM((2, tl, N_QK), F32),
                        pltpu.VMEM((tl, D_MODEL), F32), pltpu.VMEM((tl, D_MODEL), F32),
                        pltpu.VMEM((tl, D_MODEL), BF16)],
        compiler_params=_params(("arbitrary",)),
        name="mixer_prompt",
    )(x_prompt.reshape(bp * lp, D_MODEL), x_prompt.reshape(bp * lp, D_MODEL), cos_p, sin_p, dm_p,
      wmain, wga, w2, b2, ggla, gret, wmix, gpre_mix, gpost_mix)
    x1p = x1p.reshape(bp, lp, D_MODEL)

    ts = bs * ls
    xs = x_sample.reshape(ts, D_MODEL)
    z_s, la_s = pl.pallas_call(
        _inproj_kernel,
        grid=(ts // ts_tile,),
        in_specs=[_row_spec(ts_tile, D_MODEL), _const_spec((D_MODEL, N_MAIN)),
                  _const_spec((D_MODEL, GATE_RANK_PAD)),
                  _const_spec((GATE_RANK_PAD, N_QK)), _const_spec((1, N_QK)), _const_spec((1, D_MODEL))],
        out_specs=[_row_spec(ts_tile, N_MAIN), _row_spec(ts_tile, N_QK)],
        out_shape=[jax.ShapeDtypeStruct((ts, N_MAIN), F32), jax.ShapeDtypeStruct((ts, N_QK), F32)],
        compiler_params=_params(("arbitrary",)),
        name="inproj_sample",
    )(xs, wmain, wga, w2, b2, gpre_mix)

    xa_per_row = lp // tm
    mem_spec = pl.BlockSpec((1, N_MEM, D_MODEL), lambda s: (s // xa_per_row, 0, 0))
    st_spec = pl.BlockSpec((nb_mix, N_HEADS, D_K, D_V), lambda i: (i, 0, 0, 0))
    assert D_MODEL % (16 * n_mlp_steps) == 0
    wup_spec = _row_spec(D_MODEL // n_mlp_steps, D_FF)
    wdown_spec = _row_spec(D_FF // n_mlp_steps, D_MODEL)
    memkv_spec = pl.BlockSpec((N_MEM, N_HEADS, XA_DH), lambda s: (s // xa_per_row, 0, 0))
    x2p, mk, mv, merged_s, sg_s, sr_s, wup, wdown = pl.pallas_call(
        functools.partial(_xattn_prompt_mixer_sample_kernel, nb=nb_mix, ls=ls, steps_per_row=xa_per_row),
        grid=(n_mlp_steps,),
        in_specs=[_row_spec(tm, D_MODEL), mem_spec, _const_spec((1, D_MODEL)),
                  _const_spec((D_MODEL, D_MODEL)), _const_spec((D_MODEL, D_MODEL)),
                  _const_spec((D_MODEL, D_MODEL)), _const_spec((D_MODEL, D_MODEL)),
                  _const_spec((1, D_MODEL)), _const_spec((1, D_MODEL)),
                  _row_spec(rows_mix, N_MAIN), _row_spec(rows_mix, N_QK),
                  _const_spec((rows_mix, D_K)), _const_spec((rows_mix, D_K)),
                  _const_spec((N_HEADS, rows_mix, rows_mix)), st_spec, st_spec,
                  _const_spec((1, D_MODEL)), _const_spec((1, D_MODEL)),
                  wup_spec, wdown_spec],
        out_specs=[_row_spec(tm, D_MODEL), memkv_spec, memkv_spec, _row_spec(rows_mix, D_MODEL), st_spec, st_spec,
                   wup_spec, wdown_spec],
        out_shape=[jax.ShapeDtypeStruct((bp * lp, D_MODEL), F32),
                   jax.ShapeDtypeStruct((bp * N_MEM, N_HEADS, XA_DH), F32),
                   jax.ShapeDtypeStruct((bp * N_MEM, N_HEADS, XA_DH), F32),
                   jax.ShapeDtypeStruct((ts, D_MODEL), BF16),
                   jax.ShapeDtypeStruct((bs, N_HEADS, D_K, D_V), F32),
                   jax.ShapeDtypeStruct((bs, N_HEADS, D_K, D_V), F32),
                   jax.ShapeDtypeStruct((D_MODEL, D_FF), BF16),
                   jax.ShapeDtypeStruct((D_FF, D_MODEL), BF16)],
        scratch_shapes=[pltpu.VMEM((1, N_MEM, D_MODEL), BF16), pltpu.VMEM((1, N_MEM, D_MODEL), BF16),
                        pltpu.VMEM((tm, D_MODEL), BF16),
                        pltpu.VMEM((rows_mix, D_MODEL), F32), pltpu.VMEM((rows_mix, D_MODEL), F32)],
        compiler_params=_params(("arbitrary",)),
        name="xattn_prompt_mixer_sample",
    )(x1p.reshape(bp * lp, D_MODEL), mem_prompt, gmem, wxk, wxv,
      wxq, wxo, gpre_xa, gpost_xa,
      z_s, la_s, cos_s, sin_s, dm_s, state_gla[0], state_ret[0], ggla, gret, w_up[0], w_down[0])

    x1s, q_s = pl.pallas_call(
        _post_mix_sample_kernel,
        grid=(ts // ts_tile,),
        in_specs=[_row_spec(ts_tile, D_MODEL), _row_spec(ts_tile, D_MODEL), _const_spec((D_MODEL, D_MODEL)),
                  _const_spec((1, D_MODEL)), _const_spec((1, D_MODEL)), _const_spec((D_MODEL, D_MODEL))],
        out_specs=[_row_spec(ts_tile, D_MODEL)] * 2,
        out_shape=[jax.ShapeDtypeStruct((ts, D_MODEL), F32)] * 2,
        compiler_params=_params(("arbitrary",)),
        name="post_mix_sample",
    )(merged_s, xs, wmix, gpost_mix, gpre_xa, wxq)

    kv_spec = pl.BlockSpec((nb_mix, N_MEM, N_HEADS, XA_DH), lambda i: (i, 0, 0, 0))
    yp, o_s = pl.pallas_call(
        functools.partial(_mlp_xattn_sample_kernel, nb=nb_mix, ls=ls),
        grid=(n_mlp_steps,),
        in_specs=[_row_spec(tm, D_MODEL), _const_spec((D_MODEL, D_FF)), _const_spec((D_FF, D_MODEL)),
                  _const_spec((1, D_MODEL)), _const_spec((1, D_MODEL)),
                  _row_spec(rows_mix, D_MODEL), kv_spec, kv_spec],
        out_specs=[_row_spec(tm, D_MODEL), _row_spec(rows_mix, D_MODEL)],
        out_shape=[jax.ShapeDtypeStruct((bp * lp, D_MODEL), F32), jax.ShapeDtypeStruct((ts, D_MODEL), BF16)],
        compiler_params=_params(("arbitrary",)),
        name="mlp_prompt_xattn_sample",
    )(x2p, wup, wdown, gpre_ffn, gpost_ffn, q_s, cache_mem_k[0], cache_mem_v[0])
    yp = yp.reshape(bp, lp, D_MODEL)

    ys = pl.pallas_call(
        _post_xa_mlp_sample_kernel,
        grid=(ts // ts_tile,),
        in_specs=[_row_spec(ts_tile, D_MODEL), _row_spec(ts_tile, D_MODEL), _const_spec((D_MODEL, D_MODEL)),
                  _const_spec((1, D_MODEL)), _const_spec((D_MODEL, D_FF)), _const_spec((D_FF, D_MODEL)),
                  _const_spec((1, D_MODEL)), _const_spec((1, D_MODEL))],
        out_specs=_row_spec(ts_tile, D_MODEL),
        out_shape=jax.ShapeDtypeStruct((ts, D_MODEL), F32),
        scratch_shapes=[pltpu.VMEM((ts_tile, D_MODEL), F32)],
        compiler_params=_params(("arbitrary",)),
        name="post_xa_mlp_sample",
    )(o_s, x1s, wxo, gpost_xa, wup, wdown, gpre_ffn, gpost_ffn).reshape(bs, ls, D_MODEL)

    hshape = (1, bp, N_MEM, N_HEADS, XA_DH)
    return (yp, ys, sg_p[None], sr_p[None], mk.reshape(hshape), mv.reshape(hshape), sg_s[None], sr_s[None])
```

```python
import functools
import math

import jax
import jax.numpy as jnp
from jax import lax
from jax.experimental import pallas as pl
from jax.experimental.pallas import tpu as pltpu

F32 = jnp.float32
BF16 = jnp.bfloat16

D_MODEL = 1024
N_HEADS = 4
D_K = 128
D_V = 256
N_MEM = 256
XA_DH = 256
D_FF = 4 * D_MODEL
GATE_RANK = 16
GATE_RANK_PAD = 128
GLA_TAU = 16.0
CHUNK = 64
ROPE_BASE = 10000.0
PAST_LEN = 16384
EPS = 1e-6
LOG_GAMMA = tuple(math.log1p(-(2.0 ** (-5.0 - h))) for h in range(N_HEADS))

C_GQ, C_GK, C_GV, C_GR = 0, 512, 1024, 2048
C_RQ, C_RK, C_RV, C_RG = 3072, 3584, 4096, 5120
C_GA, C_GB = 6144, 7168
N_MAIN = 8192
N_WA = 3072
N_WB = N_MAIN - N_WA
N_QK = N_HEADS * D_K
PROJ_BLOCK = 512
MLP_BLOCK = 1024
SPLIT_BLOCK = 1024

LANE = 128
VMEM_LIMIT_BYTES = 60 * 1024 * 1024


def _mm(a, b):
    return jnp.dot(a, b, preferred_element_type=F32)


def _mm_nt(a, b):
    return lax.dot_general(a, b, (((1,), (1,)), ((), ())), preferred_element_type=F32)


def _mm_tn(a, b):
    return lax.dot_general(a, b, (((0,), (0,)), ((), ())), preferred_element_type=F32)


def _rms(x, g):
    return x * lax.rsqrt(jnp.mean(x * x, axis=-1, keepdims=True) + EPS) * g


def _sigmoid(x):
    return 1.0 / (1.0 + jnp.exp(-x))


def _const_spec(shape):
    nd = len(shape)
    return pl.BlockSpec(shape, lambda *_: (0,) * nd, pipeline_mode=pl.Buffered(1))


def _params(sem):
    return pltpu.CompilerParams(dimension_semantics=sem, vmem_limit_bytes=VMEM_LIMIT_BYTES)


def _inproj_pieces(h_bf, wmain_ref, wga_ref, w2_ref, b2_ref, z_ref, la_ref):
    def z_block(j):
        def run():
            z_ref[:, j:j + PROJ_BLOCK] = _mm(h_bf, wmain_ref[:, j:j + PROJ_BLOCK])
        return run

    def gate():
        ga = _mm(h_bf, wga_ref[...])
        xg = _mm(ga.astype(BF16), w2_ref[...]) + b2_ref[...]
        log_sig = jnp.minimum(xg, 0.0) - jnp.log1p(jnp.exp(-jnp.abs(xg)))
        la_ref[...] = log_sig * (1.0 / GLA_TAU)

    return [z_block(j) for j in range(0, N_MAIN, PROJ_BLOCK)] + [gate]


def _inproj(h_bf, wmain_ref, wga_ref, w2_ref, b2_ref, z_ref, la_ref):
    for piece in _inproj_pieces(h_bf, wmain_ref, wga_ref, w2_ref, b2_ref, z_ref, la_ref):
        piece()


def _run_next(pieces):
    if pieces:
        pieces.pop(0)()


def _block_causal(n, c):
    shift = c.bit_length() - 1
    row = lax.broadcasted_iota(jnp.int32, (n, n), 0)
    col = lax.broadcasted_iota(jnp.int32, (n, n), 1)
    return ((row >> shift) == (col >> shift)) & (row >= col)


def _gla_tile(z_ref, la_ref, n, c, s_get, s_put, g_ref, o_ref, between=()):
    causal = _block_causal(n, c)
    tri = jnp.where(causal, 1.0, 0.0).astype(BF16)
    la = la_ref[...]
    la_hi = la.astype(BF16)
    la_lo = (la - la_hi.astype(F32)).astype(BF16)
    b = _mm(tri, la_hi) + _mm(tri, la_lo)
    n_chunks = n // c
    b_last = [b[(ci + 1) * c - 1:(ci + 1) * c, :] for ci in range(n_chunks)]
    for h in range(N_HEADS):
        ks = slice(h * D_K, (h + 1) * D_K)
        q = z_ref[:, C_GQ + h * D_K:C_GQ + (h + 1) * D_K] * (D_K ** -0.5)
        k = z_ref[:, C_GK + h * D_K:C_GK + (h + 1) * D_K]
        v = z_ref[:, C_GV + h * D_V:C_GV + (h + 1) * D_V].astype(BF16)
        b_h = b[:, ks]
        b_last_h = jnp.concatenate([jnp.broadcast_to(bl[:, ks], (c, D_K)) for bl in b_last], axis=0)
        qe = (q * jnp.exp(b_h)).astype(BF16)
        ke = (k * jnp.exp(-b_h)).astype(BF16)
        kd = (k * jnp.exp(b_last_h - b_h)).astype(BF16)
        sc = jnp.where(causal, _mm_nt(qe, ke), 0.0).astype(BF16)
        o_intra = _mm(sc, v)
        _run_next(between)
        outs = []
        for ci in range(n_chunks):
            r = slice(ci * c, (ci + 1) * c)
            s_old = s_get(ci, h)
            outs.append(o_intra[r] + _mm(qe[r], s_old.astype(BF16)))
            dcol = jnp.transpose(jnp.broadcast_to(jnp.exp(b_last[ci][:, ks]), (D_K, D_K)))
            s_put(ci, h, jnp.concatenate([dcol, dcol], axis=1) * s_old + _mm_tn(kd[r], v[r]))
        o = jnp.concatenate(outs, axis=0) if n_chunks > 1 else outs[0]
        ms = jnp.mean(o * o, axis=-1, keepdims=True)
        o_ref[:, h * D_V:(h + 1) * D_V] = o * lax.rsqrt(ms + EPS) * g_ref[:, h * D_V:(h + 1) * D_V]
        _run_next(between)


def _ret_tile(z_ref, cos, sin, n, c, s_get, s_put, g_ref, o_ref, dm_ref, between=()):
    tpos = (lax.broadcasted_iota(jnp.int32, (n, 1), 0) & (c - 1)).astype(F32)
    n_chunks = n // c
    for h in range(N_HEADS):
        lg = LOG_GAMMA[h]
        q = z_ref[:, C_RQ + h * D_K:C_RQ + (h + 1) * D_K]
        k = z_ref[:, C_RK + h * D_K:C_RK + (h + 1) * D_K]
        v = z_ref[:, C_RV + h * D_V:C_RV + (h + 1) * D_V].astype(BF16)
        q = q * cos + pltpu.roll(q, D_K // 2, 1) * sin
        k = (k * cos + pltpu.roll(k, D_K // 2, 1) * sin) * (D_K ** -0.5)
        sc = (_mm_nt(q.astype(BF16), k.astype(BF16)) * dm_ref[h]).astype(BF16)
        o_intra = _mm(sc, v)
        _run_next(between)
        qd = (q * jnp.exp(lg * (tpos + 1.0))).astype(BF16)
        kd = (k * jnp.exp(lg * (float(c - 1) - tpos))).astype(BF16)
        outs = []
        for ci in range(n_chunks):
            r = slice(ci * c, (ci + 1) * c)
            s_old = s_get(ci, h)
            outs.append(o_intra[r] + _mm(qd[r], s_old.astype(BF16)))
            s_put(ci, h, math.exp(lg * c) * s_old + _mm_tn(kd[r], v[r]))
        o = jnp.concatenate(outs, axis=0) if n_chunks > 1 else outs[0]
        mu = jnp.mean(o, axis=-1, keepdims=True)
        oc = o - mu
        var = jnp.mean(oc * oc, axis=-1, keepdims=True)
        o_ref[:, h * D_V:(h + 1) * D_V] = oc * lax.rsqrt(var + EPS) * g_ref[:, h * D_V:(h + 1) * D_V]
        _run_next(between)


def _merge(z_ref, og_ref, or_ref, merged_ref):
    for j in range(0, D_MODEL, 256):
        cs = slice(j, j + 256)
        gr = z_ref[:, C_GR + j:C_GR + j + 256]
        rg = z_ref[:, C_RG + j:C_RG + j + 256]
        ga = z_ref[:, C_GA + j:C_GA + j + 256]
        gb = z_ref[:, C_GB + j:C_GB + j + 256]
        o_g = og_ref[:, cs] * (gr * _sigmoid(gr))
        o_r = or_ref[:, cs] * (rg * _sigmoid(rg))
        merged_ref[:, cs] = (_sigmoid(ga) * o_g + _sigmoid(gb) * o_r).astype(merged_ref.dtype)


def _split_w_in_kernel(wt_ref, wtg_ref, w_ref, wga_ref):
    w_ref[...] = jnp.transpose(wt_ref[...]).astype(BF16)
    zeros = jnp.zeros((GATE_RANK_PAD - GATE_RANK, D_MODEL), F32)
    wga_ref[...] = jnp.transpose(jnp.concatenate([wtg_ref[...], zeros], axis=0)).astype(BF16)


def _memkv_kernel(m_ref, g_ref, wk_ref, wv_ref, k_ref, v_ref, kb_ref, vb_ref):
    m = _rms(m_ref[...], g_ref[...]).astype(BF16)
    k = _mm(m, wk_ref[...])
    v = _mm(m, wv_ref[...])
    k_ref[...] = k.reshape(k_ref.shape)
    v_ref[...] = v.reshape(v_ref.shape)
    kb_ref[...] = k.astype(BF16)
    vb_ref[...] = v.astype(BF16)


def _mixer_prompt_kernel(xc_ref, xn_ref, cos_ref, sin_ref, dm_ref, wmain_ref, wga_ref, w2_ref, b2_ref,
                         ggla_ref, gret_ref, wmix_ref, gpre_ref, gpost_ref,
                         y_ref, sg_ref, sr_ref,
                         z_ref, la_ref, og_ref, or_ref, merged_ref, *, tl, steps_per_row):
    step = pl.program_id(0)

    @pl.when(step % steps_per_row == 0)
    def _():
        sg_ref[...] = jnp.zeros_like(sg_ref)
        sr_ref[...] = jnp.zeros_like(sr_ref)

    def project_pieces(x, slot):
        h_bf = _rms(x, gpre_ref[...]).astype(BF16)
        return _inproj_pieces(h_bf, wmain_ref, wga_ref, w2_ref, b2_ref, z_ref.at[slot], la_ref.at[slot])

    @pl.when(step == 0)
    def _():
        for piece in project_pieces(xc_ref[0:tl, :], 0):
            piece()

    n_chunks = tl // CHUNK
    for half in range(2):
        rows = slice(half * tl, (half + 1) * tl)
        pending = project_pieces(xc_ref[tl:2 * tl, :] if half == 0 else xn_ref[...], 1 - half)
        _run_next(pending)
        z_cur, la_cur = z_ref.at[half], la_ref.at[half]

        carried = {}

        def gla_get(ci, h, carried=carried):
            return sg_ref[0, h] if ci == 0 else carried[h]

        def gla_put(ci, h, val, carried=carried):
            carried[h] = val
            if ci == n_chunks - 1:
                sg_ref[0, h] = val

        _gla_tile(z_cur, la_cur, tl, CHUNK, gla_get, gla_put, ggla_ref, og_ref, between=pending)

        def ret_put(ci, h, val):
            sr_ref[0, h] = val

        _ret_tile(z_cur, cos_ref[rows, :], sin_ref[rows, :], tl, tl, lambda ci, h: sr_ref[0, h], ret_put,
                  gret_ref, or_ref, dm_ref, between=pending)
        while pending:
            _run_next(pending)

        _merge(z_cur, og_ref, or_ref, merged_ref)
        m = _mm(merged_ref[...], wmix_ref[...])
        y_ref[rows, :] = xc_ref[rows, :] + _rms(m, gpost_ref[...])


def _staggered(a_stages, b_stages):
    a, b = list(a_stages), list(b_stages)
    order = [a.pop(0)]
    while a or b:
        order += a[:1] + b[:1]
        a, b = a[1:], b[1:]
    return order


def _xattn_stages(x_ref, rows, mk_ref, mv_ref, wq_ref, wo_ref, gpre_ref, gpost_ref, y_ref, o_ref):
    st = {}

    def norm():
        st["hx"] = _rms(x_ref[rows, :], gpre_ref[...]).astype(BF16)

    def q_proj():
        st["q"] = _mm(st["hx"], wq_ref[...])

    def heads():
        for h in range(N_HEADS):
            hs = slice(h * XA_DH, (h + 1) * XA_DH)
            s = _mm_nt(st["q"][:, hs].astype(BF16), mk_ref[0, :, hs]) * (XA_DH ** -0.5)
            p = jnp.exp(s - jnp.max(s, axis=-1, keepdims=True))
            p = p * (1.0 / jnp.sum(p, axis=-1, keepdims=True))
            o_ref[rows, hs] = _mm(p.astype(BF16), mv_ref[0, :, hs]).astype(o_ref.dtype)

    def out_proj():
        st["a"] = _mm(o_ref[rows, :], wo_ref[...])

    def post():
        y_ref[rows, :] = x_ref[rows, :] + _rms(st["a"], gpost_ref[...])

    return [norm, q_proj, heads, out_proj, post]


def _mlp_stages(x_ref, rows, wup_ref, wdown_ref, gpre_ref, gpost_ref, y_ref):
    st = {}

    def norm():
        st["h"] = _rms(x_ref[rows, :], gpre_ref[...]).astype(BF16)

    def hidden_block(j):
        def run():
            u = jnp.maximum(_mm(st["h"], wup_ref[:, j:j + MLP_BLOCK]), 0.0)
            part = _mm((u * u).astype(BF16), wdown_ref[j:j + MLP_BLOCK, :])
            st["acc"] = part if j == 0 else st["acc"] + part
        return run

    def post():
        y_ref[rows, :] = x_ref[rows, :] + _rms(st["acc"], gpost_ref[...])

    return [norm] + [hidden_block(j) for j in range(0, D_FF, MLP_BLOCK)] + [post]


def _mlp_halves(x_ref, wup_ref, wdown_ref, gpre_ref, gpost_ref, y_ref):
    half = x_ref.shape[0] // 2
    args = (wup_ref, wdown_ref, gpre_ref, gpost_ref, y_ref)
    return _staggered(_mlp_stages(x_ref, slice(0, half), *args), _mlp_stages(x_ref, slice(half, 2 * half), *args))


def _mlp_kernel(x_ref, wup_ref, wdown_ref, gpre_ref, gpost_ref, y_ref):
    for stage in _mlp_halves(x_ref, wup_ref, wdown_ref, gpre_ref, gpost_ref, y_ref):
        stage()


def _inproj_kernel(x_ref, wmain_ref, wga_ref, w2_ref, b2_ref, gpre_ref, z_ref, la_ref):
    h_bf = _rms(x_ref[...], gpre_ref[...]).astype(BF16)
    _inproj(h_bf, wmain_ref, wga_ref, w2_ref, b2_ref, z_ref, la_ref)


def _mixer_sample_body(z_ref, la_ref, cos_ref, sin_ref, dm_ref, sgi_ref, sri_ref, ggla_ref, gret_ref,
                       merged_ref, sgo_ref, sro_ref, og_ref, or_ref, nb, ls, between=()):
    def gla_put(ci, h, val):
        sgo_ref[ci, h] = val

    def ret_put(ci, h, val):
        sro_ref[ci, h] = val

    _gla_tile(z_ref, la_ref, nb * ls, ls, lambda ci, h: sgi_ref[ci, h], gla_put, ggla_ref, og_ref, between=between)
    _ret_tile(z_ref, cos_ref[...], sin_ref[...], nb * ls, ls, lambda ci, h: sri_ref[ci, h], ret_put,
              gret_ref, or_ref, dm_ref, between=between)
    _merge(z_ref, og_ref, or_ref, merged_ref)


def _xattn_prompt_mixer_sample_kernel(x_ref, mk_ref, mv_ref, wq_ref, wo_ref, gpre_ref, gpost_ref,
                                      z_ref, la_ref, cos_ref, sin_ref, dm_ref, sgi_ref, sri_ref, ggla_ref, gret_ref,
                                      wup_f_ref, wdown_f_ref,
                                      y_ref, merged_ref, sgo_ref, sro_ref, wup_b_ref, wdown_b_ref,
                                      o_ref, og_ref, or_ref, *, nb, ls):
    wup_b_ref[...] = wup_f_ref[...].astype(BF16)
    wdown_b_ref[...] = wdown_f_ref[...].astype(BF16)
    half = x_ref.shape[0] // 2
    args = (mk_ref, mv_ref, wq_ref, wo_ref, gpre_ref, gpost_ref, y_ref, o_ref)
    pending = _staggered(_xattn_stages(x_ref, slice(0, half), *args),
                         _xattn_stages(x_ref, slice(half, 2 * half), *args))
    _mixer_sample_body(z_ref, la_ref, cos_ref, sin_ref, dm_ref, sgi_ref, sri_ref, ggla_ref, gret_ref,
                       merged_ref, sgo_ref, sro_ref, og_ref, or_ref, nb, ls)
    while pending:
        _run_next(pending)


def _post_mix_sample_kernel(a_ref, res_ref, wmix_ref, gpost_ref, gpre_xa_ref, wq_ref, x1_ref, q_ref):
    x1 = res_ref[...] + _rms(_mm(a_ref[...], wmix_ref[...]), gpost_ref[...])
    x1_ref[...] = x1
    q_ref[...] = _mm(_rms(x1, gpre_xa_ref[...]).astype(BF16), wq_ref[...])


def _post_xa_mlp_sample_kernel(o_ref, x1_ref, wo_ref, gpost_xa_ref, wup_ref, wdown_ref, gpre_ref, gpost_ref,
                               y_ref, x2_ref):
    x2_ref[...] = x1_ref[...] + _rms(_mm(o_ref[...], wo_ref[...]), gpost_xa_ref[...])
    _mlp_kernel(x2_ref, wup_ref, wdown_ref, gpre_ref, gpost_ref, y_ref)


def _xattn_sample_body(q_ref, k_ref, v_ref, o_ref, nb, ls, between=()):
    n_rows = N_HEADS * ls
    n_cols = N_MEM * N_HEADS
    row_head = lax.broadcasted_iota(jnp.int32, (n_rows, n_cols), 0) // ls
    col_head = lax.broadcasted_iota(jnp.int32, (n_rows, n_cols), 1) & (N_HEADS - 1)
    own = row_head == col_head

    for e in range(nb):
        rows = slice(e * ls, (e + 1) * ls)
        q = q_ref[rows, :]
        q_hm = jnp.concatenate([q[:, h * XA_DH:(h + 1) * XA_DH] for h in range(N_HEADS)], axis=0)
        k_all = k_ref[e].reshape(n_cols, XA_DH)
        v_all = v_ref[e].reshape(n_cols, XA_DH)
        s = _mm_nt(q_hm.astype(BF16), k_all.astype(BF16)) * (XA_DH ** -0.5)
        s = jnp.where(own, s, -jnp.inf)
        p = jnp.exp(s - jnp.max(s, axis=-1, keepdims=True))
        p = p * (1.0 / jnp.sum(p, axis=-1, keepdims=True))
        _run_next(between)
        o = _mm(p.astype(BF16), v_all.astype(BF16))
        for h in range(N_HEADS):
            o_ref[rows, h * XA_DH:(h + 1) * XA_DH] = o[h * ls:(h + 1) * ls, :].astype(o_ref.dtype)
        _run_next(between)
        _run_next(between)


def _mlp_xattn_sample_kernel(x_ref, wup_ref, wdown_ref, gpre_ref, gpost_ref, q_ref, k_ref, v_ref,
                             y_ref, o_ref, *, nb, ls):
    pending = _mlp_halves(x_ref, wup_ref, wdown_ref, gpre_ref, gpost_ref, y_ref)
    _xattn_sample_body(q_ref, k_ref, v_ref, o_ref, nb, ls, between=pending)
    while pending:
        _run_next(pending)


def _rope_tables(pos):
    half = D_K // 2
    inv = ROPE_BASE ** (-jnp.arange(half, dtype=F32) / half)
    ang = pos.astype(F32)[:, None] * inv[None, :]
    cos, sin = jnp.cos(ang), jnp.sin(ang)
    return jnp.concatenate([cos, cos], axis=-1), jnp.concatenate([-sin, sin], axis=-1)


def _decay_masks(n, c):
    t = jnp.arange(n, dtype=jnp.int32)
    keep = ((t[:, None] // c) == (t[None, :] // c)) & (t[:, None] >= t[None, :])
    dist = (t[:, None] - t[None, :]).astype(F32)
    lg = jnp.asarray(LOG_GAMMA, F32)[:, None, None]
    return jnp.where(keep[None], jnp.exp(lg * dist[None]), 0.0)


def _row_spec(tm, n):
    return pl.BlockSpec((tm, n), lambda i: (i, 0))


def kernel(x_prompt, x_sample, state_gla, state_ret, cache_mem_k, cache_mem_v, mem_prompt, w_in, w_gla_a2, b_gla_a, g_gla_head, g_ret_head, w_mix_out, w_xq, w_xk, w_xv, w_xo, g_mem, w_up, w_down, g_pre_mix, g_post_mix, g_pre_xa, g_post_xa, g_pre_ffn, g_post_ffn):
    depth = w_in.shape[0]
    assert depth == 1
    bp, lp, _ = x_prompt.shape
    bs, ls, _ = x_sample.shape
    tl = 256
    tm = 512
    ts_tile = 256
    assert lp % (2 * tl) == 0 and tl % CHUNK == 0 and lp % tm == 0
    n_mlp_steps = bp * lp // tm
    assert bs % n_mlp_steps == 0
    nb_mix = bs // n_mlp_steps
    assert (bs * ls) % ts_tile == 0 and (nb_mix * ls) % 16 == 0
    assert ls % 8 == 0 and ls <= CHUNK and ls & (ls - 1) == 0 and tl & (tl - 1) == 0
    assert (bp * N_MEM) % tm == 0

    w = w_in[0]
    c_low = 2 * N_QK + 2 * D_MODEL
    assert c_low == N_WA and w.shape[1] == N_MAIN + GATE_RANK
    wt = jnp.transpose(w)
    n_wa_blocks = N_WA // SPLIT_BLOCK

    def wt_row(i):
        units = SPLIT_BLOCK // GATE_RANK
        return GATE_RANK * jnp.where(i < n_wa_blocks, i * units, i * units + 1)

    wmain, wga = pl.pallas_call(
        _split_w_in_kernel,
        grid=(N_MAIN // SPLIT_BLOCK,),
        in_specs=[pl.BlockSpec((pl.Element(SPLIT_BLOCK), pl.Element(D_MODEL)), lambda i: (wt_row(i), 0)),
                  pl.BlockSpec((GATE_RANK, D_MODEL), lambda i: (N_WA // GATE_RANK, 0))],
        out_specs=[pl.BlockSpec((D_MODEL, SPLIT_BLOCK), lambda i: (0, i)),
                   pl.BlockSpec((D_MODEL, GATE_RANK_PAD), lambda i: (0, 0))],
        out_shape=[jax.ShapeDtypeStruct((D_MODEL, N_MAIN), BF16),
                   jax.ShapeDtypeStruct((D_MODEL, GATE_RANK_PAD), BF16)],
        compiler_params=_params(("arbitrary",)),
        name="split_w_in",
    )(wt, wt)
    w2 = jnp.pad(w_gla_a2[0], ((0, GATE_RANK_PAD - GATE_RANK), (0, 0))).astype(BF16)
    b2 = b_gla_a[0].reshape(1, N_QK)
    ggla = g_gla_head[0].reshape(1, D_MODEL)
    gret = g_ret_head[0].reshape(1, D_MODEL)
    wmix = w_mix_out[0].astype(BF16)
    wxq, wxk, wxv, wxo = (t[0].astype(BF16) for t in (w_xq, w_xk, w_xv, w_xo))
    row = lambda g: g[0].reshape(1, D_MODEL)
    gmem, gpre_mix, gpost_mix, gpre_xa, gpost_xa, gpre_ffn, gpost_ffn = (
        row(g) for g in (g_mem, g_pre_mix, g_post_mix, g_pre_xa, g_post_xa, g_pre_ffn, g_post_ffn))
    cos_p, sin_p = _rope_tables(jnp.arange(lp, dtype=jnp.int32))
    cos_s, sin_s = _rope_tables(PAST_LEN + jnp.arange(ls, dtype=jnp.int32))
    rows_mix = nb_mix * ls
    cos_s, sin_s = jnp.tile(cos_s, (nb_mix, 1)), jnp.tile(sin_s, (nb_mix, 1))
    dm_p = _decay_masks(tl, tl)
    dm_s = _decay_masks(rows_mix, ls)

    nmem_rows = bp * N_MEM
    mk, mv, mk_bf, mv_bf = pl.pallas_call(
        _memkv_kernel,
        grid=(nmem_rows // tm,),
        in_specs=[_row_spec(tm, D_MODEL), _const_spec((1, D_MODEL)),
                  _const_spec((D_MODEL, D_MODEL)), _const_spec((D_MODEL, D_MODEL))],
        out_specs=[pl.BlockSpec((tm, N_HEADS, XA_DH), lambda i: (i, 0, 0))] * 2 + [_row_spec(tm, D_MODEL)] * 2,
        out_shape=[jax.ShapeDtypeStruct((nmem_rows, N_HEADS, XA_DH), F32)] * 2
        + [jax.ShapeDtypeStruct((nmem_rows, D_MODEL), BF16)] * 2,
        compiler_params=_params(("arbitrary",)),
        name="memkv",
    )(mem_prompt.reshape(nmem_rows, D_MODEL), gmem, wxk, wxv)

    n_tiles = bp * lp // tl
    steps_per_row = lp // (2 * tl)
    state_spec = pl.BlockSpec((1, N_HEADS, D_K, D_V), lambda s: (s // steps_per_row, 0, 0, 0))
    rope_spec = pl.BlockSpec((2 * tl, D_K), lambda s: (s % steps_per_row, 0))
    x1p, sg_p, sr_p = pl.pallas_call(
        functools.partial(_mixer_prompt_kernel, tl=tl, steps_per_row=steps_per_row),
        grid=(n_tiles // 2,),
        in_specs=[_row_spec(2 * tl, D_MODEL),
                  pl.BlockSpec((tl, D_MODEL), lambda s: (jnp.minimum(2 * s + 2, n_tiles - 1), 0)),
                  rope_spec, rope_spec,
                  _const_spec((N_HEADS, tl, tl)),
                  _const_spec((D_MODEL, N_MAIN)), _const_spec((D_MODEL, GATE_RANK_PAD)),
                  _const_spec((GATE_RANK_PAD, N_QK)), _const_spec((1, N_QK)),
                  _const_spec((1, D_MODEL)), _const_spec((1, D_MODEL)),
                  _const_spec((D_MODEL, D_MODEL)), _const_spec((1, D_MODEL)), _const_spec((1, D_MODEL))],
        out_specs=[_row_spec(2 * tl, D_MODEL), state_spec, state_spec],
        out_shape=[jax.ShapeDtypeStruct((bp * lp, D_MODEL), F32),
                   jax.ShapeDtypeStruct((bp, N_HEADS, D_K, D_V), F32),
                   jax.ShapeDtypeStruct((bp, N_HEADS, D_K, D_V), F32)],
        scratch_shapes=[pltpu.VMEM((2, tl, N_MAIN), F32), pltpu.VMEM((2, tl, N_QK), F32),
                        pltpu.VMEM((tl, D_MODEL), F32), pltpu.VMEM((tl, D_MODEL), F32),
                        pltpu.VMEM((tl, D_MODEL), BF16)],
        compiler_params=_params(("arbitrary",)),
        name="mixer_prompt",
    )(x_prompt.reshape(bp * lp, D_MODEL), x_prompt.reshape(bp * lp, D_MODEL), cos_p, sin_p, dm_p,
      wmain, wga, w2, b2, ggla, gret, wmix, gpre_mix, gpost_mix)
    x1p = x1p.reshape(bp, lp, D_MODEL)

    ts = bs * ls
    xs = x_sample.reshape(ts, D_MODEL)
    z_s, la_s = pl.pallas_call(
        _inproj_kernel,
        grid=(ts // ts_tile,),
        in_specs=[_row_spec(ts_tile, D_MODEL), _const_spec((D_MODEL, N_MAIN)),
                  _const_spec((D_MODEL, GATE_RANK_PAD)),
                  _const_spec((GATE_RANK_PAD, N_QK)), _const_spec((1, N_QK)), _const_spec((1, D_MODEL))],
        out_specs=[_row_spec(ts_tile, N_MAIN), _row_spec(ts_tile, N_QK)],
        out_shape=[jax.ShapeDtypeStruct((ts, N_MAIN), F32), jax.ShapeDtypeStruct((ts, N_QK), F32)],
        compiler_params=_params(("arbitrary",)),
        name="inproj_sample",
    )(xs, wmain, wga, w2, b2, gpre_mix)

    xa_per_row = lp // tm
    mem_spec = pl.BlockSpec((1, N_MEM, D_MODEL), lambda s: (s // xa_per_row, 0, 0))
    st_spec = pl.BlockSpec((nb_mix, N_HEADS, D_K, D_V), lambda i: (i, 0, 0, 0))
    assert D_MODEL % (16 * n_mlp_steps) == 0
    wup_spec = _row_spec(D_MODEL // n_mlp_steps, D_FF)
    wdown_spec = _row_spec(D_FF // n_mlp_steps, D_MODEL)
    x2p, merged_s, sg_s, sr_s, wup, wdown = pl.pallas_call(
        functools.partial(_xattn_prompt_mixer_sample_kernel, nb=nb_mix, ls=ls),
        grid=(n_mlp_steps,),
        in_specs=[_row_spec(tm, D_MODEL), mem_spec, mem_spec,
                  _const_spec((D_MODEL, D_MODEL)), _const_spec((D_MODEL, D_MODEL)),
                  _const_spec((1, D_MODEL)), _const_spec((1, D_MODEL)),
                  _row_spec(rows_mix, N_MAIN), _row_spec(rows_mix, N_QK),
                  _const_spec((rows_mix, D_K)), _const_spec((rows_mix, D_K)),
                  _const_spec((N_HEADS, rows_mix, rows_mix)), st_spec, st_spec,
                  _const_spec((1, D_MODEL)), _const_spec((1, D_MODEL)),
                  wup_spec, wdown_spec],
        out_specs=[_row_spec(tm, D_MODEL), _row_spec(rows_mix, D_MODEL), st_spec, st_spec, wup_spec, wdown_spec],
        out_shape=[jax.ShapeDtypeStruct((bp * lp, D_MODEL), F32),
                   jax.ShapeDtypeStruct((ts, D_MODEL), BF16),
                   jax.ShapeDtypeStruct((bs, N_HEADS, D_K, D_V), F32),
                   jax.ShapeDtypeStruct((bs, N_HEADS, D_K, D_V), F32),
                   jax.ShapeDtypeStruct((D_MODEL, D_FF), BF16),
                   jax.ShapeDtypeStruct((D_FF, D_MODEL), BF16)],
        scratch_shapes=[pltpu.VMEM((tm, D_MODEL), BF16),
                        pltpu.VMEM((rows_mix, D_MODEL), F32), pltpu.VMEM((rows_mix, D_MODEL), F32)],
        compiler_params=_params(("arbitrary",)),
        name="xattn_prompt_mixer_sample",
    )(x1p.reshape(bp * lp, D_MODEL), mk_bf.reshape(bp, N_MEM, D_MODEL), mv_bf.reshape(bp, N_MEM, D_MODEL),
      wxq, wxo, gpre_xa, gpost_xa,
      z_s, la_s, cos_s, sin_s, dm_s, state_gla[0], state_ret[0], ggla, gret, w_up[0], w_down[0])

    x1s, q_s = pl.pallas_call(
        _post_mix_sample_kernel,
        grid=(ts // ts_tile,),
        in_specs=[_row_spec(ts_tile, D_MODEL), _row_spec(ts_tile, D_MODEL), _const_spec((D_MODEL, D_MODEL)),
                  _const_spec((1, D_MODEL)), _const_spec((1, D_MODEL)), _const_spec((D_MODEL, D_MODEL))],
        out_specs=[_row_spec(ts_tile, D_MODEL)] * 2,
        out_shape=[jax.ShapeDtypeStruct((ts, D_MODEL), F32)] * 2,
        compiler_params=_params(("arbitrary",)),
        name="post_mix_sample",
    )(merged_s, xs, wmix, gpost_mix, gpre_xa, wxq)

    kv_spec = pl.BlockSpec((nb_mix, N_MEM, N_HEADS, XA_DH), lambda i: (i, 0, 0, 0))
    yp, o_s = pl.pallas_call(
        functools.partial(_mlp_xattn_sample_kernel, nb=nb_mix, ls=ls),
        grid=(n_mlp_steps,),
        in_specs=[_row_spec(tm, D_MODEL), _const_spec((D_MODEL, D_FF)), _const_spec((D_FF, D_MODEL)),
                  _const_spec((1, D_MODEL)), _const_spec((1, D_MODEL)),
                  _row_spec(rows_mix, D_MODEL), kv_spec, kv_spec],
        out_specs=[_row_spec(tm, D_MODEL), _row_spec(rows_mix, D_MODEL)],
        out_shape=[jax.ShapeDtypeStruct((bp * lp, D_MODEL), F32), jax.ShapeDtypeStruct((ts, D_MODEL), BF16)],
        compiler_params=_params(("arbitrary",)),
        name="mlp_prompt_xattn_sample",
    )(x2p, wup, wdown, gpre_ffn, gpost_ffn, q_s, cache_mem_k[0], cache_mem_v[0])
    yp = yp.reshape(bp, lp, D_MODEL)

    ys = pl.pallas_call(
        _post_xa_mlp_sample_kernel,
        grid=(ts // ts_tile,),
        in_specs=[_row_spec(ts_tile, D_MODEL), _row_spec(ts_tile, D_MODEL), _const_spec((D_MODEL, D_MODEL)),
                  _const_spec((1, D_MODEL)), _const_spec((D_MODEL, D_FF)), _const_spec((D_FF, D_MODEL)),
                  _const_spec((1, D_MODEL)), _const_spec((1, D_MODEL))],
        out_specs=_row_spec(ts_tile, D_MODEL),
        out_shape=jax.ShapeDtypeStruct((ts, D_MODEL), F32),
        scratch_shapes=[pltpu.VMEM((ts_tile, D_MODEL), F32)],
        compiler_params=_params(("arbitrary",)),
        name="post_xa_mlp_sample",
    )(o_s, x1s, wxo, gpost_xa, wup, wdown, gpre_ffn, gpost_ffn).reshape(bs, ls, D_MODEL)

    hshape = (1, bp, N_MEM, N_HEADS, XA_DH)
    return (yp, ys, sg_p[None], sr_p[None], mk.reshape(hshape), mv.reshape(hshape), sg_s[None], sr_s[None])
```
